```python
import math
import jax, jax.numpy as jnp
from jax import lax
import numpy as np

D_MODEL = 1024
BATCH = 8
SEQ = 4096
DEPTH = 1

HEAD_DIM = 64
DILATED_GROUPS = ((128, 1), (512, 4), (2048, 16))
HEADS_PER_GROUP = 4
N_HEADS_A = HEADS_PER_GROUP * len(DILATED_GROUPS)
N_HEADS_B = 8
A_QKV = N_HEADS_A * HEAD_DIM
B_QKV = N_HEADS_B * HEAD_DIM
A_OUT = HEADS_PER_GROUP * HEAD_DIM
N_IN = 3 * A_QKV + 3 * B_QKV + 2 * D_MODEL
BLOCK = 128
NUM_BUCKETS = 32
MAX_DISTANCE = 2048
D_FF = 2816
CONV_WIDTH = 3
EPS = 1e-6

kernel_name = "hybrid_dilated_stickbreaking_block"


def rms_norm(x, g):
    xf = x.astype(jnp.float32)
    y = xf * lax.rsqrt(jnp.mean(xf * xf, axis=-1, keepdims=True) + EPS)
    return (y * g.astype(jnp.float32)).astype(x.dtype)


def t5_bucket(dist):
    max_exact = NUM_BUCKETS // 2
    n = np.asarray(dist, dtype=np.float32)
    large = max_exact + (np.log(np.maximum(n, 1.0) / max_exact)
                         / np.log(MAX_DISTANCE / max_exact)
                         * (NUM_BUCKETS - max_exact)).astype(np.int32)
    large = np.minimum(large, NUM_BUCKETS - 1)
    return np.where(n < max_exact, n.astype(np.int32), large).astype(np.int32)


def dilated_window_attention(q, k, v, bias, window, dilation):
    B, S, H, Dh = q.shape
    n_back = window // dilation
    unit = dilation * BLOCK
    Sp = -(-S // unit) * unit
    nb = Sp // unit

    def to_blocks(t):
        t = jnp.pad(t.astype(jnp.float32), ((0, 0), (0, Sp - S), (0, 0), (0, 0)))
        return t.reshape(B, nb, BLOCK, dilation, H, Dh)

    qb, kb, vb = to_blocks(q), to_blocks(k), to_blocks(v)

    def band(t):
        prev = jnp.concatenate([jnp.zeros_like(t[:, :1]), t[:, :-1]], axis=1)
        return jnp.concatenate([prev, t], axis=2)

    kband, vband = band(kb), band(vb)
    logits = jnp.einsum('bnirhe,bncrhe->bnrhic', qb, kband) * (1.0 / math.sqrt(Dh))

    i = np.arange(BLOCK)[:, None]
    c = np.arange(2 * BLOCK)[None, :]
    j = i + BLOCK - c
    local = (j >= 0) & (j <= n_back)
    key_m = np.arange(nb)[:, None, None] * BLOCK - BLOCK + c[None]
    mask = local[None] & (key_m >= 0)
    bias_band = bias.astype(jnp.float32)[:, np.clip(j, 0, n_back)]

    logits = jnp.where(mask[None, :, None, None], logits + bias_band[None, None, None], -jnp.inf)
    mx = jnp.max(logits, axis=-1, keepdims=True)
    e = jnp.exp(logits - mx)
    den = jnp.sum(e, axis=-1, keepdims=True)
    o = jnp.einsum('bnrhic,bncrhe->bnirhe', e, vband) / den.transpose(0, 1, 4, 2, 3, 5)
    lse = (mx + jnp.log(den))[..., 0].transpose(0, 1, 4, 2, 3)
    o = o.reshape(B, Sp, H, Dh)[:, :S]
    lse = lse.reshape(B, Sp, H)[:, :S]
    return o, lse


def stick_breaking_attention(q, k, v):
    B, S, H, Dh = q.shape
    nb = S // BLOCK
    kf = k.astype(jnp.float32)
    vf = v.astype(jnp.float32)
    qblocks = q.astype(jnp.float32).reshape(B, nb, BLOCK, H, Dh).transpose(1, 0, 2, 3, 4)
    pos_k = jnp.arange(S)

    def one_block(args):
        qblk, n = args
        z = jnp.einsum('bqhe,bkhe->bhqk', qblk, kf) * (1.0 / math.sqrt(Dh))
        t = n * BLOCK + jnp.arange(BLOCK)
        mask = pos_k[None, :] < t[:, None]
        log_beta = jax.nn.log_sigmoid(z)
        log_1m = jnp.where(mask, log_beta - z, 0.0)
        tail = lax.cumsum(log_1m, axis=3, reverse=True) - log_1m
        w = jnp.where(mask, jnp.exp(log_beta + tail), 0.0)
        return jnp.einsum('bhqk,bkhe->bqhe', w, vf)

    out = lax.map(one_block, (qblocks, jnp.arange(nb)))
    return out.transpose(1, 0, 2, 3, 4).reshape(B, S, H, Dh)


def setup_inputs(seed: int = 0) -> dict:
    key = jax.random.key(seed)
    ks = jax.random.split(key, 16)
    nrm = jax.random.normal
    x = nrm(ks[0], (BATCH, SEQ, D_MODEL), jnp.float32)
    g_pre_mix = 1.0 + 0.05 * nrm(ks[1], (DEPTH, D_MODEL), jnp.float32)
    w_in = nrm(ks[2], (DEPTH, D_MODEL, N_IN), jnp.float32) * D_MODEL ** -0.5
    rel_bias = 0.5 * nrm(ks[3], (NUM_BUCKETS, N_HEADS_A), jnp.float32)
    w_branch_a = nrm(ks[4], (DEPTH, A_OUT, D_MODEL), jnp.float32) * A_OUT ** -0.5
    w_branch_b = nrm(ks[5], (DEPTH, B_QKV, D_MODEL), jnp.float32) * B_QKV ** -0.5
    w_out = nrm(ks[6], (DEPTH, D_MODEL, D_MODEL), jnp.float32) * D_MODEL ** -0.5
    g_post_mix = 1.0 + 0.05 * nrm(ks[7], (DEPTH, D_MODEL), jnp.float32)
    g_pre_ffn = 1.0 + 0.05 * nrm(ks[8], (DEPTH, D_MODEL), jnp.float32)
    w_up = nrm(ks[9], (DEPTH, D_MODEL, 2 * D_FF), jnp.float32) * D_MODEL ** -0.5
    conv_w = nrm(ks[10], (DEPTH, CONV_WIDTH, 2 * D_FF), jnp.float32) * CONV_WIDTH ** -0.5
    conv_b = 0.02 * nrm(ks[11], (DEPTH, 2 * D_FF), jnp.float32)
    w_down = nrm(ks[12], (DEPTH, D_FF, D_MODEL), jnp.float32) * D_FF ** -0.5
    g_post_ffn = 1.0 + 0.05 * nrm(ks[13], (DEPTH, D_MODEL), jnp.float32)
    return {"x": x, "g_pre_mix": g_pre_mix, "w_in": w_in, "rel_bias": rel_bias,
            "w_branch_a": w_branch_a, "w_branch_b": w_branch_b, "w_out": w_out,
            "g_post_mix": g_post_mix, "g_pre_ffn": g_pre_ffn, "w_up": w_up,
            "conv_w": conv_w, "conv_b": conv_b, "w_down": w_down, "g_post_ffn": g_post_ffn}


def reference(x, g_pre_mix, w_in, rel_bias, w_branch_a, w_branch_b, w_out, g_post_mix,
              g_pre_ffn, w_up, conv_w, conv_b, w_down, g_post_ffn):
    B, S, D = x.shape
    splits = np.cumsum([A_QKV, A_QKV, A_QKV, B_QKV, B_QKV, B_QKV, D_MODEL]).tolist()
    for l in range(DEPTH):
        h = rms_norm(x, g_pre_mix[l])
        proj = h @ w_in[l]
        qa, ka, va, qb, kb, vb, gate_a, gate_b = jnp.split(proj, splits, axis=-1)
        qa = qa.reshape(B, S, N_HEADS_A, HEAD_DIM)
        ka = ka.reshape(B, S, N_HEADS_A, HEAD_DIM)
        va = va.reshape(B, S, N_HEADS_A, HEAD_DIM)
        qb = qb.reshape(B, S, N_HEADS_B, HEAD_DIM)
        kb = kb.reshape(B, S, N_HEADS_B, HEAD_DIM)
        vb = vb.reshape(B, S, N_HEADS_B, HEAD_DIM)

        outs, lses = [], []
        for g, (window, dilation) in enumerate(DILATED_GROUPS):
            sl = slice(g * HEADS_PER_GROUP, (g + 1) * HEADS_PER_GROUP)
            buckets = t5_bucket(np.arange(window // dilation + 1) * dilation)
            bias = rel_bias[buckets][:, sl].T
            o_g, lse_g = dilated_window_attention(qa[:, :, sl], ka[:, :, sl], va[:, :, sl],
                                                  bias, window, dilation)
            outs.append(o_g)
            lses.append(lse_g)
        wts = jax.nn.softmax(jnp.stack(lses, axis=0), axis=0)
        ya = jnp.sum(wts[..., None] * jnp.stack(outs, axis=0), axis=0).reshape(B, S, A_OUT)

        yb = stick_breaking_attention(qb, kb, vb).reshape(B, S, B_QKV)

        merged = (jax.nn.sigmoid(gate_a.astype(jnp.float32)) * (ya @ w_branch_a[l])
                  + jax.nn.sigmoid(gate_b.astype(jnp.float32)) * (yb @ w_branch_b[l]))
        x = x + rms_norm(merged @ w_out[l], g_post_mix[l]).astype(x.dtype)

        h = rms_norm(x, g_pre_ffn[l])
        u = h @ w_up[l]
        up = jnp.pad(u, ((0, 0), (CONV_WIDTH - 1, 0), (0, 0)))
        cw = conv_w[l]
        u = conv_b[l] + sum(cw[t] * up[:, t:t + S] for t in range(CONV_WIDTH))
        gate, val = jnp.split(u, 2, axis=-1)
        y = (jax.nn.gelu(gate.astype(jnp.float32), approximate=True) * val) @ w_down[l]
        x = x + rms_norm(y, g_post_ffn[l]).astype(x.dtype)
    return x
```

```python
import functools
import math

import numpy as np
import jax
import jax.numpy as jnp
from jax import lax
from jax.experimental import pallas as pl
from jax.experimental.pallas import tpu as pltpu

F32 = jnp.float32
BF16 = jnp.bfloat16

HEAD_DIM = 64
DILATED_GROUPS = ((128, 1), (512, 4), (2048, 16))
HEADS_PER_GROUP = 4
N_GROUPS = len(DILATED_GROUPS)
N_HEADS_A = HEADS_PER_GROUP * N_GROUPS
N_HEADS_B = 8
GROUP_W = HEADS_PER_GROUP * HEAD_DIM
A_QKV = N_HEADS_A * HEAD_DIM
B_QKV = N_HEADS_B * HEAD_DIM
BLOCK = 128
NUM_BUCKETS = 32
MAX_DISTANCE = 2048
CONV_WIDTH = 3
EPS = 1e-6
QK_SCALE = 1.0 / math.sqrt(HEAD_DIM)
MASKED = -1e30

LANES = 128
VMEM_LIMIT_BYTES = 56 * 1024 * 1024

TOKEN_TILE = 512
STICK_TILE = 256
CONV_HALO = 8
FF_CHUNK = 256


def _t5_bucket(dist):
    max_exact = NUM_BUCKETS // 2
    n = np.asarray(dist, dtype=np.float32)
    large = max_exact + (np.log(np.maximum(n, 1.0) / max_exact)
                         / np.log(MAX_DISTANCE / max_exact)
                         * (NUM_BUCKETS - max_exact)).astype(np.int32)
    large = np.minimum(large, NUM_BUCKETS - 1)
    return np.where(n < max_exact, n.astype(np.int32), large).astype(np.int32)


def _rms_norm(x, g):
    return x * lax.rsqrt(jnp.mean(x * x, axis=-1, keepdims=True) + EPS) * g


def _compiler_params(semantics):
    return pltpu.CompilerParams(dimension_semantics=semantics, vmem_limit_bytes=VMEM_LIMIT_BYTES)


def _resident(shape):
    return pl.BlockSpec(shape, lambda *_: (0,) * len(shape), pipeline_mode=pl.Buffered(1))


def _in_proj_body(x_ref, g_ref, w_ref, *refs):
    a_refs = refs[:3 * N_GROUPS]
    qb_ref, kb_ref, vb_ref, gates_ref, slab_ref = refs[3 * N_GROUPS:]
    tile = x_ref.shape[0]
    h = _rms_norm(x_ref[...], g_ref[...]).astype(BF16)

    def proj(col, width):
        return jnp.dot(h, w_ref[:, col:col + width], preferred_element_type=F32)

    for t in range(3):
        scale = QK_SCALE if t == 0 else 1.0
        for g, (_, dil) in enumerate(DILATED_GROUPS):
            acc = proj(t * A_QKV + g * GROUP_W, GROUP_W) * scale
            out_ref = a_refs[3 * g + t]
            if dil == 1:
                out_ref[...] = acc.astype(BF16)
                continue
            rows = tile // dil
            for s in range(GROUP_W // LANES):
                slab_ref[s] = acc[:, s * LANES:(s + 1) * LANES]
            for r in range(dil):
                for s in range(GROUP_W // LANES):
                    lane0 = r * GROUP_W + s * LANES
                    out_ref[:, lane0:lane0 + LANES] = (
                        slab_ref[s, pl.ds(r, rows, stride=dil), :].astype(BF16))

    base = 3 * A_QKV
    qb_ref[...] = (proj(base, B_QKV) * QK_SCALE).astype(BF16)
    kb_ref[...] = proj(base + B_QKV, B_QKV).astype(BF16)
    vb_ref[...] = proj(base + 2 * B_QKV, B_QKV).astype(BF16)
    base += 3 * B_QKV
    gate_w = gates_ref.shape[1]
    for c in range(0, gate_w, 512):
        gates_ref[:, c:c + 512] = proj(base + c, 512).astype(BF16)


def _in_proj(x, g, w):
    batch, seq, d_model = x.shape
    tile = TOKEN_TILE
    n_in = w.shape[1]
    gate_w = n_in - 3 * A_QKV - 3 * B_QKV
    out_shape, out_specs = [], []
    for _, dil in DILATED_GROUPS:
        for _ in range(3):
            out_shape.append(jax.ShapeDtypeStruct((batch, seq // dil, dil * GROUP_W), BF16))
            out_specs.append(pl.BlockSpec((None, tile // dil, dil * GROUP_W), lambda b, i: (b, i, 0)))
    for _ in range(3):
        out_shape.append(jax.ShapeDtypeStruct((batch, seq, B_QKV), BF16))
        out_specs.append(pl.BlockSpec((None, tile, B_QKV), lambda b, i: (b, i, 0)))
    out_shape.append(jax.ShapeDtypeStruct((batch, seq, gate_w), BF16))
    out_specs.append(pl.BlockSpec((None, tile, gate_w), lambda b, i: (b, i, 0)))
    return pl.pallas_call(
        _in_proj_body,
        grid=(batch, seq // tile),
        in_specs=[pl.BlockSpec((None, tile, d_model), lambda b, i: (b, i, 0)),
                  _resident((1, d_model)),
                  _resident((d_model, n_in))],
        out_specs=out_specs,
        out_shape=out_shape,
        scratch_shapes=[pltpu.VMEM((GROUP_W // LANES, tile, LANES), F32)],
        compiler_params=_compiler_params(("parallel", "parallel")),
        name="in_proj",
    )(x, g, w)


def _dilated_body(q_ref, k_ref, v_ref, kp_ref, vp_ref, bias_ref, o_ref, lse_ref, *, blocks):
    missing_cols = jnp.where(pl.program_id(2) == 0, BLOCK, 0)
    lane_head = lax.broadcasted_iota(jnp.int32, (BLOCK, GROUP_W), 1) // HEAD_DIM
    key_col = lax.broadcasted_iota(jnp.int32, (BLOCK, 2 * BLOCK), 1)
    for nb in range(blocks):
        rows = slice(nb * BLOCK, (nb + 1) * BLOCK)
        q = q_ref[rows, :]
        if nb == 0:
            kband = jnp.concatenate([kp_ref[...], k_ref[rows, :]], axis=0)
            vband = jnp.concatenate([vp_ref[...], v_ref[rows, :]], axis=0)
        else:
            band = slice((nb - 1) * BLOCK, (nb + 1) * BLOCK)
            kband, vband = k_ref[band, :], v_ref[band, :]
        o_acc = jnp.zeros((BLOCK, GROUP_W), F32)
        lse_acc = jnp.zeros((BLOCK, GROUP_W), F32)
        for hd in range(HEADS_PER_GROUP):
            in_head = lane_head == hd
            qh = jnp.where(in_head, q, jnp.zeros_like(q))
            logits = lax.dot_general(qh, kband, (((1,), (1,)), ((), ())),
                                     preferred_element_type=F32) + bias_ref[hd]
            if nb == 0:
                logits = jnp.where(key_col < missing_cols, MASKED, logits)
            mx = jnp.max(logits, axis=-1, keepdims=True)
            e = jnp.exp(logits - mx)
            den = jnp.sum(e, axis=-1, keepdims=True)
            pv = jnp.dot(e.astype(BF16), vband, preferred_element_type=F32)
            o_acc = jnp.where(in_head, pv / den, o_acc)
            lse_acc = jnp.where(in_head, mx + jnp.log(den), lse_acc)
        o_ref[rows, :] = o_acc
        lse_ref[rows, :] = lse_acc


def _dilated_group(q, k, v, bias_band, dil):
    batch, sub_seq, _ = q.shape
    blocks = min(8, sub_seq // BLOCK)
    tile = blocks * BLOCK
    cur = pl.BlockSpec((None, tile, GROUP_W), lambda b, r, n: (b, n, r))
    prev = pl.BlockSpec((None, BLOCK, GROUP_W),
                        lambda b, r, n: (b, jnp.maximum(n * blocks - 1, 0), r))
    out_sds = jax.ShapeDtypeStruct(q.shape, F32)
    return pl.pallas_call(
        functools.partial(_dilated_body, blocks=blocks),
        grid=(batch, dil, sub_seq // tile),
        in_specs=[cur, cur, cur, prev, prev, _resident(bias_band.shape)],
        out_specs=[cur, cur],
        out_shape=[out_sds, out_sds],
        compiler_params=_compiler_params(("parallel", "parallel", "parallel")),
        name=f"dilated_d{dil}",
    )(q, k, v, k, v, bias_band)


def _bias_band(rel_bias, group, dil, n_back):
    heads = slice(group * HEADS_PER_GROUP, (group + 1) * HEADS_PER_GROUP)
    buckets = _t5_bucket(np.arange(n_back + 1) * dil)
    bias = rel_bias[buckets][:, heads].T.astype(F32)
    i = np.arange(BLOCK)[:, None]
    c = np.arange(2 * BLOCK)[None, :]
    j = i + BLOCK - c
    local = (j >= 0) & (j <= n_back)
    return jnp.where(local[None], bias[:, np.clip(j, 0, n_back)], MASKED)


def _stick_body(q_ref, k_ref, v_ref, upper_ref, o_ref, acc_ref, carry_ref):
    tile = q_ref.shape[0]
    qi = pl.program_id(2)
    q = q_ref[...]
    lane = lax.broadcasted_iota(jnp.int32, (tile, LANES), 1)
    row = lax.broadcasted_iota(jnp.int32, (tile, tile), 0)
    col = lax.broadcasted_iota(jnp.int32, (tile, tile), 1)
    strictly_before = col < row
    upper = upper_ref[...]

    def key_tile(qh, kt, diagonal):
        start = pl.multiple_of(kt * tile, tile)
        kblk = k_ref[pl.ds(start, tile), :]
        vblk = v_ref[pl.ds(start, tile), :]
        z = lax.dot_general(qh, kblk, (((1,), (1,)), ((), ())), preferred_element_type=F32)
        soft = jnp.log1p(jnp.exp(-jnp.abs(z)))
        log_beta = jnp.minimum(z, 0.0) - soft
        log_1m = log_beta - z
        if diagonal:
            log_1m = jnp.where(strictly_before, log_1m, 0.0)
        hi = log_1m.astype(BF16)
        lo = (log_1m - hi.astype(F32)).astype(BF16)
        tail = (jnp.dot(hi, upper, preferred_element_type=F32)
                + jnp.dot(lo, upper, preferred_element_type=F32)) + carry_ref[...]
        w = jnp.exp(log_beta + tail)
        if diagonal:
            w = jnp.where(strictly_before, w, 0.0)
        acc_ref[...] += jnp.dot(w.astype(BF16), vblk, preferred_element_type=F32)
        carry_ref[...] += jnp.sum(log_1m, axis=-1, keepdims=True)

    out = jnp.zeros((tile, LANES), F32)
    for hd in range(LANES // HEAD_DIM):
        in_head = (lane // HEAD_DIM) == hd
        qh = jnp.where(in_head, q, jnp.zeros_like(q))
        acc_ref[...] = jnp.zeros_like(acc_ref)
        carry_ref[...] = jnp.zeros_like(carry_ref)
        key_tile(qh, qi, True)

        def earlier(j, _):
            key_tile(qh, qi - 1 - j, False)
            return 0

        lax.fori_loop(0, qi, earlier, 0)
        out = jnp.where(in_head, acc_ref[...], out)
    o_ref[...] = out.astype(o_ref.dtype)


def _stick_breaking(q, k, v):
    batch, seq, width = q.shape
    tile = STICK_TILE
    j = np.arange(tile)
    upper = jnp.asarray((j[:, None] > j[None, :]), BF16)
    return pl.pallas_call(
        _stick_body,
        grid=(batch, width // LANES, seq // tile),
        in_specs=[pl.BlockSpec((None, tile, LANES), lambda b, p, i: (b, i, p)),
                  pl.BlockSpec((None, seq, LANES), lambda b, p, i: (b, 0, p)),
                  pl.BlockSpec((None, seq, LANES), lambda b, p, i: (b, 0, p)),
                  _resident((tile, tile))],
        out_specs=pl.BlockSpec((None, tile, LANES), lambda b, p, i: (b, i, p)),
        out_shape=jax.ShapeDtypeStruct(q.shape, BF16),
        scratch_shapes=[pltpu.VMEM((tile, LANES), F32), pltpu.VMEM((tile, 1), F32)],
        compiler_params=_compiler_params(("parallel", "parallel", "parallel")),
        name="stick_breaking",
    )(q, k, v, upper)


def _post_mix_body(x_ref, gates_ref, yb_ref, *refs):
    group_refs = refs[:2 * N_GROUPS]
    wa_ref, wb_ref, wo_ref, g_ref, out_ref, slab_ref = refs[2 * N_GROUPS:]
    tile, d_model = x_ref.shape

    def natural_order(ref, dil):
        if dil == 1:
            return ref[...]
        rows = tile // dil
        for r in range(dil):
            for s in range(GROUP_W // LANES):
                lane0 = r * GROUP_W + s * LANES
                slab_ref[s, pl.ds(r, rows, stride=dil), :] = ref[:, lane0:lane0 + LANES]
        return jnp.concatenate([slab_ref[s] for s in range(GROUP_W // LANES)], axis=-1)

    outs, lses = [], []
    for g, (_, dil) in enumerate(DILATED_GROUPS):
        outs.append(natural_order(group_refs[2 * g], dil))
        lses.append(natural_order(group_refs[2 * g + 1], dil))
    top = functools.reduce(jnp.maximum, lses)
    wts = [jnp.exp(l - top) for l in lses]
    ya = sum(w * o for w, o in zip(wts, outs)) / sum(wts)

    branch_a = jnp.dot(ya.astype(BF16), wa_ref[...], preferred_element_type=F32)
    branch_b = jnp.dot(yb_ref[...], wb_ref[...], preferred_element_type=F32)
    gate_a = gates_ref[:, :d_model].astype(F32)
    gate_b = gates_ref[:, d_model:].astype(F32)
    merged = jax.nn.sigmoid(gate_a) * branch_a + jax.nn.sigmoid(gate_b) * branch_b
    mixed = jnp.dot(merged.astype(BF16), wo_ref[...], preferred_element_type=F32)
    out_ref[...] = x_ref[...] + _rms_norm(mixed, g_ref[...])


def _post_mix(x, gates, yb, groups, wa, wb, wo, g):
    batch, seq, d_model = x.shape
    tile = TOKEN_TILE
    row_spec = lambda width: pl.BlockSpec((None, tile, width), lambda b, i: (b, i, 0))
    in_specs = [row_spec(d_model), row_spec(gates.shape[-1]), row_spec(yb.shape[-1])]
    operands = [x, gates, yb]
    for (o, lse), (_, dil) in zip(groups, DILATED_GROUPS):
        for arr in (o, lse):
            in_specs.append(pl.BlockSpec((None, tile // dil, dil * GROUP_W), lambda b, i: (b, i, 0)))
            operands.append(arr)
    for arr in (wa, wb, wo, g):
        in_specs.append(_resident(arr.shape))
        operands.append(arr)
    return pl.pallas_call(
        _post_mix_body,
        grid=(batch, seq // tile),
        in_specs=in_specs,
        out_specs=row_spec(d_model),
        out_shape=jax.ShapeDtypeStruct(x.shape, F32),
        scratch_shapes=[pltpu.VMEM((GROUP_W // LANES, tile, LANES), F32)],
        compiler_params=_compiler_params(("parallel", "parallel")),
        name="post_mix",
    )(*operands)


def _ffn_body(x_ref, g_pre_ref, w_up_ref, cw_ref, cb_ref, w_down_ref, g_post_ref, out_ref,
              halo_ref, u_ref, acc_ref):
    tile = x_ref.shape[0]
    d_ff = w_down_ref.shape[0]

    @pl.when(pl.program_id(1) == 0)
    def _():
        halo_ref[...] = jnp.zeros_like(halo_ref)

    x = x_ref[...]
    h = _rms_norm(x, g_pre_ref[...]).astype(BF16)

    def conv(col):
        cols = slice(col, col + FF_CHUNK)
        u = jnp.dot(h, w_up_ref[:, cols], preferred_element_type=F32)
        u_ref[:CONV_HALO, :] = halo_ref[:, cols]
        u_ref[CONV_HALO:, :] = u
        halo_ref[:, cols] = u[tile - CONV_HALO:, :]
        y = cb_ref[:, cols]
        for t in range(CONV_WIDTH):
            shift = CONV_WIDTH - 1 - t
            y = y + cw_ref[t:t + 1, cols] * u_ref[CONV_HALO - shift:CONV_HALO - shift + tile, :]
        return y

    for c in range(d_ff // FF_CHUNK):
        gate = conv(c * FF_CHUNK)
        val = conv(d_ff + c * FF_CHUNK)
        act = (jax.nn.gelu(gate, approximate=True) * val).astype(BF16)
        part = jnp.dot(act, w_down_ref[c * FF_CHUNK:(c + 1) * FF_CHUNK, :],
                       preferred_element_type=F32)
        if c == 0:
            acc_ref[...] = part
        else:
            acc_ref[...] += part
    out_ref[...] = x + _rms_norm(acc_ref[...], g_post_ref[...])


def _ffn(x, g_pre, w_up, conv_w, conv_b, w_down, g_post):
    batch, seq, d_model = x.shape
    tile = TOKEN_TILE
    row_spec = pl.BlockSpec((None, tile, d_model), lambda b, i: (b, i, 0))
    consts = (g_pre, w_up, conv_w, conv_b, w_down, g_post)
    return pl.pallas_call(
        _ffn_body,
        grid=(batch, seq // tile),
        in_specs=[row_spec] + [_resident(a.shape) for a in consts],
        out_specs=row_spec,
        out_shape=jax.ShapeDtypeStruct(x.shape, F32),
        scratch_shapes=[pltpu.VMEM((CONV_HALO, w_up.shape[1]), F32),
                        pltpu.VMEM((CONV_HALO + tile, FF_CHUNK), F32),
                        pltpu.VMEM((tile, d_model), F32)],
        compiler_params=_compiler_params(("arbitrary", "arbitrary")),
        name="ffn",
    )(x, *consts)


def kernel(x, g_pre_mix, w_in, rel_bias, w_branch_a, w_branch_b, w_out, g_post_mix, g_pre_ffn,
           w_up, conv_w, conv_b, w_down, g_post_ffn):
    depth = w_in.shape[0]
    for l in range(depth):
        proj = _in_proj(x, g_pre_mix[l][None], w_in[l].astype(BF16))
        qb, kb, vb, gates = proj[3 * N_GROUPS:]
        groups = []
        for g, (window, dil) in enumerate(DILATED_GROUPS):
            q, k, v = proj[3 * g:3 * g + 3]
            band = _bias_band(rel_bias, g, dil, window // dil)
            groups.append(_dilated_group(q, k, v, band, dil))
        yb = _stick_breaking(qb, kb, vb)
        x = _post_mix(x, gates, yb, groups, w_branch_a[l].astype(BF16), w_branch_b[l].astype(BF16),
                      w_out[l].astype(BF16), g_post_mix[l][None])
        x = _ffn(x, g_pre_ffn[l][None], w_up[l].astype(BF16), conv_w[l], conv_b[l][None],
                 w_down[l].astype(BF16), g_post_ffn[l][None])
    return x
```

```python
import functools
import math

import numpy as np
import jax
import jax.numpy as jnp
from jax import lax
from jax.experimental import pallas as pl
from jax.experimental.pallas import tpu as pltpu

F32 = jnp.float32
BF16 = jnp.bfloat16

HEAD_DIM = 64
DILATED_GROUPS = ((128, 1), (512, 4), (2048, 16))
HEADS_PER_GROUP = 4
N_GROUPS = len(DILATED_GROUPS)
N_HEADS_A = HEADS_PER_GROUP * N_GROUPS
N_HEADS_B = 8
GROUP_W = HEADS_PER_GROUP * HEAD_DIM
A_QKV = N_HEADS_A * HEAD_DIM
B_QKV = N_HEADS_B * HEAD_DIM
BLOCK = 128
NUM_BUCKETS = 32
MAX_DISTANCE = 2048
CONV_WIDTH = 3
EPS = 1e-6
QK_SCALE = 1.0 / math.sqrt(HEAD_DIM)
MASKED = -1e30
F32_EXP_UNDERFLOW = 104.0

LANES = 128
VMEM_LIMIT_BYTES = 56 * 1024 * 1024

TOKEN_TILE = 512
STICK_TILE = 256
CONV_HALO = 8
FF_CHUNK = 256


def _t5_bucket(dist):
    max_exact = NUM_BUCKETS // 2
    n = np.asarray(dist, dtype=np.float32)
    large = max_exact + (np.log(np.maximum(n, 1.0) / max_exact)
                         / np.log(MAX_DISTANCE / max_exact)
                         * (NUM_BUCKETS - max_exact)).astype(np.int32)
    large = np.minimum(large, NUM_BUCKETS - 1)
    return np.where(n < max_exact, n.astype(np.int32), large).astype(np.int32)


def _rms_norm(x, g):
    return x * lax.rsqrt(jnp.mean(x * x, axis=-1, keepdims=True) + EPS) * g


def _compiler_params(semantics):
    return pltpu.CompilerParams(dimension_semantics=semantics, vmem_limit_bytes=VMEM_LIMIT_BYTES)


def _resident(shape):
    return pl.BlockSpec(shape, lambda *_: (0,) * len(shape), pipeline_mode=pl.Buffered(1))


def _in_proj_body(x_ref, g_ref, w_ref, *refs):
    a_refs = refs[:3 * N_GROUPS]
    qb_ref, kb_ref, vb_ref, gates_ref, slab_ref = refs[3 * N_GROUPS:]
    tile = x_ref.shape[0]
    h = _rms_norm(x_ref[...], g_ref[...]).astype(BF16)

    def proj(col, width):
        return jnp.dot(h, w_ref[:, col:col + width], preferred_element_type=F32)

    for t in range(3):
        scale = QK_SCALE if t == 0 else 1.0
        for g, (_, dil) in enumerate(DILATED_GROUPS):
            acc = proj(t * A_QKV + g * GROUP_W, GROUP_W) * scale
            out_ref = a_refs[3 * g + t]
            if dil == 1:
                out_ref[...] = acc.astype(BF16)
                continue
            rows = tile // dil
            for s in range(GROUP_W // LANES):
                slab_ref[s] = acc[:, s * LANES:(s + 1) * LANES]
            for r in range(dil):
                for s in range(GROUP_W // LANES):
                    lane0 = r * GROUP_W + s * LANES
                    out_ref[:, lane0:lane0 + LANES] = (
                        slab_ref[s, pl.ds(r, rows, stride=dil), :].astype(BF16))

    base = 3 * A_QKV
    qb_ref[...] = (proj(base, B_QKV) * QK_SCALE).astype(BF16)
    kb_ref[...] = proj(base + B_QKV, B_QKV).astype(BF16)
    vb_ref[...] = proj(base + 2 * B_QKV, B_QKV).astype(BF16)
    base += 3 * B_QKV
    gate_w = gates_ref.shape[1]
    for c in range(0, gate_w, 512):
        gates_ref[:, c:c + 512] = proj(base + c, 512).astype(BF16)


def _in_proj(x, g, w):
    batch, seq, d_model = x.shape
    tile = TOKEN_TILE
    n_in = w.shape[1]
    gate_w = n_in - 3 * A_QKV - 3 * B_QKV
    out_shape, out_specs = [], []
    for _, dil in DILATED_GROUPS:
        for _ in range(3):
            out_shape.append(jax.ShapeDtypeStruct((batch, seq // dil, dil * GROUP_W), BF16))
            out_specs.append(pl.BlockSpec((None, tile // dil, dil * GROUP_W), lambda b, i: (b, i, 0)))
    for _ in range(3):
        out_shape.append(jax.ShapeDtypeStruct((batch, seq, B_QKV), BF16))
        out_specs.append(pl.BlockSpec((None, tile, B_QKV), lambda b, i: (b, i, 0)))
    out_shape.append(jax.ShapeDtypeStruct((batch, seq, gate_w), BF16))
    out_specs.append(pl.BlockSpec((None, tile, gate_w), lambda b, i: (b, i, 0)))
    return pl.pallas_call(
        _in_proj_body,
        grid=(batch, seq // tile),
        in_specs=[pl.BlockSpec((None, tile, d_model), lambda b, i: (b, i, 0)),
                  _resident((1, d_model)),
                  _resident((d_model, n_in))],
        out_specs=out_specs,
        out_shape=out_shape,
        scratch_shapes=[pltpu.VMEM((GROUP_W // LANES, tile, LANES), F32)],
        compiler_params=_compiler_params(("parallel", "parallel")),
        name="in_proj",
    )(x, g, w)


def _dilated_body(q_ref, k_ref, v_ref, kp_ref, vp_ref, bias_ref, o_ref, lse_ref, *, blocks):
    missing_cols = jnp.where(pl.program_id(2) == 0, BLOCK, 0)
    lane_head = lax.broadcasted_iota(jnp.int32, (BLOCK, GROUP_W), 1) // HEAD_DIM
    key_col = lax.broadcasted_iota(jnp.int32, (BLOCK, 2 * BLOCK), 1)
    for nb in range(blocks):
        rows = slice(nb * BLOCK, (nb + 1) * BLOCK)
        q = q_ref[rows, :]
        if nb == 0:
            kband = jnp.concatenate([kp_ref[...], k_ref[rows, :]], axis=0)
            vband = jnp.concatenate([vp_ref[...], v_ref[rows, :]], axis=0)
        else:
            band = slice((nb - 1) * BLOCK, (nb + 1) * BLOCK)
            kband, vband = k_ref[band, :], v_ref[band, :]
        o_acc = jnp.zeros((BLOCK, GROUP_W), F32)
        lse_acc = jnp.zeros((BLOCK, GROUP_W), F32)
        for hd in range(HEADS_PER_GROUP):
            in_head = lane_head == hd
            qh = jnp.where(in_head, q, jnp.zeros_like(q))
            logits = lax.dot_general(qh, kband, (((1,), (1,)), ((), ())),
                                     preferred_element_type=F32) + bias_ref[hd]
            if nb == 0:
                logits = jnp.where(key_col < missing_cols, MASKED, logits)
            mx = jnp.max(logits, axis=-1, keepdims=True)
            e = jnp.exp(logits - mx)
            den = jnp.sum(e, axis=-1, keepdims=True)
            pv = jnp.dot(e.astype(BF16), vband, preferred_element_type=F32)
            o_acc = jnp.where(in_head, pv / den, o_acc)
            lse_acc = jnp.where(in_head, mx + jnp.log(den), lse_acc)
        o_ref[rows, :] = o_acc
        lse_ref[rows, :] = lse_acc


def _dilated_group(q, k, v, bias_band, dil):
    batch, sub_seq, _ = q.shape
    blocks = min(8, sub_seq // BLOCK)
    tile = blocks * BLOCK
    cur = pl.BlockSpec((None, tile, GROUP_W), lambda b, r, n: (b, n, r))
    prev = pl.BlockSpec((None, BLOCK, GROUP_W),
                        lambda b, r, n: (b, jnp.maximum(n * blocks - 1, 0), r))
    out_sds = jax.ShapeDtypeStruct(q.shape, F32)
    return pl.pallas_call(
        functools.partial(_dilated_body, blocks=blocks),
        grid=(batch, dil, sub_seq // tile),
        in_specs=[cur, cur, cur, prev, prev, _resident(bias_band.shape)],
        out_specs=[cur, cur],
        out_shape=[out_sds, out_sds],
        compiler_params=_compiler_params(("parallel", "parallel", "parallel")),
        name=f"dilated_d{dil}",
    )(q, k, v, k, v, bias_band)


def _bias_band(rel_bias, group, dil, n_back):
    heads = slice(group * HEADS_PER_GROUP, (group + 1) * HEADS_PER_GROUP)
    buckets = _t5_bucket(np.arange(n_back + 1) * dil)
    bias = rel_bias[buckets][:, heads].T.astype(F32)
    period = 2 * BLOCK + 1
    j = BLOCK - np.arange(period)
    local = (j >= 0) & (j <= n_back)
    diag = jnp.where(local[None], bias[:, np.clip(j, 0, n_back)], MASKED)
    band = jnp.tile(diag, (1, BLOCK))[:, :BLOCK * 2 * BLOCK]
    return band.reshape(HEADS_PER_GROUP, BLOCK, 2 * BLOCK)


def _stick_tile(qh, kblk, vblk, upper, carry, mask):
    z = lax.dot_general(qh, kblk, (((1,), (1,)), ((), ())), preferred_element_type=F32)
    soft = jnp.maximum(z, 0.0) + jnp.log(1.0 + jnp.exp(-jnp.abs(z)))
    log_beta = z - soft
    if mask is not None:
        soft = jnp.where(mask, soft, 0.0)
    hi = soft.astype(BF16)
    lo = (soft - hi.astype(F32)).astype(BF16)
    tail = (jnp.dot(hi, upper, preferred_element_type=F32)
            + jnp.dot(lo, upper, preferred_element_type=F32)) + carry
    w = jnp.exp(log_beta - tail)
    if mask is not None:
        w = jnp.where(mask, w, 0.0)
    out = jnp.dot(w.astype(BF16), vblk, preferred_element_type=F32)
    return out, carry + jnp.sum(soft, axis=-1, keepdims=True)


def _stick_body(q_ref, k_ref, v_ref, upper_ref, o_ref, acc_ref, carry_ref):
    tile = q_ref.shape[0]
    heads = LANES // HEAD_DIM
    qi = pl.program_id(2)
    q = q_ref[...]
    lane_head = lax.broadcasted_iota(jnp.int32, (tile, LANES), 1) // HEAD_DIM
    row = lax.broadcasted_iota(jnp.int32, (tile, tile), 0)
    col = lax.broadcasted_iota(jnp.int32, (tile, tile), 1)
    strictly_before = col < row
    upper = upper_ref[...]
    qh = [jnp.where(lane_head == hd, q, jnp.zeros_like(q)) for hd in range(heads)]

    def key_value(kt):
        start = pl.multiple_of(kt * tile, tile)
        return k_ref[pl.ds(start, tile), :], v_ref[pl.ds(start, tile), :]

    k_diag, v_diag = key_value(qi)
    k_prev, v_prev = key_value(jnp.maximum(qi - 1, 0))
    v_prev = jnp.where(qi > 0, v_prev, jnp.zeros_like(v_prev))
    for hd in range(heads):
        out_diag, carry = _stick_tile(qh[hd], k_diag, v_diag, upper, 0.0, strictly_before)
        out_prev, carry = _stick_tile(qh[hd], k_prev, v_prev, upper, carry, None)
        acc_ref[hd] = out_diag + out_prev
        carry_ref[hd] = carry

    def any_weight_left():
        return (jnp.min(carry_ref[...]) < F32_EXP_UNDERFLOW).astype(jnp.int32)

    def more_tiles(state):
        j, live = state
        return jnp.logical_and(j < qi, live > 0)

    def earlier(state):
        j, _ = state
        kblk, vblk = key_value(qi - 1 - j)
        for hd in range(heads):
            out, carry = _stick_tile(qh[hd], kblk, vblk, upper, carry_ref[hd], None)
            acc_ref[hd] += out
            carry_ref[hd] = carry
        return j + 1, any_weight_left()

    lax.while_loop(more_tiles, earlier, (jnp.int32(1), any_weight_left()))
    out = acc_ref[0]
    for hd in range(1, heads):
        out = jnp.where(lane_head == hd, acc_ref[hd], out)
    o_ref[...] = out.astype(o_ref.dtype)


def _stick_breaking(q, k, v):
    batch, seq, width = q.shape
    tile = STICK_TILE
    j = np.arange(tile)
    upper = jnp.asarray((j[:, None] > j[None, :]), BF16)
    return pl.pallas_call(
        _stick_body,
        grid=(batch, width // LANES, seq // tile),
        in_specs=[pl.BlockSpec((None, tile, LANES), lambda b, p, i: (b, i, p)),
                  pl.BlockSpec((None, seq, LANES), lambda b, p, i: (b, 0, p)),
                  pl.BlockSpec((None, seq, LANES), lambda b, p, i: (b, 0, p)),
                  _resident((tile, tile))],
        out_specs=pl.BlockSpec((None, tile, LANES), lambda b, p, i: (b, i, p)),
        out_shape=jax.ShapeDtypeStruct(q.shape, BF16),
        scratch_shapes=[pltpu.VMEM((LANES // HEAD_DIM, tile, LANES), F32),
                        pltpu.VMEM((LANES // HEAD_DIM, tile, 1), F32)],
        compiler_params=_compiler_params(("parallel", "parallel", "parallel")),
        name="stick_breaking",
    )(q, k, v, upper)


def _post_mix_body(x_ref, gates_ref, yb_ref, *refs):
    group_refs = refs[:2 * N_GROUPS]
    wa_ref, wb_ref, wo_ref, g_ref, out_ref, slab_ref = refs[2 * N_GROUPS:]
    tile, d_model = x_ref.shape

    def natural_order(ref, dil):
        if dil == 1:
            return ref[...]
        rows = tile // dil
        for r in range(dil):
            for s in range(GROUP_W // LANES):
                lane0 = r * GROUP_W + s * LANES
                slab_ref[s, pl.ds(r, rows, stride=dil), :] = ref[:, lane0:lane0 + LANES]
        return jnp.concatenate([slab_ref[s] for s in range(GROUP_W // LANES)], axis=-1)

    outs, lses = [], []
    for g, (_, dil) in enumerate(DILATED_GROUPS):
        outs.append(natural_order(group_refs[2 * g], dil))
        lses.append(natural_order(group_refs[2 * g + 1], dil))
    top = functools.reduce(jnp.maximum, lses)
    wts = [jnp.exp(l - top) for l in lses]
    ya = sum(w * o for w, o in zip(wts, outs)) / sum(wts)

    branch_a = jnp.dot(ya.astype(BF16), wa_ref[...], preferred_element_type=F32)
    branch_b = jnp.dot(yb_ref[...], wb_ref[...], preferred_element_type=F32)
    gate_a = gates_ref[:, :d_model].astype(F32)
    gate_b = gates_ref[:, d_model:].astype(F32)
    merged = jax.nn.sigmoid(gate_a) * branch_a + jax.nn.sigmoid(gate_b) * branch_b
    mixed = jnp.dot(merged.astype(BF16), wo_ref[...], preferred_element_type=F32)
    out_ref[...] = x_ref[...] + _rms_norm(mixed, g_ref[...])


def _post_mix(x, gates, yb, groups, wa, wb, wo, g):
    batch, seq, d_model = x.shape
    tile = TOKEN_TILE
    row_spec = lambda width: pl.BlockSpec((None, tile, width), lambda b, i: (b, i, 0))
    in_specs = [row_spec(d_model), row_spec(gates.shape[-1]), row_spec(yb.shape[-1])]
    operands = [x, gates, yb]
    for (o, lse), (_, dil) in zip(groups, DILATED_GROUPS):
        for arr in (o, lse):
            in_specs.append(pl.BlockSpec((None, tile // dil, dil * GROUP_W), lambda b, i: (b, i, 0)))
            operands.append(arr)
    for arr in (wa, wb, wo, g):
        in_specs.append(_resident(arr.shape))
        operands.append(arr)
    return pl.pallas_call(
        _post_mix_body,
        grid=(batch, seq // tile),
        in_specs=in_specs,
        out_specs=row_spec(d_model),
        out_shape=jax.ShapeDtypeStruct(x.shape, F32),
        scratch_shapes=[pltpu.VMEM((GROUP_W // LANES, tile, LANES), F32)],
        compiler_params=_compiler_params(("parallel", "parallel")),
        name="post_mix",
    )(*operands)


def _ffn_body(x_ref, g_pre_ref, w_up_ref, cw_ref, cb_ref, w_down_ref, g_post_ref, out_ref,
              halo_ref, u_ref, acc_ref):
    tile = x_ref.shape[0]
    d_ff = w_down_ref.shape[0]

    @pl.when(pl.program_id(1) == 0)
    def _():
        halo_ref[...] = jnp.zeros_like(halo_ref)

    x = x_ref[...]
    h = _rms_norm(x, g_pre_ref[...]).astype(BF16)

    def conv(col):
        cols = slice(col, col + FF_CHUNK)
        u = jnp.dot(h, w_up_ref[:, cols], preferred_element_type=F32)
        u_ref[:CONV_HALO, :] = halo_ref[:, cols]
        u_ref[CONV_HALO:, :] = u
        halo_ref[:, cols] = u[tile - CONV_HALO:, :]
        y = cb_ref[:, cols]
        for t in range(CONV_WIDTH):
            shift = CONV_WIDTH - 1 - t
            y = y + cw_ref[t:t + 1, cols] * u_ref[CONV_HALO - shift:CONV_HALO - shift + tile, :]
        return y

    for c in range(d_ff // FF_CHUNK):
        gate = conv(c * FF_CHUNK)
        val = conv(d_ff + c * FF_CHUNK)
        act = (jax.nn.gelu(gate, approximate=True) * val).astype(BF16)
        part = jnp.dot(act, w_down_ref[c * FF_CHUNK:(c + 1) * FF_CHUNK, :],
                       preferred_element_type=F32)
        if c == 0:
            acc_ref[...] = part
        else:
            acc_ref[...] += part
    out_ref[...] = x + _rms_norm(acc_ref[...], g_post_ref[...])


def _ffn(x, g_pre, w_up, conv_w, conv_b, w_down, g_post):
    batch, seq, d_model = x.shape
    tile = TOKEN_TILE
    row_spec = pl.BlockSpec((None, tile, d_model), lambda b, i: (b, i, 0))
    consts = (g_pre, w_up, conv_w, conv_b, w_down, g_post)
    return pl.pallas_call(
        _ffn_body,
        grid=(batch, seq // tile),
        in_specs=[row_spec] + [_resident(a.shape) for a in consts],
        out_specs=row_spec,
        out_shape=jax.ShapeDtypeStruct(x.shape, F32),
        scratch_shapes=[pltpu.VMEM((CONV_HALO, w_up.shape[1]), F32),
                        pltpu.VMEM((CONV_HALO + tile, FF_CHUNK), F32),
                        pltpu.VMEM((tile, d_model), F32)],
        compiler_params=_compiler_params(("arbitrary", "arbitrary")),
        name="ffn",
    )(x, *consts)


def kernel(x, g_pre_mix, w_in, rel_bias, w_branch_a, w_branch_b, w_out, g_post_mix, g_pre_ffn,
           w_up, conv_w, conv_b, w_down, g_post_ffn):
    depth = w_in.shape[0]
    for l in range(depth):
        proj = _in_proj(x, g_pre_mix[l][None], w_in[l].astype(BF16))
        qb, kb, vb, gates = proj[3 * N_GROUPS:]
        groups = []
        for g, (window, dil) in enumerate(DILATED_GROUPS):
            q, k, v = proj[3 * g:3 * g + 3]
            band = _bias_band(rel_bias, g, dil, window // dil)
            groups.append(_dilated_group(q, k, v, band, dil))
        yb = _stick_breaking(qb, kb, vb)
        x = _post_mix(x, gates, yb, groups, w_branch_a[l].astype(BF16), w_branch_b[l].astype(BF16),
                      w_out[l].astype(BF16), g_post_mix[l][None])
        x = _ffn(x, g_pre_ffn[l][None], w_up[l].astype(BF16), conv_w[l], conv_b[l][None],
                 w_down[l].astype(BF16), g_post_ffn[l][None])
    return x
```

```python
import functools
import math
from typing import Any, NamedTuple, Optional

import numpy as np
import jax
import jax.numpy as jnp
from jax import lax
from jax.experimental import pallas as pl
from jax.experimental.pallas import tpu as pltpu

F32 = jnp.float32
BF16 = jnp.bfloat16

HEAD_DIM = 64
DILATED_GROUPS = ((128, 1), (512, 4), (2048, 16))
HEADS_PER_GROUP = 4
N_GROUPS = len(DILATED_GROUPS)
N_HEADS_A = HEADS_PER_GROUP * N_GROUPS
N_HEADS_B = 8
GROUP_W = HEADS_PER_GROUP * HEAD_DIM
A_QKV = N_HEADS_A * HEAD_DIM
B_QKV = N_HEADS_B * HEAD_DIM
BLOCK = 128
NUM_BUCKETS = 32
MAX_DISTANCE = 2048
CONV_WIDTH = 3
EPS = 1e-6
QK_SCALE = 1.0 / math.sqrt(HEAD_DIM)
MASKED = -1e30
F32_EXP_UNDERFLOW = 104.0

LANES = 128
VMEM_LIMIT_BYTES = 56 * 1024 * 1024

TOKEN_TILE = 512
STICK_TILE = 256
CONV_HALO = 8
FF_CHUNK = 256


def _t5_bucket(dist):
    max_exact = NUM_BUCKETS // 2
    n = np.asarray(dist, dtype=np.float32)
    large = max_exact + (np.log(np.maximum(n, 1.0) / max_exact)
                         / np.log(MAX_DISTANCE / max_exact)
                         * (NUM_BUCKETS - max_exact)).astype(np.int32)
    large = np.minimum(large, NUM_BUCKETS - 1)
    return np.where(n < max_exact, n.astype(np.int32), large).astype(np.int32)


def _rms_norm(x, g):
    return x * lax.rsqrt(jnp.mean(x * x, axis=-1, keepdims=True) + EPS) * g


def _compiler_params(semantics):
    return pltpu.CompilerParams(dimension_semantics=semantics, vmem_limit_bytes=VMEM_LIMIT_BYTES)


def _resident(shape):
    return pl.BlockSpec(shape, lambda *_: (0,) * len(shape), pipeline_mode=pl.Buffered(1))


def _in_proj_body(x_ref, g_ref, w_ref, *refs):
    a_refs = refs[:3 * N_GROUPS]
    qb_ref, kb_ref, vb_ref, gates_ref, slab_ref = refs[3 * N_GROUPS:]
    tile = x_ref.shape[0]
    h = _rms_norm(x_ref[...], g_ref[...]).astype(BF16)

    def proj(col, width):
        return jnp.dot(h, w_ref[:, col:col + width], preferred_element_type=F32)

    for t in range(3):
        scale = QK_SCALE if t == 0 else 1.0
        for g, (_, dil) in enumerate(DILATED_GROUPS):
            acc = proj(t * A_QKV + g * GROUP_W, GROUP_W) * scale
            out_ref = a_refs[3 * g + t]
            if dil == 1:
                out_ref[...] = acc.astype(BF16)
                continue
            rows = tile // dil
            for s in range(GROUP_W // LANES):
                slab_ref[s] = acc[:, s * LANES:(s + 1) * LANES]
            for r in range(dil):
                for s in range(GROUP_W // LANES):
                    lane0 = r * GROUP_W + s * LANES
                    out_ref[:, lane0:lane0 + LANES] = (
                        slab_ref[s, pl.ds(r, rows, stride=dil), :].astype(BF16))

    base = 3 * A_QKV
    qb_ref[...] = (proj(base, B_QKV) * QK_SCALE).astype(BF16)
    kb_ref[...] = proj(base + B_QKV, B_QKV).astype(BF16)
    vb_ref[...] = proj(base + 2 * B_QKV, B_QKV).astype(BF16)
    base += 3 * B_QKV
    gate_w = gates_ref.shape[1]
    for c in range(0, gate_w, 512):
        gates_ref[:, c:c + 512] = proj(base + c, 512).astype(BF16)


def _in_proj(x, g, w):
    batch, seq, d_model = x.shape
    tile = TOKEN_TILE
    n_in = w.shape[1]
    gate_w = n_in - 3 * A_QKV - 3 * B_QKV
    out_shape, out_specs = [], []
    for _, dil in DILATED_GROUPS:
        for _ in range(3):
            out_shape.append(jax.ShapeDtypeStruct((batch, seq // dil, dil * GROUP_W), BF16))
            out_specs.append(pl.BlockSpec((None, tile // dil, dil * GROUP_W), lambda b, i: (b, i, 0)))
    for _ in range(3):
        out_shape.append(jax.ShapeDtypeStruct((batch, seq, B_QKV), BF16))
        out_specs.append(pl.BlockSpec((None, tile, B_QKV), lambda b, i: (b, i, 0)))
    out_shape.append(jax.ShapeDtypeStruct((batch, seq, gate_w), BF16))
    out_specs.append(pl.BlockSpec((None, tile, gate_w), lambda b, i: (b, i, 0)))
    return pl.pallas_call(
        _in_proj_body,
        grid=(batch, seq // tile),
        in_specs=[pl.BlockSpec((None, tile, d_model), lambda b, i: (b, i, 0)),
                  _resident((1, d_model)),
                  _resident((d_model, n_in))],
        out_specs=out_specs,
        out_shape=out_shape,
        scratch_shapes=[pltpu.VMEM((GROUP_W // LANES, tile, LANES), F32)],
        compiler_params=_compiler_params(("parallel", "parallel")),
        name="in_proj",
    )(x, g, w)


def _dilated_body(q_ref, k_ref, v_ref, kp_ref, vp_ref, bias_ref, o_ref, lse_ref, *, blocks):
    missing_cols = jnp.where(pl.program_id(2) == 0, BLOCK, 0)
    lane_head = lax.broadcasted_iota(jnp.int32, (BLOCK, GROUP_W), 1) // HEAD_DIM
    key_col = lax.broadcasted_iota(jnp.int32, (BLOCK, 2 * BLOCK), 1)
    for nb in range(blocks):
        rows = slice(nb * BLOCK, (nb + 1) * BLOCK)
        q = q_ref[rows, :]
        if nb == 0:
            kband = jnp.concatenate([kp_ref[...], k_ref[rows, :]], axis=0)
            vband = jnp.concatenate([vp_ref[...], v_ref[rows, :]], axis=0)
        else:
            band = slice((nb - 1) * BLOCK, (nb + 1) * BLOCK)
            kband, vband = k_ref[band, :], v_ref[band, :]
        heads = range(HEADS_PER_GROUP)
        logits = []
        for hd in heads:
            qh = jnp.where(lane_head == hd, q, jnp.zeros_like(q))
            s = lax.dot_general(qh, kband, (((1,), (1,)), ((), ())),
                                preferred_element_type=F32) + bias_ref[hd]
            if nb == 0:
                s = jnp.where(key_col < missing_cols, MASKED, s)
            logits.append(s)
        mx = [jnp.max(s, axis=-1, keepdims=True) for s in logits]
        e = [jnp.exp(s - m) for s, m in zip(logits, mx)]
        den = [jnp.sum(p, axis=-1, keepdims=True) for p in e]
        pv = [jnp.dot(p.astype(BF16), vband, preferred_element_type=F32) for p in e]
        o_acc = pv[0] / den[0]
        lse_acc = jnp.broadcast_to(mx[0] + jnp.log(den[0]), (BLOCK, GROUP_W))
        for hd in heads[1:]:
            in_head = lane_head == hd
            o_acc = jnp.where(in_head, pv[hd] / den[hd], o_acc)
            lse_acc = jnp.where(in_head, mx[hd] + jnp.log(den[hd]), lse_acc)
        o_ref[rows, :] = o_acc
        lse_ref[rows, :] = lse_acc


def _dilated_group(q, k, v, bias_band, dil):
    batch, sub_seq, _ = q.shape
    blocks = min(8, sub_seq // BLOCK)
    tile = blocks * BLOCK
    cur = pl.BlockSpec((None, tile, GROUP_W), lambda b, r, n: (b, n, r))
    prev = pl.BlockSpec((None, BLOCK, GROUP_W),
                        lambda b, r, n: (b, jnp.maximum(n * blocks - 1, 0), r))
    out_sds = jax.ShapeDtypeStruct(q.shape, F32)
    return pl.pallas_call(
        functools.partial(_dilated_body, blocks=blocks),
        grid=(batch, dil, sub_seq // tile),
        in_specs=[cur, cur, cur, prev, prev, _resident(bias_band.shape)],
        out_specs=[cur, cur],
        out_shape=[out_sds, out_sds],
        compiler_params=_compiler_params(("parallel", "parallel", "parallel")),
        name=f"dilated_d{dil}",
    )(q, k, v, k, v, bias_band)


def _bias_band(rel_bias, group, dil, n_back):
    heads = slice(group * HEADS_PER_GROUP, (group + 1) * HEADS_PER_GROUP)
    buckets = _t5_bucket(np.arange(n_back + 1) * dil)
    bias = rel_bias[buckets][:, heads].T.astype(F32)
    period = 2 * BLOCK + 1
    j = BLOCK - np.arange(period)
    local = (j >= 0) & (j <= n_back)
    diag = jnp.where(local[None], bias[:, np.clip(j, 0, n_back)], MASKED)
    band = jnp.tile(diag, (1, BLOCK))[:, :BLOCK * 2 * BLOCK]
    return band.reshape(HEADS_PER_GROUP, BLOCK, 2 * BLOCK)


class _StickStep(NamedTuple):
    q: jax.Array
    k: jax.Array
    v: jax.Array
    mask: Optional[jax.Array]
    carry: Any
    after: Optional[int]


def _stick_steps(steps, upper):
    z = [lax.dot_general(st.q, st.k, (((1,), (1,)), ((), ())), preferred_element_type=F32)
         for st in steps]
    soft = [jnp.maximum(s, 0.0) + jnp.log(1.0 + jnp.exp(-jnp.abs(s))) for s in z]
    log_beta = [s - p for s, p in zip(z, soft)]
    soft = [p if st.mask is None else jnp.where(st.mask, p, 0.0) for p, st in zip(soft, steps)]
    hi = [p.astype(BF16) for p in soft]
    lo = [(p - h.astype(F32)).astype(BF16) for p, h in zip(soft, hi)]
    tail = [jnp.dot(h, upper, preferred_element_type=F32) + jnp.dot(l, upper, preferred_element_type=F32)
            for h, l in zip(hi, lo)]
    sums = [jnp.sum(p, axis=-1, keepdims=True) for p in soft]
    weights = []
    for i, st in enumerate(steps):
        carry = st.carry if st.after is None else st.carry + sums[st.after]
        w = jnp.exp(log_beta[i] - (tail[i] + carry))
        weights.append(w if st.mask is None else jnp.where(st.mask, w, 0.0))
    outs = [jnp.dot(w.astype(BF16), st.v, preferred_element_type=F32)
            for w, st in zip(weights, steps)]
    return outs, sums


def _stick_body(q_ref, k_ref, v_ref, upper_ref, o_ref, acc_ref, carry_ref):
    tile = q_ref.shape[0]
    heads = LANES // HEAD_DIM
    qi = pl.program_id(2)
    q = q_ref[...]
    lane_head = lax.broadcasted_iota(jnp.int32, (tile, LANES), 1) // HEAD_DIM
    row = lax.broadcasted_iota(jnp.int32, (tile, tile), 0)
    col = lax.broadcasted_iota(jnp.int32, (tile, tile), 1)
    strictly_before = col < row
    upper = upper_ref[...]
    qh = [jnp.where(lane_head == hd, q, jnp.zeros_like(q)) for hd in range(heads)]

    def key_value(kt):
        start = pl.multiple_of(kt * tile, tile)
        return k_ref[pl.ds(start, tile), :], v_ref[pl.ds(start, tile), :]

    k_diag, v_diag = key_value(qi)
    k_prev, v_prev = key_value(jnp.maximum(qi - 1, 0))
    v_prev = jnp.where(qi > 0, v_prev, jnp.zeros_like(v_prev))
    steps = []
    for hd in range(heads):
        steps.append(_StickStep(qh[hd], k_diag, v_diag, strictly_before, 0.0, None))
        steps.append(_StickStep(qh[hd], k_prev, v_prev, None, 0.0, 2 * hd))
    outs, sums = _stick_steps(steps, upper)
    for hd in range(heads):
        acc_ref[hd] = outs[2 * hd] + outs[2 * hd + 1]
        carry_ref[hd] = sums[2 * hd] + sums[2 * hd + 1]

    def any_weight_left():
        return (jnp.min(carry_ref[...]) < F32_EXP_UNDERFLOW).astype(jnp.int32)

    def more_tiles(state):
        j, live = state
        return jnp.logical_and(j < qi, live > 0)

    def earlier(state):
        j, _ = state
        kblk, vblk = key_value(qi - 1 - j)
        outs, sums = _stick_steps(
            [_StickStep(qh[hd], kblk, vblk, None, carry_ref[hd], None) for hd in range(heads)],
            upper)
        for hd in range(heads):
            acc_ref[hd] += outs[hd]
            carry_ref[hd] += sums[hd]
        return j + 1, any_weight_left()

    lax.while_loop(more_tiles, earlier, (jnp.int32(1), any_weight_left()))
    out = acc_ref[0]
    for hd in range(1, heads):
        out = jnp.where(lane_head == hd, acc_ref[hd], out)
    o_ref[...] = out.astype(o_ref.dtype)


def _stick_breaking(q, k, v):
    batch, seq, width = q.shape
    tile = STICK_TILE
    j = np.arange(tile)
    upper = jnp.asarray((j[:, None] > j[None, :]), BF16)
    return pl.pallas_call(
        _stick_body,
        grid=(batch, width // LANES, seq // tile),
        in_specs=[pl.BlockSpec((None, tile, LANES), lambda b, p, i: (b, i, p)),
                  pl.BlockSpec((None, seq, LANES), lambda b, p, i: (b, 0, p)),
                  pl.BlockSpec((None, seq, LANES), lambda b, p, i: (b, 0, p)),
                  _resident((tile, tile))],
        out_specs=pl.BlockSpec((None, tile, LANES), lambda b, p, i: (b, i, p)),
        out_shape=jax.ShapeDtypeStruct(q.shape, BF16),
        scratch_shapes=[pltpu.VMEM((LANES // HEAD_DIM, tile, LANES), F32),
                        pltpu.VMEM((LANES // HEAD_DIM, tile, 1), F32)],
        compiler_params=_compiler_params(("parallel", "parallel", "parallel")),
        name="stick_breaking",
    )(q, k, v, upper)


def _post_mix_body(x_ref, gates_ref, yb_ref, *refs):
    group_refs = refs[:2 * N_GROUPS]
    wa_ref, wb_ref, wo_ref, g_ref, out_ref, slab_ref = refs[2 * N_GROUPS:]
    tile, d_model = x_ref.shape

    def natural_order(ref, dil):
        if dil == 1:
            return ref[...]
        rows = tile // dil
        for r in range(dil):
            for s in range(GROUP_W // LANES):
                lane0 = r * GROUP_W + s * LANES
                slab_ref[s, pl.ds(r, rows, stride=dil), :] = ref[:, lane0:lane0 + LANES]
        return jnp.concatenate([slab_ref[s] for s in range(GROUP_W // LANES)], axis=-1)

    outs, lses = [], []
    for g, (_, dil) in enumerate(DILATED_GROUPS):
        outs.append(natural_order(group_refs[2 * g], dil))
        lses.append(natural_order(group_refs[2 * g + 1], dil))
    top = functools.reduce(jnp.maximum, lses)
    wts = [jnp.exp(l - top) for l in lses]
    ya = sum(w * o for w, o in zip(wts, outs)) / sum(wts)

    branch_a = jnp.dot(ya.astype(BF16), wa_ref[...], preferred_element_type=F32)
    branch_b = jnp.dot(yb_ref[...], wb_ref[...], preferred_element_type=F32)
    gate_a = gates_ref[:, :d_model].astype(F32)
    gate_b = gates_ref[:, d_model:].astype(F32)
    merged = jax.nn.sigmoid(gate_a) * branch_a + jax.nn.sigmoid(gate_b) * branch_b
    mixed = jnp.dot(merged.astype(BF16), wo_ref[...], preferred_element_type=F32)
    out_ref[...] = x_ref[...] + _rms_norm(mixed, g_ref[...])


def _post_mix(x, gates, yb, groups, wa, wb, wo, g):
    batch, seq, d_model = x.shape
    tile = TOKEN_TILE
    row_spec = lambda width: pl.BlockSpec((None, tile, width), lambda b, i: (b, i, 0))
    in_specs = [row_spec(d_model), row_spec(gates.shape[-1]), row_spec(yb.shape[-1])]
    operands = [x, gates, yb]
    for (o, lse), (_, dil) in zip(groups, DILATED_GROUPS):
        for arr in (o, lse):
            in_specs.append(pl.BlockSpec((None, tile // dil, dil * GROUP_W), lambda b, i: (b, i, 0)))
            operands.append(arr)
    for arr in (wa, wb, wo, g):
        in_specs.append(_resident(arr.shape))
        operands.append(arr)
    return pl.pallas_call(
        _post_mix_body,
        grid=(batch, seq // tile),
        in_specs=in_specs,
        out_specs=row_spec(d_model),
        out_shape=jax.ShapeDtypeStruct(x.shape, F32),
        scratch_shapes=[pltpu.VMEM((GROUP_W // LANES, tile, LANES), F32)],
        compiler_params=_compiler_params(("parallel", "parallel")),
        name="post_mix",
    )(*operands)


def _ffn_body(x_ref, g_pre_ref, w_up_ref, cw_ref, cb_ref, w_down_ref, g_post_ref, out_ref,
              halo_ref, u_ref, acc_ref):
    tile = x_ref.shape[0]
    d_ff = w_down_ref.shape[0]

    @pl.when(pl.program_id(1) == 0)
    def _():
        halo_ref[...] = jnp.zeros_like(halo_ref)

    x = x_ref[...]
    h = _rms_norm(x, g_pre_ref[...]).astype(BF16)

    def up(c):
        return [jnp.dot(h, w_up_ref[:, col:col + FF_CHUNK], preferred_element_type=F32)
                for col in (c * FF_CHUNK, d_ff + c * FF_CHUNK)]

    def conv(u, col, slot):
        cols = slice(col, col + FF_CHUNK)
        u_ref[slot, :CONV_HALO, :] = halo_ref[:, cols]
        u_ref[slot, CONV_HALO:, :] = u
        halo_ref[:, cols] = u[tile - CONV_HALO:, :]
        y = cb_ref[:, cols]
        for t in range(CONV_WIDTH):
            shift = CONV_WIDTH - 1 - t
            y = y + (cw_ref[t:t + 1, cols]
                     * u_ref[slot, CONV_HALO - shift:CONV_HALO - shift + tile, :])
        return y

    n_chunks = d_ff // FF_CHUNK
    u_next = up(0)
    for c in range(n_chunks):
        u_gate, u_val = u_next
        if c + 1 < n_chunks:
            u_next = up(c + 1)
        gate = conv(u_gate, c * FF_CHUNK, 0)
        val = conv(u_val, d_ff + c * FF_CHUNK, 1)
        act = (jax.nn.gelu(gate, approximate=True) * val).astype(BF16)
        part = jnp.dot(act, w_down_ref[c * FF_CHUNK:(c + 1) * FF_CHUNK, :],
                       preferred_element_type=F32)
        if c == 0:
            acc_ref[...] = part
        else:
            acc_ref[...] += part
    out_ref[...] = x + _rms_norm(acc_ref[...], g_post_ref[...])


def _ffn(x, g_pre, w_up, conv_w, conv_b, w_down, g_post):
    batch, seq, d_model = x.shape
    tile = TOKEN_TILE
    row_spec = pl.BlockSpec((None, tile, d_model), lambda b, i: (b, i, 0))
    consts = (g_pre, w_up, conv_w, conv_b, w_down, g_post)
    return pl.pallas_call(
        _ffn_body,
        grid=(batch, seq // tile),
        in_specs=[row_spec] + [_resident(a.shape) for a in consts],
        out_specs=row_spec,
        out_shape=jax.ShapeDtypeStruct(x.shape, F32),
        scratch_shapes=[pltpu.VMEM((CONV_HALO, w_up.shape[1]), F32),
                        pltpu.VMEM((2, CONV_HALO + tile, FF_CHUNK), F32),
                        pltpu.VMEM((tile, d_model), F32)],
        compiler_params=_compiler_params(("arbitrary", "arbitrary")),
        name="ffn",
    )(x, *consts)


def kernel(x, g_pre_mix, w_in, rel_bias, w_branch_a, w_branch_b, w_out, g_post_mix, g_pre_ffn,
           w_up, conv_w, conv_b, w_down, g_post_ffn):
    depth = w_in.shape[0]
    for l in range(depth):
        proj = _in_proj(x, g_pre_mix[l][None], w_in[l].astype(BF16))
        qb, kb, vb, gates = proj[3 * N_GROUPS:]
        groups = []
        for g, (window, dil) in enumerate(DILATED_GROUPS):
            q, k, v = proj[3 * g:3 * g + 3]
            band = _bias_band(rel_bias, g, dil, window // dil)
            groups.append(_dilated_group(q, k, v, band, dil))
        yb = _stick_breaking(qb, kb, vb)
        x = _post_mix(x, gates, yb, groups, w_branch_a[l].astype(BF16), w_branch_b[l].astype(BF16),
                      w_out[l].astype(BF16), g_post_mix[l][None])
        x = _ffn(x, g_pre_ffn[l][None], w_up[l].astype(BF16), conv_w[l], conv_b[l][None],
                 w_down[l].astype(BF16), g_post_ffn[l][None])
    return x
```

```python
import functools
import math
from typing import Any, NamedTuple, Optional

import numpy as np
import jax
import jax.numpy as jnp
from jax import lax
from jax.experimental import pallas as pl
from jax.experimental.pallas import tpu as pltpu

F32 = jnp.float32
BF16 = jnp.bfloat16

HEAD_DIM = 64
DILATED_GROUPS = ((128, 1), (512, 4), (2048, 16))
HEADS_PER_GROUP = 4
N_GROUPS = len(DILATED_GROUPS)
N_HEADS_A = HEADS_PER_GROUP * N_GROUPS
N_HEADS_B = 8
GROUP_W = HEADS_PER_GROUP * HEAD_DIM
A_QKV = N_HEADS_A * HEAD_DIM
B_QKV = N_HEADS_B * HEAD_DIM
BLOCK = 128
NUM_BUCKETS = 32
MAX_DISTANCE = 2048
CONV_WIDTH = 3
EPS = 1e-6
QK_SCALE = 1.0 / math.sqrt(HEAD_DIM)
MASKED = -1e30
F32_EXP_UNDERFLOW = 104.0

LANES = 128
SUBLANES = 8
VMEM_LIMIT_BYTES = 56 * 1024 * 1024

TOKEN_TILE = 512
STICK_TILE = 256
FF_CHUNK = 256
DOWN_GROUP = 3


def _t5_bucket(dist):
    max_exact = NUM_BUCKETS // 2
    n = np.asarray(dist, dtype=np.float32)
    large = max_exact + (np.log(np.maximum(n, 1.0) / max_exact)
                         / np.log(MAX_DISTANCE / max_exact)
                         * (NUM_BUCKETS - max_exact)).astype(np.int32)
    large = np.minimum(large, NUM_BUCKETS - 1)
    return np.where(n < max_exact, n.astype(np.int32), large).astype(np.int32)


def _rms_norm(x, g):
    return x * lax.rsqrt(jnp.mean(x * x, axis=-1, keepdims=True) + EPS) * g


def _compiler_params(semantics):
    return pltpu.CompilerParams(dimension_semantics=semantics, vmem_limit_bytes=VMEM_LIMIT_BYTES)


def _resident(shape):
    return pl.BlockSpec(shape, lambda *_: (0,) * len(shape), pipeline_mode=pl.Buffered(1))


def _in_proj_body(x_ref, g_ref, w_ref, *refs):
    a_refs = refs[:3 * N_GROUPS]
    qb_ref, kb_ref, vb_ref, gates_ref, slab_ref = refs[3 * N_GROUPS:]
    tile = x_ref.shape[0]
    h = _rms_norm(x_ref[...], g_ref[...]).astype(BF16)

    def proj(col, width):
        return jnp.dot(h, w_ref[:, col:col + width], preferred_element_type=F32)

    for t in range(3):
        scale = QK_SCALE if t == 0 else 1.0
        for g, (_, dil) in enumerate(DILATED_GROUPS):
            acc = proj(t * A_QKV + g * GROUP_W, GROUP_W) * scale
            out_ref = a_refs[3 * g + t]
            if dil == 1:
                out_ref[...] = acc.astype(BF16)
                continue
            rows = tile // dil
            for s in range(GROUP_W // LANES):
                slab_ref[s] = acc[:, s * LANES:(s + 1) * LANES]
            for r in range(dil):
                for s in range(GROUP_W // LANES):
                    lane0 = r * GROUP_W + s * LANES
                    out_ref[:, lane0:lane0 + LANES] = (
                        slab_ref[s, pl.ds(r, rows, stride=dil), :].astype(BF16))

    base = 3 * A_QKV
    qb_ref[...] = (proj(base, B_QKV) * QK_SCALE).astype(BF16)
    kb_ref[...] = proj(base + B_QKV, B_QKV).astype(BF16)
    vb_ref[...] = proj(base + 2 * B_QKV, B_QKV).astype(BF16)
    base += 3 * B_QKV
    gate_w = gates_ref.shape[1]
    for c in range(0, gate_w, 512):
        gates_ref[:, c:c + 512] = proj(base + c, 512).astype(BF16)


def _in_proj(x, g, w):
    batch, seq, d_model = x.shape
    tile = TOKEN_TILE
    n_in = w.shape[1]
    gate_w = n_in - 3 * A_QKV - 3 * B_QKV
    out_shape, out_specs = [], []
    for _, dil in DILATED_GROUPS:
        for _ in range(3):
            out_shape.append(jax.ShapeDtypeStruct((batch, seq // dil, dil * GROUP_W), BF16))
            out_specs.append(pl.BlockSpec((None, tile // dil, dil * GROUP_W), lambda b, i: (b, i, 0)))
    for _ in range(3):
        out_shape.append(jax.ShapeDtypeStruct((batch, seq, B_QKV), BF16))
        out_specs.append(pl.BlockSpec((None, tile, B_QKV), lambda b, i: (b, i, 0)))
    out_shape.append(jax.ShapeDtypeStruct((batch, seq, gate_w), BF16))
    out_specs.append(pl.BlockSpec((None, tile, gate_w), lambda b, i: (b, i, 0)))
    return pl.pallas_call(
        _in_proj_body,
        grid=(batch, seq // tile),
        in_specs=[pl.BlockSpec((None, tile, d_model), lambda b, i: (b, i, 0)),
                  _resident((1, d_model)),
                  _resident((d_model, n_in))],
        out_specs=out_specs,
        out_shape=out_shape,
        scratch_shapes=[pltpu.VMEM((GROUP_W // LANES, tile, LANES), F32)],
        compiler_params=_compiler_params(("parallel", "parallel")),
        name="in_proj",
    )(x, g, w)


def _dilated_body(q_ref, k_ref, v_ref, kp_ref, vp_ref, bias_ref, o_ref, lse_ref, *, blocks):
    missing_cols = jnp.where(pl.program_id(2) == 0, BLOCK, 0)
    lane_head = lax.broadcasted_iota(jnp.int32, (BLOCK, GROUP_W), 1) // HEAD_DIM
    key_col = lax.broadcasted_iota(jnp.int32, (BLOCK, 2 * BLOCK), 1)
    for nb in range(blocks):
        rows = slice(nb * BLOCK, (nb + 1) * BLOCK)
        q = q_ref[rows, :]
        if nb == 0:
            kband = jnp.concatenate([kp_ref[...], k_ref[rows, :]], axis=0)
            vband = jnp.concatenate([vp_ref[...], v_ref[rows, :]], axis=0)
        else:
            band = slice((nb - 1) * BLOCK, (nb + 1) * BLOCK)
            kband, vband = k_ref[band, :], v_ref[band, :]
        heads = range(HEADS_PER_GROUP)
        logits = []
        for hd in heads:
            qh = jnp.where(lane_head == hd, q, jnp.zeros_like(q))
            s = lax.dot_general(qh, kband, (((1,), (1,)), ((), ())),
                                preferred_element_type=F32) + bias_ref[hd]
            if nb == 0:
                s = jnp.where(key_col < missing_cols, MASKED, s)
            logits.append(s)
        mx = [jnp.max(s, axis=-1, keepdims=True) for s in logits]
        e = [jnp.exp(s - m) for s, m in zip(logits, mx)]
        den = [jnp.sum(p, axis=-1, keepdims=True) for p in e]
        pv = [jnp.dot(p.astype(BF16), vband, preferred_element_type=F32) for p in e]
        o_acc = pv[0] / den[0]
        lse_acc = jnp.broadcast_to(mx[0] + jnp.log(den[0]), (BLOCK, GROUP_W))
        for hd in heads[1:]:
            in_head = lane_head == hd
            o_acc = jnp.where(in_head, pv[hd] / den[hd], o_acc)
            lse_acc = jnp.where(in_head, mx[hd] + jnp.log(den[hd]), lse_acc)
        o_ref[rows, :] = o_acc
        lse_ref[rows, :] = lse_acc


def _dilated_group(q, k, v, bias_band, dil):
    batch, sub_seq, _ = q.shape
    blocks = min(8, sub_seq // BLOCK)
    tile = blocks * BLOCK
    cur = pl.BlockSpec((None, tile, GROUP_W), lambda b, r, n: (b, n, r))
    prev = pl.BlockSpec((None, BLOCK, GROUP_W),
                        lambda b, r, n: (b, jnp.maximum(n * blocks - 1, 0), r))
    out_sds = jax.ShapeDtypeStruct(q.shape, F32)
    return pl.pallas_call(
        functools.partial(_dilated_body, blocks=blocks),
        grid=(batch, dil, sub_seq // tile),
        in_specs=[cur, cur, cur, prev, prev, _resident(bias_band.shape)],
        out_specs=[cur, cur],
        out_shape=[out_sds, out_sds],
        compiler_params=_compiler_params(("parallel", "parallel", "parallel")),
        name=f"dilated_d{dil}",
    )(q, k, v, k, v, bias_band)


def _bias_band(rel_bias, group, dil, n_back):
    heads = slice(group * HEADS_PER_GROUP, (group + 1) * HEADS_PER_GROUP)
    buckets = _t5_bucket(np.arange(n_back + 1) * dil)
    bias = rel_bias[buckets][:, heads].T.astype(F32)
    period = 2 * BLOCK + 1
    j = BLOCK - np.arange(period)
    local = (j >= 0) & (j <= n_back)
    diag = jnp.where(local[None], bias[:, np.clip(j, 0, n_back)], MASKED)
    band = jnp.tile(diag, (1, BLOCK))[:, :BLOCK * 2 * BLOCK]
    return band.reshape(HEADS_PER_GROUP, BLOCK, 2 * BLOCK)


class _StickStep(NamedTuple):
    q: jax.Array
    k: jax.Array
    v: jax.Array
    mask: Optional[jax.Array]
    carry: Any
    after: Optional[int]


def _stick_steps(steps, upper):
    z = [lax.dot_general(st.q, st.k, (((1,), (1,)), ((), ())), preferred_element_type=F32)
         for st in steps]
    soft = [jnp.maximum(s, 0.0) + jnp.log(1.0 + jnp.exp(-jnp.abs(s))) for s in z]
    log_beta = [s - p for s, p in zip(z, soft)]
    soft = [p if st.mask is None else jnp.where(st.mask, p, 0.0) for p, st in zip(soft, steps)]
    hi = [p.astype(BF16) for p in soft]
    lo = [(p - h.astype(F32)).astype(BF16) for p, h in zip(soft, hi)]
    tail = [jnp.dot(h, upper, preferred_element_type=F32) + jnp.dot(l, upper, preferred_element_type=F32)
            for h, l in zip(hi, lo)]
    sums = [jnp.sum(p, axis=-1, keepdims=True) for p in soft]
    weights = []
    for i, st in enumerate(steps):
        carry = st.carry if st.after is None else st.carry + sums[st.after]
        w = jnp.exp(log_beta[i] - (tail[i] + carry))
        weights.append(w if st.mask is None else jnp.where(st.mask, w, 0.0))
    outs = [jnp.dot(w.astype(BF16), st.v, preferred_element_type=F32)
            for w, st in zip(weights, steps)]
    return outs, sums


def _stick_body(q_ref, k_ref, v_ref, upper_ref, o_ref, acc_ref, carry_ref):
    tile = q_ref.shape[0]
    heads = LANES // HEAD_DIM
    qi = pl.program_id(2)
    q = q_ref[...]
    lane_head = lax.broadcasted_iota(jnp.int32, (tile, LANES), 1) // HEAD_DIM
    row = lax.broadcasted_iota(jnp.int32, (tile, tile), 0)
    col = lax.broadcasted_iota(jnp.int32, (tile, tile), 1)
    strictly_before = col < row
    upper = upper_ref[...]
    qh = [jnp.where(lane_head == hd, q, jnp.zeros_like(q)) for hd in range(heads)]

    def key_value(kt):
        start = pl.multiple_of(kt * tile, tile)
        return k_ref[pl.ds(start, tile), :], v_ref[pl.ds(start, tile), :]

    k_diag, v_diag = key_value(qi)
    k_prev, v_prev = key_value(jnp.maximum(qi - 1, 0))
    v_prev = jnp.where(qi > 0, v_prev, jnp.zeros_like(v_prev))
    steps = []
    for hd in range(heads):
        steps.append(_StickStep(qh[hd], k_diag, v_diag, strictly_before, 0.0, None))
        steps.append(_StickStep(qh[hd], k_prev, v_prev, None, 0.0, 2 * hd))
    outs, sums = _stick_steps(steps, upper)
    for hd in range(heads):
        acc_ref[hd] = outs[2 * hd] + outs[2 * hd + 1]
        carry_ref[hd] = sums[2 * hd] + sums[2 * hd + 1]

    def any_weight_left():
        return (jnp.min(carry_ref[...]) < F32_EXP_UNDERFLOW).astype(jnp.int32)

    def more_tiles(state):
        j, live = state
        return jnp.logical_and(j < qi, live > 0)

    def earlier(state):
        j, _ = state
        kblk, vblk = key_value(qi - 1 - j)
        outs, sums = _stick_steps(
            [_StickStep(qh[hd], kblk, vblk, None, carry_ref[hd], None) for hd in range(heads)],
            upper)
        for hd in range(heads):
            acc_ref[hd] += outs[hd]
            carry_ref[hd] += sums[hd]
        return j + 1, any_weight_left()

    lax.while_loop(more_tiles, earlier, (jnp.int32(1), any_weight_left()))
    out = acc_ref[0]
    for hd in range(1, heads):
        out = jnp.where(lane_head == hd, acc_ref[hd], out)
    o_ref[...] = out.astype(o_ref.dtype)


def _stick_breaking(q, k, v):
    batch, seq, width = q.shape
    tile = STICK_TILE
    j = np.arange(tile)
    upper = jnp.asarray((j[:, None] > j[None, :]), BF16)
    return pl.pallas_call(
        _stick_body,
        grid=(batch, width // LANES, seq // tile),
        in_specs=[pl.BlockSpec((None, tile, LANES), lambda b, p, i: (b, i, p)),
                  pl.BlockSpec((None, seq, LANES), lambda b, p, i: (b, 0, p)),
                  pl.BlockSpec((None, seq, LANES), lambda b, p, i: (b, 0, p)),
                  _resident((tile, tile))],
        out_specs=pl.BlockSpec((None, tile, LANES), lambda b, p, i: (b, i, p)),
        out_shape=jax.ShapeDtypeStruct(q.shape, BF16),
        scratch_shapes=[pltpu.VMEM((LANES // HEAD_DIM, tile, LANES), F32),
                        pltpu.VMEM((LANES // HEAD_DIM, tile, 1), F32)],
        compiler_params=_compiler_params(("parallel", "parallel", "parallel")),
        name="stick_breaking",
    )(q, k, v, upper)


def _post_mix_body(x_ref, gates_ref, yb_ref, *refs):
    group_refs = refs[:2 * N_GROUPS]
    wa_ref, wb_ref, wo_ref, g_ref, out_ref, slab_ref = refs[2 * N_GROUPS:]
    tile, d_model = x_ref.shape

    def natural_order(ref, dil):
        if dil == 1:
            return ref[...]
        rows = tile // dil
        for r in range(dil):
            for s in range(GROUP_W // LANES):
                lane0 = r * GROUP_W + s * LANES
                slab_ref[s, pl.ds(r, rows, stride=dil), :] = ref[:, lane0:lane0 + LANES]
        return jnp.concatenate([slab_ref[s] for s in range(GROUP_W // LANES)], axis=-1)

    def sigmoid(v):
        return 0.5 * jnp.tanh(0.5 * v) + 0.5

    branch_b = jnp.dot(yb_ref[...], wb_ref[...], preferred_element_type=F32)
    outs, lses = [], []
    for g, (_, dil) in enumerate(DILATED_GROUPS):
        outs.append(natural_order(group_refs[2 * g], dil))
        lses.append(natural_order(group_refs[2 * g + 1], dil))
    top = functools.reduce(jnp.maximum, lses)
    wts = [jnp.exp(l - top) for l in lses]
    ya = sum(w * o for w, o in zip(wts, outs)) / sum(wts)

    branch_a = jnp.dot(ya.astype(BF16), wa_ref[...], preferred_element_type=F32)
    gate_a = gates_ref[:, :d_model].astype(F32)
    gate_b = gates_ref[:, d_model:].astype(F32)
    merged = sigmoid(gate_a) * branch_a + sigmoid(gate_b) * branch_b
    mixed = jnp.dot(merged.astype(BF16), wo_ref[...], preferred_element_type=F32)
    out_ref[...] = x_ref[...] + _rms_norm(mixed, g_ref[...])


def _post_mix(x, gates, yb, groups, wa, wb, wo, g):
    batch, seq, d_model = x.shape
    tile = TOKEN_TILE
    row_spec = lambda width: pl.BlockSpec((None, tile, width), lambda b, i: (b, i, 0))
    in_specs = [row_spec(d_model), row_spec(gates.shape[-1]), row_spec(yb.shape[-1])]
    operands = [x, gates, yb]
    for (o, lse), (_, dil) in zip(groups, DILATED_GROUPS):
        for arr in (o, lse):
            in_specs.append(pl.BlockSpec((None, tile // dil, dil * GROUP_W), lambda b, i: (b, i, 0)))
            operands.append(arr)
    for arr in (wa, wb, wo, g):
        in_specs.append(_resident(arr.shape))
        operands.append(arr)
    return pl.pallas_call(
        _post_mix_body,
        grid=(batch, seq // tile),
        in_specs=in_specs,
        out_specs=row_spec(d_model),
        out_shape=jax.ShapeDtypeStruct(x.shape, F32),
        scratch_shapes=[pltpu.VMEM((GROUP_W // LANES, tile, LANES), F32)],
        compiler_params=_compiler_params(("parallel", "parallel")),
        name="post_mix",
    )(*operands)


def _ffn_body(x_ref, g_pre_ref, w_up_ref, cw_ref, cb_ref, w_down_ref, g_post_ref, out_ref,
              halo_ref, u_ref, act_ref, acc_ref, slab_ref):
    tile, d_model = x_ref.shape
    d_ff = w_down_ref.shape[0]
    seg = tile // SUBLANES
    n_slabs = d_model // LANES
    halo_rows = (CONV_WIDTH - 1) * SUBLANES

    @pl.when(pl.program_id(1) == 0)
    def _():
        halo_ref[...] = jnp.zeros_like(halo_ref)

    for c in range(n_slabs):
        for s in range(SUBLANES):
            slab_ref[c, pl.ds(s, seg, stride=SUBLANES), :] = (
                x_ref[s * seg:(s + 1) * seg, c * LANES:(c + 1) * LANES])
    x = jnp.concatenate([slab_ref[c] for c in range(n_slabs)], axis=-1)
    h = _rms_norm(x, g_pre_ref[...]).astype(BF16)
    sublane = lax.broadcasted_iota(jnp.int32, (SUBLANES, FF_CHUNK), 0)

    def up(c):
        return [jnp.dot(h, w_up_ref[:, col:col + FF_CHUNK], preferred_element_type=F32)
                for col in (c * FF_CHUNK, d_ff + c * FF_CHUNK)]

    def conv(u, col, slot):
        cols = slice(col, col + FF_CHUNK)
        last = u[tile - halo_rows:, :]
        for k in range(CONV_WIDTH - 1):
            group = slice(k * SUBLANES, (k + 1) * SUBLANES)
            mixed = jnp.where(sublane == SUBLANES - 1, halo_ref[group, cols], last[group, :])
            u_ref[slot, group, :] = pltpu.roll(mixed, 1, axis=0)
        u_ref[slot, halo_rows:, :] = u
        halo_ref[:, cols] = last
        y = cb_ref[:, cols]
        for t in range(CONV_WIDTH):
            start = t * SUBLANES
            y = y + cw_ref[t:t + 1, cols] * u_ref[slot, start:start + tile, :]
        return y

    n_chunks = d_ff // FF_CHUNK
    u_next = up(0)
    first = 0
    for c in range(n_chunks):
        u_gate, u_val = u_next
        if c + 1 < n_chunks:
            u_next = up(c + 1)
        gate = conv(u_gate, c * FF_CHUNK, 0)
        val = conv(u_val, d_ff + c * FF_CHUNK, 1)
        act_ref[:, c * FF_CHUNK:(c + 1) * FF_CHUNK] = (
            jax.nn.gelu(gate, approximate=True) * val).astype(BF16)
        if (c + 1) % DOWN_GROUP == 0 or c + 1 == n_chunks:
            rows = slice(first * FF_CHUNK, (c + 1) * FF_CHUNK)
            part = jnp.dot(act_ref[:, rows], w_down_ref[rows, :], preferred_element_type=F32)
            if first == 0:
                acc_ref[...] = part
            else:
                acc_ref[...] += part
            first = c + 1
    y = x + _rms_norm(acc_ref[...], g_post_ref[...])
    for c in range(n_slabs):
        slab_ref[c] = y[:, c * LANES:(c + 1) * LANES]
    for c in range(n_slabs):
        for s in range(SUBLANES):
            out_ref[s * seg:(s + 1) * seg, c * LANES:(c + 1) * LANES] = (
                slab_ref[c, pl.ds(s, seg, stride=SUBLANES), :])


def _ffn(x, g_pre, w_up, conv_w, conv_b, w_down, g_post):
    batch, seq, d_model = x.shape
    tile = TOKEN_TILE
    d_ff = w_down.shape[0]
    row_spec = pl.BlockSpec((None, tile, d_model), lambda b, i: (b, i, 0))
    consts = (g_pre, w_up, conv_w, conv_b, w_down, g_post)
    halo_rows = (CONV_WIDTH - 1) * SUBLANES
    return pl.pallas_call(
        _ffn_body,
        grid=(batch, seq // tile),
        in_specs=[row_spec] + [_resident(a.shape) for a in consts],
        out_specs=row_spec,
        out_shape=jax.ShapeDtypeStruct(x.shape, F32),
        scratch_shapes=[pltpu.VMEM((halo_rows, 2 * d_ff), F32),
                        pltpu.VMEM((2, halo_rows + tile, FF_CHUNK), F32),
                        pltpu.VMEM((tile, d_ff), BF16),
                        pltpu.VMEM((tile, d_model), F32),
                        pltpu.VMEM((d_model // LANES, tile, LANES), F32)],
        compiler_params=_compiler_params(("arbitrary", "arbitrary")),
        name="ffn",
    )(x, *consts)


def kernel(x, g_pre_mix, w_in, rel_bias, w_branch_a, w_branch_b, w_out, g_post_mix, g_pre_ffn,
           w_up, conv_w, conv_b, w_down, g_post_ffn):
    depth = w_in.shape[0]
    for l in range(depth):
        proj = _in_proj(x, g_pre_mix[l][None], w_in[l].astype(BF16))
        qb, kb, vb, gates = proj[3 * N_GROUPS:]
        groups = []
        for g, (window, dil) in enumerate(DILATED_GROUPS):
            q, k, v = proj[3 * g:3 * g + 3]
            band = _bias_band(rel_bias, g, dil, window // dil)
            groups.append(_dilated_group(q, k, v, band, dil))
        yb = _stick_breaking(qb, kb, vb)
        x = _post_mix(x, gates, yb, groups, w_branch_a[l].astype(BF16), w_branch_b[l].astype(BF16),
                      w_out[l].astype(BF16), g_post_mix[l][None])
        x = _ffn(x, g_pre_ffn[l][None], w_up[l].astype(BF16), conv_w[l], conv_b[l][None],
                 w_down[l].astype(BF16), g_post_ffn[l][None])
    return x
```

```python
import functools
import math
from typing import Any, NamedTuple, Optional

import numpy as np
import jax
import jax.numpy as jnp
from jax import lax
from jax.experimental import pallas as pl
from jax.experimental.pallas import tpu as pltpu

F32 = jnp.float32
BF16 = jnp.bfloat16

HEAD_DIM = 64
DILATED_GROUPS = ((128, 1), (512, 4), (2048, 16))
HEADS_PER_GROUP = 4
N_GROUPS = len(DILATED_GROUPS)
N_HEADS_A = HEADS_PER_GROUP * N_GROUPS
N_HEADS_B = 8
GROUP_W = HEADS_PER_GROUP * HEAD_DIM
A_QKV = N_HEADS_A * HEAD_DIM
B_QKV = N_HEADS_B * HEAD_DIM
BLOCK = 128
NUM_BUCKETS = 32
MAX_DISTANCE = 2048
CONV_WIDTH = 3
EPS = 1e-6
QK_SCALE = 1.0 / math.sqrt(HEAD_DIM)
MASKED = -1e30
F32_EXP_UNDERFLOW = 104.0

LANES = 128
SUBLANES = 8
VMEM_LIMIT_BYTES = 56 * 1024 * 1024

TOKEN_TILE = 512
DILATED_BLOCKS_PER_STEP = 8
STICK_TILE = 256
FF_CHUNK = 256
DOWN_GROUP = 3


def _t5_bucket(dist):
    max_exact = NUM_BUCKETS // 2
    n = np.asarray(dist, dtype=np.float32)
    large = max_exact + (np.log(np.maximum(n, 1.0) / max_exact)
                         / np.log(MAX_DISTANCE / max_exact)
                         * (NUM_BUCKETS - max_exact)).astype(np.int32)
    large = np.minimum(large, NUM_BUCKETS - 1)
    return np.where(n < max_exact, n.astype(np.int32), large).astype(np.int32)


def _rms_norm(x, g):
    return x * lax.rsqrt(jnp.mean(x * x, axis=-1, keepdims=True) + EPS) * g


def _compiler_params(semantics):
    return pltpu.CompilerParams(dimension_semantics=semantics, vmem_limit_bytes=VMEM_LIMIT_BYTES)


def _resident(shape):
    return pl.BlockSpec(shape, lambda *_: (0,) * len(shape), pipeline_mode=pl.Buffered(1))


def _in_proj_body(x_ref, g_ref, w_ref, *refs):
    a_refs = refs[:3 * N_GROUPS]
    qb_ref, kb_ref, vb_ref, gates_ref, slab_ref = refs[3 * N_GROUPS:]
    tile = x_ref.shape[0]
    h = _rms_norm(x_ref[...], g_ref[...]).astype(BF16)

    def proj(col, width):
        return jnp.dot(h, w_ref[:, col:col + width], preferred_element_type=F32)

    for t in range(3):
        scale = QK_SCALE if t == 0 else 1.0
        for g, (_, dil) in enumerate(DILATED_GROUPS):
            acc = proj(t * A_QKV + g * GROUP_W, GROUP_W) * scale
            out_ref = a_refs[3 * g + t]
            if dil == 1:
                out_ref[...] = acc.astype(BF16)
                continue
            rows = tile // dil
            for s in range(GROUP_W // LANES):
                slab_ref[s] = acc[:, s * LANES:(s + 1) * LANES]
            for r in range(dil):
                for s in range(GROUP_W // LANES):
                    lane0 = r * GROUP_W + s * LANES
                    out_ref[:, lane0:lane0 + LANES] = (
                        slab_ref[s, pl.ds(r, rows, stride=dil), :].astype(BF16))

    base = 3 * A_QKV
    qb_ref[...] = (proj(base, B_QKV) * QK_SCALE).astype(BF16)
    kb_ref[...] = proj(base + B_QKV, B_QKV).astype(BF16)
    vb_ref[...] = proj(base + 2 * B_QKV, B_QKV).astype(BF16)
    base += 3 * B_QKV
    gate_w = gates_ref.shape[1]
    for c in range(0, gate_w, 512):
        gates_ref[:, c:c + 512] = proj(base + c, 512).astype(BF16)


def _in_proj(x, g, w):
    batch, seq, d_model = x.shape
    tile = TOKEN_TILE
    n_in = w.shape[1]
    gate_w = n_in - 3 * A_QKV - 3 * B_QKV
    out_shape, out_specs = [], []
    for _, dil in DILATED_GROUPS:
        for _ in range(3):
            out_shape.append(jax.ShapeDtypeStruct((batch, seq // dil, dil * GROUP_W), BF16))
            out_specs.append(pl.BlockSpec((None, tile // dil, dil * GROUP_W), lambda b, i: (b, i, 0)))
    for _ in range(3):
        out_shape.append(jax.ShapeDtypeStruct((batch, seq, B_QKV), BF16))
        out_specs.append(pl.BlockSpec((None, tile, B_QKV), lambda b, i: (b, i, 0)))
    out_shape.append(jax.ShapeDtypeStruct((batch, seq, gate_w), BF16))
    out_specs.append(pl.BlockSpec((None, tile, gate_w), lambda b, i: (b, i, 0)))
    return pl.pallas_call(
        _in_proj_body,
        grid=(batch, seq // tile),
        in_specs=[pl.BlockSpec((None, tile, d_model), lambda b, i: (b, i, 0)),
                  _resident((1, d_model)),
                  _resident((d_model, n_in))],
        out_specs=out_specs,
        out_shape=out_shape,
        scratch_shapes=[pltpu.VMEM((GROUP_W // LANES, tile, LANES), F32)],
        compiler_params=_compiler_params(("parallel", "parallel")),
        name="in_proj",
    )(x, g, w)


def _dilated_body(q_ref, k_ref, v_ref, kp_ref, vp_ref, bias_ref, o_ref, lse_ref, *,
                  blocks, residues):
    heads = HEADS_PER_GROUP
    missing_cols = jnp.where(pl.program_id(2) == 0, BLOCK, 0)
    lane_head = lax.broadcasted_iota(jnp.int32, (BLOCK, GROUP_W), 1) // HEAD_DIM
    key_col = lax.broadcasted_iota(jnp.int32, (heads * BLOCK, 2 * BLOCK), 1)
    bias = bias_ref[...]

    def keys_values(res):
        lanes = slice(res * GROUP_W, (res + 1) * GROUP_W)
        k_all = jnp.concatenate([kp_ref[:, lanes], k_ref[:, lanes]], axis=0)
        v_all = jnp.concatenate([vp_ref[:, lanes], v_ref[:, lanes]], axis=0)
        v_lane_head = lax.broadcasted_iota(jnp.int32, v_all.shape, 1) // HEAD_DIM
        v_heads = [jnp.where(v_lane_head == hd, v_all, jnp.zeros_like(v_all))
                   for hd in range(heads)]
        return k_all, v_heads

    def scores(res, nb, k_all):
        q = q_ref[nb * BLOCK:(nb + 1) * BLOCK, res * GROUP_W:(res + 1) * GROUP_W]
        q_stack = jnp.concatenate(
            [jnp.where(lane_head == hd, q, jnp.zeros_like(q)) for hd in range(heads)], axis=0)
        s = lax.dot_general(q_stack, k_all[nb * BLOCK:(nb + 2) * BLOCK], (((1,), (1,)), ((), ())),
                            preferred_element_type=F32) + bias
        if nb == 0:
            s = jnp.where(key_col < missing_cols, MASKED, s)
        return s

    def finish(res, nb, s, v_heads):
        mx = jnp.max(s, axis=-1, keepdims=True)
        e = jnp.exp(s - mx)
        den = jnp.sum(e, axis=-1, keepdims=True)
        p = (e * (1.0 / den)).astype(BF16)
        p_cat = jnp.concatenate([p[hd * BLOCK:(hd + 1) * BLOCK] for hd in range(heads)], axis=1)
        v_stack = jnp.concatenate([vh[nb * BLOCK:(nb + 2) * BLOCK] for vh in v_heads], axis=0)
        o = jnp.dot(p_cat, v_stack, preferred_element_type=F32)
        lse = mx + jnp.log(den)
        lse_full = jnp.broadcast_to(lse[:BLOCK], (BLOCK, GROUP_W))
        for hd in range(1, heads):
            lse_full = jnp.where(lane_head == hd, lse[hd * BLOCK:(hd + 1) * BLOCK], lse_full)
        rows = slice(nb * BLOCK, (nb + 1) * BLOCK)
        lanes = slice(res * GROUP_W, (res + 1) * GROUP_W)
        o_ref[rows, lanes] = o.astype(o_ref.dtype)
        lse_ref[rows, lanes] = lse_full

    units = [(res, nb) for res in range(residues) for nb in range(blocks)]
    kv = {res: keys_values(res) for res in range(residues)}
    s_next = scores(*units[0], kv[units[0][0]][0])
    for i, (res, nb) in enumerate(units):
        s = s_next
        if i + 1 < len(units):
            s_next = scores(*units[i + 1], kv[units[i + 1][0]][0])
        finish(res, nb, s, kv[res][1])


def _dilated_group(q, k, v, bias_band, dil):
    batch, sub_seq, _ = q.shape
    blocks = min(DILATED_BLOCKS_PER_STEP, sub_seq // BLOCK)
    residues = min(DILATED_BLOCKS_PER_STEP // blocks, dil)
    tile = blocks * BLOCK
    width = residues * GROUP_W
    cur = pl.BlockSpec((None, tile, width), lambda b, r, n: (b, n, r))
    prev = pl.BlockSpec((None, BLOCK, width),
                        lambda b, r, n: (b, jnp.maximum(n * blocks - 1, 0), r))
    return pl.pallas_call(
        functools.partial(_dilated_body, blocks=blocks, residues=residues),
        grid=(batch, dil // residues, sub_seq // tile),
        in_specs=[cur, cur, cur, prev, prev, _resident(bias_band.shape)],
        out_specs=[cur, cur],
        out_shape=[jax.ShapeDtypeStruct(q.shape, BF16), jax.ShapeDtypeStruct(q.shape, F32)],
        compiler_params=_compiler_params(("parallel", "parallel", "parallel")),
        name=f"dilated_d{dil}",
    )(q, k, v, k, v, bias_band)


def _bias_band(rel_bias, group, dil, n_back):
    heads = slice(group * HEADS_PER_GROUP, (group + 1) * HEADS_PER_GROUP)
    buckets = _t5_bucket(np.arange(n_back + 1) * dil)
    bias = rel_bias[buckets][:, heads].T.astype(F32)
    period = 2 * BLOCK + 1
    j = BLOCK - np.arange(period)
    local = (j >= 0) & (j <= n_back)
    diag = jnp.where(local[None], bias[:, np.clip(j, 0, n_back)], MASKED)
    band = jnp.tile(diag, (1, BLOCK))[:, :BLOCK * 2 * BLOCK]
    return band.reshape(HEADS_PER_GROUP * BLOCK, 2 * BLOCK)


class _StickStep(NamedTuple):
    q: jax.Array
    k: jax.Array
    v: jax.Array
    mask: Optional[jax.Array]
    carry: Any
    after: Optional[int]


def _stick_steps(steps, upper):
    z = [lax.dot_general(st.q, st.k, (((1,), (1,)), ((), ())), preferred_element_type=F32)
         for st in steps]
    soft = [jnp.maximum(s, 0.0) + jnp.log(1.0 + jnp.exp(-jnp.abs(s))) for s in z]
    log_beta = [s - p for s, p in zip(z, soft)]
    soft = [p if st.mask is None else jnp.where(st.mask, p, 0.0) for p, st in zip(soft, steps)]
    tail = [jnp.dot(p.astype(BF16), upper, preferred_element_type=F32) for p in soft]
    sums = [jnp.sum(p, axis=-1, keepdims=True) for p in soft]
    weights = []
    for i, st in enumerate(steps):
        carry = st.carry if st.after is None else st.carry + sums[st.after]
        w = jnp.exp(log_beta[i] - (tail[i] + carry))
        weights.append(w if st.mask is None else jnp.where(st.mask, w, 0.0))
    outs = [jnp.dot(w.astype(BF16), st.v, preferred_element_type=F32)
            for w, st in zip(weights, steps)]
    return outs, sums


def _stick_body(q_ref, k_ref, v_ref, upper_ref, o_ref, acc_ref, carry_ref):
    tile = q_ref.shape[0]
    heads = LANES // HEAD_DIM
    qi = pl.program_id(2)
    q = q_ref[...]
    lane_head = lax.broadcasted_iota(jnp.int32, (tile, LANES), 1) // HEAD_DIM
    row = lax.broadcasted_iota(jnp.int32, (tile, tile), 0)
    col = lax.broadcasted_iota(jnp.int32, (tile, tile), 1)
    strictly_before = col < row
    upper = upper_ref[...]
    qh = [jnp.where(lane_head == hd, q, jnp.zeros_like(q)) for hd in range(heads)]

    def key_value(kt):
        start = pl.multiple_of(kt * tile, tile)
        return k_ref[pl.ds(start, tile), :], v_ref[pl.ds(start, tile), :]

    k_diag, v_diag = key_value(qi)
    k_prev, v_prev = key_value(jnp.maximum(qi - 1, 0))
    v_prev = jnp.where(qi > 0, v_prev, jnp.zeros_like(v_prev))
    steps = []
    for hd in range(heads):
        steps.append(_StickStep(qh[hd], k_diag, v_diag, strictly_before, 0.0, None))
        steps.append(_StickStep(qh[hd], k_prev, v_prev, None, 0.0, 2 * hd))
    outs, sums = _stick_steps(steps, upper)
    for hd in range(heads):
        acc_ref[hd] = outs[2 * hd] + outs[2 * hd + 1]
        carry_ref[hd] = sums[2 * hd] + sums[2 * hd + 1]

    def any_weight_left():
        return (jnp.min(carry_ref[...]) < F32_EXP_UNDERFLOW).astype(jnp.int32)

    def more_tiles(state):
        j, live = state
        return jnp.logical_and(j < qi, live > 0)

    def earlier(state):
        j, _ = state
        kblk, vblk = key_value(qi - 1 - j)
        outs, sums = _stick_steps(
            [_StickStep(qh[hd], kblk, vblk, None, carry_ref[hd], None) for hd in range(heads)],
            upper)
        for hd in range(heads):
            acc_ref[hd] += outs[hd]
            carry_ref[hd] += sums[hd]
        return j + 1, any_weight_left()

    lax.while_loop(more_tiles, earlier, (jnp.int32(1), any_weight_left()))
    out = acc_ref[0]
    for hd in range(1, heads):
        out = jnp.where(lane_head == hd, acc_ref[hd], out)
    o_ref[...] = out.astype(o_ref.dtype)


def _stick_breaking(q, k, v):
    batch, seq, width = q.shape
    tile = STICK_TILE
    j = np.arange(tile)
    upper = jnp.asarray((j[:, None] > j[None, :]), BF16)
    return pl.pallas_call(
        _stick_body,
        grid=(batch, width // LANES, seq // tile),
        in_specs=[pl.BlockSpec((None, tile, LANES), lambda b, p, i: (b, i, p)),
                  pl.BlockSpec((None, seq, LANES), lambda b, p, i: (b, 0, p)),
                  pl.BlockSpec((None, seq, LANES), lambda b, p, i: (b, 0, p)),
                  _resident((tile, tile))],
        out_specs=pl.BlockSpec((None, tile, LANES), lambda b, p, i: (b, i, p)),
        out_shape=jax.ShapeDtypeStruct(q.shape, BF16),
        scratch_shapes=[pltpu.VMEM((LANES // HEAD_DIM, tile, LANES), F32),
                        pltpu.VMEM((LANES // HEAD_DIM, tile, 1), F32)],
        compiler_params=_compiler_params(("parallel", "parallel", "parallel")),
        name="stick_breaking",
    )(q, k, v, upper)


def _post_mix_body(x_ref, gates_ref, yb_ref, *refs):
    group_refs = refs[:2 * N_GROUPS]
    wa_ref, wb_ref, wo_ref, g_ref, out_ref, slab_ref = refs[2 * N_GROUPS:]
    tile, d_model = x_ref.shape

    def natural_order(ref, dil):
        if dil == 1:
            return ref[...].astype(F32)
        rows = tile // dil
        for r in range(dil):
            for s in range(GROUP_W // LANES):
                lane0 = r * GROUP_W + s * LANES
                slab_ref[s, pl.ds(r, rows, stride=dil), :] = (
                    ref[:, lane0:lane0 + LANES].astype(F32))
        return jnp.concatenate([slab_ref[s] for s in range(GROUP_W // LANES)], axis=-1)

    def sigmoid(v):
        return 0.5 * jnp.tanh(0.5 * v) + 0.5

    branch_b = jnp.dot(yb_ref[...], wb_ref[...], preferred_element_type=F32)
    outs, lses = [], []
    for g, (_, dil) in enumerate(DILATED_GROUPS):
        outs.append(natural_order(group_refs[2 * g], dil))
        lses.append(natural_order(group_refs[2 * g + 1], dil))
    top = functools.reduce(jnp.maximum, lses)
    wts = [jnp.exp(l - top) for l in lses]
    ya = sum(w * o for w, o in zip(wts, outs)) / sum(wts)

    branch_a = jnp.dot(ya.astype(BF16), wa_ref[...], preferred_element_type=F32)
    gate_a = gates_ref[:, :d_model].astype(F32)
    gate_b = gates_ref[:, d_model:].astype(F32)
    merged = sigmoid(gate_a) * branch_a + sigmoid(gate_b) * branch_b
    mixed = jnp.dot(merged.astype(BF16), wo_ref[...], preferred_element_type=F32)
    out_ref[...] = x_ref[...] + _rms_norm(mixed, g_ref[...])


def _post_mix(x, gates, yb, groups, wa, wb, wo, g):
    batch, seq, d_model = x.shape
    tile = TOKEN_TILE
    row_spec = lambda width: pl.BlockSpec((None, tile, width), lambda b, i: (b, i, 0))
    in_specs = [row_spec(d_model), row_spec(gates.shape[-1]), row_spec(yb.shape[-1])]
    operands = [x, gates, yb]
    for (o, lse), (_, dil) in zip(groups, DILATED_GROUPS):
        for arr in (o, lse):
            in_specs.append(pl.BlockSpec((None, tile // dil, dil * GROUP_W), lambda b, i: (b, i, 0)))
            operands.append(arr)
    for arr in (wa, wb, wo, g):
        in_specs.append(_resident(arr.shape))
        operands.append(arr)
    return pl.pallas_call(
        _post_mix_body,
        grid=(batch, seq // tile),
        in_specs=in_specs,
        out_specs=row_spec(d_model),
        out_shape=jax.ShapeDtypeStruct(x.shape, F32),
        scratch_shapes=[pltpu.VMEM((GROUP_W // LANES, tile, LANES), F32)],
        compiler_params=_compiler_params(("parallel", "parallel")),
        name="post_mix",
    )(*operands)


def _ffn_body(x_ref, g_pre_ref, w_up_ref, cw_ref, cb_ref, w_down_ref, g_post_ref, out_ref,
              halo_ref, u_ref, act_ref, acc_ref, slab_ref):
    tile, d_model = x_ref.shape
    d_ff = w_down_ref.shape[0]
    seg = tile // SUBLANES
    n_slabs = d_model // LANES
    halo_rows = (CONV_WIDTH - 1) * SUBLANES

    @pl.when(pl.program_id(1) == 0)
    def _():
        halo_ref[...] = jnp.zeros_like(halo_ref)

    for c in range(n_slabs):
        for s in range(SUBLANES):
            slab_ref[c, pl.ds(s, seg, stride=SUBLANES), :] = (
                x_ref[s * seg:(s + 1) * seg, c * LANES:(c + 1) * LANES])
    x = jnp.concatenate([slab_ref[c] for c in range(n_slabs)], axis=-1)
    h = _rms_norm(x, g_pre_ref[...]).astype(BF16)
    sublane = lax.broadcasted_iota(jnp.int32, (SUBLANES, FF_CHUNK), 0)

    def up(c):
        return [jnp.dot(h, w_up_ref[:, col:col + FF_CHUNK], preferred_element_type=F32)
                for col in (c * FF_CHUNK, d_ff + c * FF_CHUNK)]

    def conv(u, col, slot):
        cols = slice(col, col + FF_CHUNK)
        last = u[tile - halo_rows:, :]
        for k in range(CONV_WIDTH - 1):
            group = slice(k * SUBLANES, (k + 1) * SUBLANES)
            mixed = jnp.where(sublane == SUBLANES - 1, halo_ref[group, cols], last[group, :])
            u_ref[slot, group, :] = pltpu.roll(mixed, 1, axis=0)
        u_ref[slot, halo_rows:, :] = u
        halo_ref[:, cols] = last
        y = cb_ref[:, cols]
        for t in range(CONV_WIDTH):
            start = t * SUBLANES
            y = y + cw_ref[t:t + 1, cols] * u_ref[slot, start:start + tile, :]
        return y

    n_chunks = d_ff // FF_CHUNK
    u_next = up(0)
    first = 0
    for c in range(n_chunks):
        u_gate, u_val = u_next
        if c + 1 < n_chunks:
            u_next = up(c + 1)
        gate = conv(u_gate, c * FF_CHUNK, 0)
        val = conv(u_val, d_ff + c * FF_CHUNK, 1)
        act_ref[:, c * FF_CHUNK:(c + 1) * FF_CHUNK] = (
            jax.nn.gelu(gate, approximate=True) * val).astype(BF16)
        if (c + 1) % DOWN_GROUP == 0 or c + 1 == n_chunks:
            rows = slice(first * FF_CHUNK, (c + 1) * FF_CHUNK)
            part = jnp.dot(act_ref[:, rows], w_down_ref[rows, :], preferred_element_type=F32)
            if first == 0:
                acc_ref[...] = part
            else:
                acc_ref[...] += part
            first = c + 1
    y = x + _rms_norm(acc_ref[...], g_post_ref[...])
    for c in range(n_slabs):
        slab_ref[c] = y[:, c * LANES:(c + 1) * LANES]
    for c in range(n_slabs):
        for s in range(SUBLANES):
            out_ref[s * seg:(s + 1) * seg, c * LANES:(c + 1) * LANES] = (
                slab_ref[c, pl.ds(s, seg, stride=SUBLANES), :])


def _ffn(x, g_pre, w_up, conv_w, conv_b, w_down, g_post):
    batch, seq, d_model = x.shape
    tile = TOKEN_TILE
    d_ff = w_down.shape[0]
    row_spec = pl.BlockSpec((None, tile, d_model), lambda b, i: (b, i, 0))
    consts = (g_pre, w_up, conv_w, conv_b, w_down, g_post)
    halo_rows = (CONV_WIDTH - 1) * SUBLANES
    return pl.pallas_call(
        _ffn_body,
        grid=(batch, seq // tile),
        in_specs=[row_spec] + [_resident(a.shape) for a in consts],
        out_specs=row_spec,
        out_shape=jax.ShapeDtypeStruct(x.shape, F32),
        scratch_shapes=[pltpu.VMEM((halo_rows, 2 * d_ff), F32),
                        pltpu.VMEM((2, halo_rows + tile, FF_CHUNK), F32),
                        pltpu.VMEM((tile, d_ff), BF16),
                        pltpu.VMEM((tile, d_model), F32),
                        pltpu.VMEM((d_model // LANES, tile, LANES), F32)],
        compiler_params=_compiler_params(("arbitrary", "arbitrary")),
        name="ffn",
    )(x, *consts)


def kernel(x, g_pre_mix, w_in, rel_bias, w_branch_a, w_branch_b, w_out, g_post_mix, g_pre_ffn,
           w_up, conv_w, conv_b, w_down, g_post_ffn):
    depth = w_in.shape[0]
    for l in range(depth):
        proj = _in_proj(x, g_pre_mix[l][None], w_in[l].astype(BF16))
        qb, kb, vb, gates = proj[3 * N_GROUPS:]
        groups = []
        for g, (window, dil) in enumerate(DILATED_GROUPS):
            q, k, v = proj[3 * g:3 * g + 3]
            band = _bias_band(rel_bias, g, dil, window // dil)
            groups.append(_dilated_group(q, k, v, band, dil))
        yb = _stick_breaking(qb, kb, vb)
        x = _post_mix(x, gates, yb, groups, w_branch_a[l].astype(BF16), w_branch_b[l].astype(BF16),
                      w_out[l].astype(BF16), g_post_mix[l][None])
        x = _ffn(x, g_pre_ffn[l][None], w_up[l].astype(BF16), conv_w[l], conv_b[l][None],
                 w_down[l].astype(BF16), g_post_ffn[l][None])
    return x
```

```python
import functools
import math
from typing import Any, NamedTuple, Optional

import numpy as np
import jax
import jax.numpy as jnp
from jax import lax
from jax.experimental import pallas as pl
from jax.experimental.pallas import tpu as pltpu

F32 = jnp.float32
BF16 = jnp.bfloat16

HEAD_DIM = 64
DILATED_GROUPS = ((128, 1), (512, 4), (2048, 16))
HEADS_PER_GROUP = 4
N_GROUPS = len(DILATED_GROUPS)
N_HEADS_A = HEADS_PER_GROUP * N_GROUPS
N_HEADS_B = 8
GROUP_W = HEADS_PER_GROUP * HEAD_DIM
A_QKV = N_HEADS_A * HEAD_DIM
B_QKV = N_HEADS_B * HEAD_DIM
BLOCK = 128
NUM_BUCKETS = 32
MAX_DISTANCE = 2048
CONV_WIDTH = 3
EPS = 1e-6
QK_SCALE = 1.0 / math.sqrt(HEAD_DIM)
MASKED = -1e30
F32_EXP_UNDERFLOW = 104.0

LANES = 128
SUBLANES = 8
VMEM_LIMIT_BYTES = 56 * 1024 * 1024

TOKEN_TILE = 512
DILATED_BLOCKS_PER_STEP = 8
STICK_Q_TILES = 2
STICK_TILE = 256
FF_CHUNK = 256
DOWN_GROUP = 3


def _t5_bucket(dist):
    max_exact = NUM_BUCKETS // 2
    n = np.asarray(dist, dtype=np.float32)
    large = max_exact + (np.log(np.maximum(n, 1.0) / max_exact)
                         / np.log(MAX_DISTANCE / max_exact)
                         * (NUM_BUCKETS - max_exact)).astype(np.int32)
    large = np.minimum(large, NUM_BUCKETS - 1)
    return np.where(n < max_exact, n.astype(np.int32), large).astype(np.int32)


def _rms_norm(x, g):
    return x * lax.rsqrt(jnp.mean(x * x, axis=-1, keepdims=True) + EPS) * g


def _compiler_params(semantics):
    return pltpu.CompilerParams(dimension_semantics=semantics, vmem_limit_bytes=VMEM_LIMIT_BYTES)


def _resident(shape):
    return pl.BlockSpec(shape, lambda *_: (0,) * len(shape), pipeline_mode=pl.Buffered(1))


def _in_proj_body(x_ref, g_ref, w_ref, *refs):
    a_refs = refs[:3 * N_GROUPS]
    qb_ref, kb_ref, vb_ref, gates_ref, slab_ref = refs[3 * N_GROUPS:]
    tile = x_ref.shape[0]
    h = _rms_norm(x_ref[...], g_ref[...]).astype(BF16)

    def proj(col, width):
        return jnp.dot(h, w_ref[:, col:col + width], preferred_element_type=F32)

    for t in range(3):
        scale = QK_SCALE if t == 0 else 1.0
        for g, (_, dil) in enumerate(DILATED_GROUPS):
            acc = proj(t * A_QKV + g * GROUP_W, GROUP_W) * scale
            out_ref = a_refs[3 * g + t]
            if dil == 1:
                out_ref[...] = acc.astype(BF16)
                continue
            rows = tile // dil
            for s in range(GROUP_W // LANES):
                slab_ref[s] = acc[:, s * LANES:(s + 1) * LANES]
            for r in range(dil):
                for s in range(GROUP_W // LANES):
                    lane0 = r * GROUP_W + s * LANES
                    out_ref[:, lane0:lane0 + LANES] = (
                        slab_ref[s, pl.ds(r, rows, stride=dil), :].astype(BF16))

    base = 3 * A_QKV
    qb_ref[...] = (proj(base, B_QKV) * QK_SCALE).astype(BF16)
    kb_ref[...] = proj(base + B_QKV, B_QKV).astype(BF16)
    vb_ref[...] = proj(base + 2 * B_QKV, B_QKV).astype(BF16)
    base += 3 * B_QKV
    gate_w = gates_ref.shape[1]
    for c in range(0, gate_w, 512):
        gates_ref[:, c:c + 512] = proj(base + c, 512).astype(BF16)


def _in_proj(x, g, w):
    batch, seq, d_model = x.shape
    tile = TOKEN_TILE
    n_in = w.shape[1]
    gate_w = n_in - 3 * A_QKV - 3 * B_QKV
    out_shape, out_specs = [], []
    for _, dil in DILATED_GROUPS:
        for _ in range(3):
            out_shape.append(jax.ShapeDtypeStruct((batch, seq // dil, dil * GROUP_W), BF16))
            out_specs.append(pl.BlockSpec((None, tile // dil, dil * GROUP_W), lambda b, i: (b, i, 0)))
    for _ in range(3):
        out_shape.append(jax.ShapeDtypeStruct((batch, seq, B_QKV), BF16))
        out_specs.append(pl.BlockSpec((None, tile, B_QKV), lambda b, i: (b, i, 0)))
    out_shape.append(jax.ShapeDtypeStruct((batch, seq, gate_w), BF16))
    out_specs.append(pl.BlockSpec((None, tile, gate_w), lambda b, i: (b, i, 0)))
    return pl.pallas_call(
        _in_proj_body,
        grid=(batch, seq // tile),
        in_specs=[pl.BlockSpec((None, tile, d_model), lambda b, i: (b, i, 0)),
                  _resident((1, d_model)),
                  _resident((d_model, n_in))],
        out_specs=out_specs,
        out_shape=out_shape,
        scratch_shapes=[pltpu.VMEM((GROUP_W // LANES, tile, LANES), F32)],
        compiler_params=_compiler_params(("parallel", "parallel")),
        name="in_proj",
    )(x, g, w)


def _dilated_body(q_ref, k_ref, v_ref, kp_ref, vp_ref, bias_ref, o_ref, lse_ref, *,
                  blocks, residues):
    heads = HEADS_PER_GROUP
    missing_cols = jnp.where(pl.program_id(2) == 0, BLOCK, 0)
    lane_head = lax.broadcasted_iota(jnp.int32, (BLOCK, GROUP_W), 1) // HEAD_DIM
    key_col = lax.broadcasted_iota(jnp.int32, (heads * BLOCK, 2 * BLOCK), 1)
    bias = bias_ref[...]

    def keys_values(res):
        lanes = slice(res * GROUP_W, (res + 1) * GROUP_W)
        k_all = jnp.concatenate([kp_ref[:, lanes], k_ref[:, lanes]], axis=0)
        v_all = jnp.concatenate([vp_ref[:, lanes], v_ref[:, lanes]], axis=0)
        v_lane_head = lax.broadcasted_iota(jnp.int32, v_all.shape, 1) // HEAD_DIM
        v_heads = [jnp.where(v_lane_head == hd, v_all, jnp.zeros_like(v_all))
                   for hd in range(heads)]
        return k_all, v_heads

    def scores(res, nb, k_all):
        q = q_ref[nb * BLOCK:(nb + 1) * BLOCK, res * GROUP_W:(res + 1) * GROUP_W]
        q_stack = jnp.concatenate(
            [jnp.where(lane_head == hd, q, jnp.zeros_like(q)) for hd in range(heads)], axis=0)
        s = lax.dot_general(q_stack, k_all[nb * BLOCK:(nb + 2) * BLOCK], (((1,), (1,)), ((), ())),
                            preferred_element_type=F32) + bias
        if nb == 0:
            s = jnp.where(key_col < missing_cols, MASKED, s)
        return s

    def finish(res, nb, s, v_heads):
        mx = jnp.max(s, axis=-1, keepdims=True)
        e = jnp.exp(s - mx)
        den = jnp.sum(e, axis=-1, keepdims=True)
        p = (e * (1.0 / den)).astype(BF16)
        p_cat = jnp.concatenate([p[hd * BLOCK:(hd + 1) * BLOCK] for hd in range(heads)], axis=1)
        v_stack = jnp.concatenate([vh[nb * BLOCK:(nb + 2) * BLOCK] for vh in v_heads], axis=0)
        o = jnp.dot(p_cat, v_stack, preferred_element_type=F32)
        lse = mx + jnp.log(den)
        lse_full = jnp.broadcast_to(lse[:BLOCK], (BLOCK, GROUP_W))
        for hd in range(1, heads):
            lse_full = jnp.where(lane_head == hd, lse[hd * BLOCK:(hd + 1) * BLOCK], lse_full)
        rows = slice(nb * BLOCK, (nb + 1) * BLOCK)
        lanes = slice(res * GROUP_W, (res + 1) * GROUP_W)
        o_ref[rows, lanes] = o.astype(o_ref.dtype)
        lse_ref[rows, lanes] = lse_full

    units = [(res, nb) for res in range(residues) for nb in range(blocks)]
    kv = {res: keys_values(res) for res in range(residues)}
    s_next = scores(*units[0], kv[units[0][0]][0])
    for i, (res, nb) in enumerate(units):
        s = s_next
        if i + 1 < len(units):
            s_next = scores(*units[i + 1], kv[units[i + 1][0]][0])
        finish(res, nb, s, kv[res][1])


def _dilated_group(q, k, v, bias_band, dil):
    batch, sub_seq, _ = q.shape
    blocks = min(DILATED_BLOCKS_PER_STEP, sub_seq // BLOCK)
    residues = min(DILATED_BLOCKS_PER_STEP // blocks, dil)
    tile = blocks * BLOCK
    width = residues * GROUP_W
    cur = pl.BlockSpec((None, tile, width), lambda b, r, n: (b, n, r))
    prev = pl.BlockSpec((None, BLOCK, width),
                        lambda b, r, n: (b, jnp.maximum(n * blocks - 1, 0), r))
    return pl.pallas_call(
        functools.partial(_dilated_body, blocks=blocks, residues=residues),
        grid=(batch, dil // residues, sub_seq // tile),
        in_specs=[cur, cur, cur, prev, prev, _resident(bias_band.shape)],
        out_specs=[cur, cur],
        out_shape=[jax.ShapeDtypeStruct(q.shape, BF16), jax.ShapeDtypeStruct(q.shape, F32)],
        compiler_params=_compiler_params(("parallel", "parallel", "parallel")),
        name=f"dilated_d{dil}",
    )(q, k, v, k, v, bias_band)


def _bias_band(rel_bias, group, dil, n_back):
    heads = slice(group * HEADS_PER_GROUP, (group + 1) * HEADS_PER_GROUP)
    buckets = _t5_bucket(np.arange(n_back + 1) * dil)
    bias = rel_bias[buckets][:, heads].T.astype(F32)
    period = 2 * BLOCK + 1
    j = BLOCK - np.arange(period)
    local = (j >= 0) & (j <= n_back)
    diag = jnp.where(local[None], bias[:, np.clip(j, 0, n_back)], MASKED)
    band = jnp.tile(diag, (1, BLOCK))[:, :BLOCK * 2 * BLOCK]
    return band.reshape(HEADS_PER_GROUP * BLOCK, 2 * BLOCK)


class _StickStep(NamedTuple):
    q: jax.Array
    k: jax.Array
    v: jax.Array
    mask: Optional[jax.Array]
    carry: Any
    after: Optional[int]


def _stick_steps(steps, upper):
    z = [lax.dot_general(st.q, st.k, (((1,), (1,)), ((), ())), preferred_element_type=F32)
         for st in steps]
    soft = [jnp.maximum(s, 0.0) + jnp.log(1.0 + jnp.exp(-jnp.abs(s))) for s in z]
    log_beta = [s - p for s, p in zip(z, soft)]
    soft = [p if st.mask is None else jnp.where(st.mask, p, 0.0) for p, st in zip(soft, steps)]
    tail = [jnp.dot(p.astype(BF16), upper, preferred_element_type=F32) for p in soft]
    sums = [jnp.sum(p, axis=-1, keepdims=True) for p in soft]
    weights = []
    for i, st in enumerate(steps):
        carry = st.carry if st.after is None else st.carry + sums[st.after]
        w = jnp.exp(log_beta[i] - (tail[i] + carry))
        weights.append(w if st.mask is None else jnp.where(st.mask, w, 0.0))
    outs = [jnp.dot(w.astype(BF16), st.v, preferred_element_type=F32)
            for w, st in zip(weights, steps)]
    return outs, sums


def _stick_body(q_ref, k_ref, v_ref, upper_ref, o_ref, acc_ref, carry_ref):
    tile = upper_ref.shape[0]
    q_tiles = q_ref.shape[0] // tile
    heads = LANES // HEAD_DIM
    first_tile = pl.program_id(2) * q_tiles
    lane_head = lax.broadcasted_iota(jnp.int32, (tile, LANES), 1) // HEAD_DIM
    row = lax.broadcasted_iota(jnp.int32, (tile, tile), 0)
    col = lax.broadcasted_iota(jnp.int32, (tile, tile), 1)
    strictly_before = col < row
    upper = upper_ref[...]
    chains = []
    for t in range(q_tiles):
        q = q_ref[t * tile:(t + 1) * tile, :]
        for hd in range(heads):
            chains.append((t, hd, jnp.where(lane_head == hd, q, jnp.zeros_like(q))))

    def key_value(kt):
        start = pl.multiple_of(jnp.maximum(kt, 0) * tile, tile)
        v = v_ref[pl.ds(start, tile), :]
        return k_ref[pl.ds(start, tile), :], jnp.where(kt >= 0, v, jnp.zeros_like(v))

    diag = [key_value(first_tile + t) for t in range(q_tiles)]
    prev = [key_value(first_tile + t - 1) for t in range(q_tiles)]
    steps = []
    for t, hd, qh in chains:
        steps.append(_StickStep(qh, *diag[t], strictly_before, 0.0, None))
        steps.append(_StickStep(qh, *prev[t], None, 0.0, len(steps) - 1))
    outs, sums = _stick_steps(steps, upper)
    for c in range(len(chains)):
        acc_ref[c] = outs[2 * c] + outs[2 * c + 1]
        carry_ref[c] = sums[2 * c] + sums[2 * c + 1]

    def any_weight_left():
        return (jnp.min(carry_ref[...]) < F32_EXP_UNDERFLOW).astype(jnp.int32)

    def more_tiles(state):
        j, live = state
        return jnp.logical_and(j < first_tile + q_tiles - 1, live > 0)

    def earlier(state):
        j, _ = state
        tiles = [key_value(first_tile + t - 1 - j) for t in range(q_tiles)]
        outs, sums = _stick_steps(
            [_StickStep(qh, *tiles[t], None, carry_ref[c], None)
             for c, (t, _, qh) in enumerate(chains)], upper)
        for c in range(len(chains)):
            acc_ref[c] += outs[c]
            carry_ref[c] += sums[c]
        return j + 1, any_weight_left()

    lax.while_loop(more_tiles, earlier, (jnp.int32(1), any_weight_left()))
    for t in range(q_tiles):
        out = acc_ref[t * heads]
        for hd in range(1, heads):
            out = jnp.where(lane_head == hd, acc_ref[t * heads + hd], out)
        o_ref[t * tile:(t + 1) * tile, :] = out.astype(o_ref.dtype)


def _stick_breaking(q, k, v):
    batch, seq, width = q.shape
    tile = STICK_TILE
    rows = STICK_Q_TILES * tile
    chains = STICK_Q_TILES * (LANES // HEAD_DIM)
    j = np.arange(tile)
    upper = jnp.asarray((j[:, None] > j[None, :]), BF16)
    return pl.pallas_call(
        _stick_body,
        grid=(batch, width // LANES, seq // rows),
        in_specs=[pl.BlockSpec((None, rows, LANES), lambda b, p, i: (b, i, p)),
                  pl.BlockSpec((None, seq, LANES), lambda b, p, i: (b, 0, p)),
                  pl.BlockSpec((None, seq, LANES), lambda b, p, i: (b, 0, p)),
                  _resident((tile, tile))],
        out_specs=pl.BlockSpec((None, rows, LANES), lambda b, p, i: (b, i, p)),
        out_shape=jax.ShapeDtypeStruct(q.shape, BF16),
        scratch_shapes=[pltpu.VMEM((chains, tile, LANES), F32),
                        pltpu.VMEM((chains, tile, 1), F32)],
        compiler_params=_compiler_params(("parallel", "parallel", "parallel")),
        name="stick_breaking",
    )(q, k, v, upper)


def _post_mix_body(x_ref, gates_ref, yb_ref, *refs):
    group_refs = refs[:2 * N_GROUPS]
    wa_ref, wb_ref, wo_ref, g_ref, out_ref, slab_ref = refs[2 * N_GROUPS:]
    tile, d_model = x_ref.shape

    def natural_order(ref, dil):
        if dil == 1:
            return ref[...].astype(F32)
        rows = tile // dil
        for r in range(dil):
            for s in range(GROUP_W // LANES):
                lane0 = r * GROUP_W + s * LANES
                slab_ref[s, pl.ds(r, rows, stride=dil), :] = (
                    ref[:, lane0:lane0 + LANES].astype(F32))
        return jnp.concatenate([slab_ref[s] for s in range(GROUP_W // LANES)], axis=-1)

    def sigmoid(v):
        return 0.5 * jnp.tanh(0.5 * v) + 0.5

    branch_b = jnp.dot(yb_ref[...], wb_ref[...], preferred_element_type=F32)
    outs, lses = [], []
    for g, (_, dil) in enumerate(DILATED_GROUPS):
        outs.append(natural_order(group_refs[2 * g], dil))
        lses.append(natural_order(group_refs[2 * g + 1], dil))
    top = functools.reduce(jnp.maximum, lses)
    wts = [jnp.exp(l - top) for l in lses]
    ya = sum(w * o for w, o in zip(wts, outs)) / sum(wts)

    branch_a = jnp.dot(ya.astype(BF16), wa_ref[...], preferred_element_type=F32)
    gate_a = gates_ref[:, :d_model].astype(F32)
    gate_b = gates_ref[:, d_model:].astype(F32)
    merged = sigmoid(gate_a) * branch_a + sigmoid(gate_b) * branch_b
    mixed = jnp.dot(merged.astype(BF16), wo_ref[...], preferred_element_type=F32)
    out_ref[...] = x_ref[...] + _rms_norm(mixed, g_ref[...])


def _post_mix(x, gates, yb, groups, wa, wb, wo, g):
    batch, seq, d_model = x.shape
    tile = TOKEN_TILE
    row_spec = lambda width: pl.BlockSpec((None, tile, width), lambda b, i: (b, i, 0))
    in_specs = [row_spec(d_model), row_spec(gates.shape[-1]), row_spec(yb.shape[-1])]
    operands = [x, gates, yb]
    for (o, lse), (_, dil) in zip(groups, DILATED_GROUPS):
        for arr in (o, lse):
            in_specs.append(pl.BlockSpec((None, tile // dil, dil * GROUP_W), lambda b, i: (b, i, 0)))
            operands.append(arr)
    for arr in (wa, wb, wo, g):
        in_specs.append(_resident(arr.shape))
        operands.append(arr)
    return pl.pallas_call(
        _post_mix_body,
        grid=(batch, seq // tile),
        in_specs=in_specs,
        out_specs=row_spec(d_model),
        out_shape=jax.ShapeDtypeStruct(x.shape, F32),
        scratch_shapes=[pltpu.VMEM((GROUP_W // LANES, tile, LANES), F32)],
        compiler_params=_compiler_params(("parallel", "parallel")),
        name="post_mix",
    )(*operands)


def _ffn_body(x_ref, g_pre_ref, w_up_ref, cw_ref, cb_ref, w_down_ref, g_post_ref, out_ref,
              halo_ref, u_ref, act_ref, acc_ref, slab_ref):
    tile, d_model = x_ref.shape
    d_ff = w_down_ref.shape[0]
    seg = tile // SUBLANES
    n_slabs = d_model // LANES
    halo_rows = (CONV_WIDTH - 1) * SUBLANES

    @pl.when(pl.program_id(1) == 0)
    def _():
        halo_ref[...] = jnp.zeros_like(halo_ref)

    for c in range(n_slabs):
        for s in range(SUBLANES):
            slab_ref[c, pl.ds(s, seg, stride=SUBLANES), :] = (
                x_ref[s * seg:(s + 1) * seg, c * LANES:(c + 1) * LANES])
    x = jnp.concatenate([slab_ref[c] for c in range(n_slabs)], axis=-1)
    h = _rms_norm(x, g_pre_ref[...]).astype(BF16)
    sublane = lax.broadcasted_iota(jnp.int32, (SUBLANES, FF_CHUNK), 0)

    def up(c):
        return [jnp.dot(h, w_up_ref[:, col:col + FF_CHUNK], preferred_element_type=F32)
                for col in (c * FF_CHUNK, d_ff + c * FF_CHUNK)]

    def conv(u, col, slot):
        cols = slice(col, col + FF_CHUNK)
        last = u[tile - halo_rows:, :]
        for k in range(CONV_WIDTH - 1):
            group = slice(k * SUBLANES, (k + 1) * SUBLANES)
            mixed = jnp.where(sublane == SUBLANES - 1, halo_ref[group, cols], last[group, :])
            u_ref[slot, group, :] = pltpu.roll(mixed, 1, axis=0)
        u_ref[slot, halo_rows:, :] = u
        halo_ref[:, cols] = last
        y = cb_ref[:, cols]
        for t in range(CONV_WIDTH):
            start = t * SUBLANES
            y = y + cw_ref[t:t + 1, cols] * u_ref[slot, start:start + tile, :]
        return y

    n_chunks = d_ff // FF_CHUNK
    u_next = up(0)
    first = 0
    for c in range(n_chunks):
        u_gate, u_val = u_next
        if c + 1 < n_chunks:
            u_next = up(c + 1)
        gate = conv(u_gate, c * FF_CHUNK, 0)
        val = conv(u_val, d_ff + c * FF_CHUNK, 1)
        act_ref[:, c * FF_CHUNK:(c + 1) * FF_CHUNK] = (
            jax.nn.gelu(gate, approximate=True) * val).astype(BF16)
        if (c + 1) % DOWN_GROUP == 0 or c + 1 == n_chunks:
            rows = slice(first * FF_CHUNK, (c + 1) * FF_CHUNK)
            part = jnp.dot(act_ref[:, rows], w_down_ref[rows, :], preferred_element_type=F32)
            if first == 0:
                acc_ref[...] = part
            else:
                acc_ref[...] += part
            first = c + 1
    y = x + _rms_norm(acc_ref[...], g_post_ref[...])
    for c in range(n_slabs):
        slab_ref[c] = y[:, c * LANES:(c + 1) * LANES]
    for c in range(n_slabs):
        for s in range(SUBLANES):
            out_ref[s * seg:(s + 1) * seg, c * LANES:(c + 1) * LANES] = (
                slab_ref[c, pl.ds(s, seg, stride=SUBLANES), :])


def _ffn(x, g_pre, w_up, conv_w, conv_b, w_down, g_post):
    batch, seq, d_model = x.shape
    tile = TOKEN_TILE
    d_ff = w_down.shape[0]
    row_spec = pl.BlockSpec((None, tile, d_model), lambda b, i: (b, i, 0))
    consts = (g_pre, w_up, conv_w, conv_b, w_down, g_post)
    halo_rows = (CONV_WIDTH - 1) * SUBLANES
    return pl.pallas_call(
        _ffn_body,
        grid=(batch, seq // tile),
        in_specs=[row_spec] + [_resident(a.shape) for a in consts],
        out_specs=row_spec,
        out_shape=jax.ShapeDtypeStruct(x.shape, F32),
        scratch_shapes=[pltpu.VMEM((halo_rows, 2 * d_ff), F32),
                        pltpu.VMEM((2, halo_rows + tile, FF_CHUNK), F32),
                        pltpu.VMEM((tile, d_ff), BF16),
                        pltpu.VMEM((tile, d_model), F32),
                        pltpu.VMEM((d_model // LANES, tile, LANES), F32)],
        compiler_params=_compiler_params(("arbitrary", "arbitrary")),
        name="ffn",
    )(x, *consts)


def kernel(x, g_pre_mix, w_in, rel_bias, w_branch_a, w_branch_b, w_out, g_post_mix, g_pre_ffn,
           w_up, conv_w, conv_b, w_down, g_post_ffn):
    depth = w_in.shape[0]
    for l in range(depth):
        proj = _in_proj(x, g_pre_mix[l][None], w_in[l].astype(BF16))
        qb, kb, vb, gates = proj[3 * N_GROUPS:]
        groups = []
        for g, (window, dil) in enumerate(DILATED_GROUPS):
            q, k, v = proj[3 * g:3 * g + 3]
            band = _bias_band(rel_bias, g, dil, window // dil)
            groups.append(_dilated_group(q, k, v, band, dil))
        yb = _stick_breaking(qb, kb, vb)
        x = _post_mix(x, gates, yb, groups, w_branch_a[l].astype(BF16), w_branch_b[l].astype(BF16),
                      w_out[l].astype(BF16), g_post_mix[l][None])
        x = _ffn(x, g_pre_ffn[l][None], w_up[l].astype(BF16), conv_w[l], conv_b[l][None],
                 w_down[l].astype(BF16), g_post_ffn[l][None])
    return x
```

```python
import functools
import math
from typing import Any, NamedTuple, Optional

import numpy as np
import jax
import jax.numpy as jnp
from jax import lax
from jax.experimental import pallas as pl
from jax.experimental.pallas import tpu as pltpu

F32 = jnp.float32
BF16 = jnp.bfloat16

HEAD_DIM = 64
DILATED_GROUPS = ((128, 1), (512, 4), (2048, 16))
HEADS_PER_GROUP = 4
N_GROUPS = len(DILATED_GROUPS)
N_HEADS_A = HEADS_PER_GROUP * N_GROUPS
N_HEADS_B = 8
GROUP_W = HEADS_PER_GROUP * HEAD_DIM
A_QKV = N_HEADS_A * HEAD_DIM
B_QKV = N_HEADS_B * HEAD_DIM
BLOCK = 128
NUM_BUCKETS = 32
MAX_DISTANCE = 2048
CONV_WIDTH = 3
EPS = 1e-6
QK_SCALE = 1.0 / math.sqrt(HEAD_DIM)
MASKED = -1e30
F32_EXP_UNDERFLOW = 104.0

LANES = 128
SUBLANES = 8
VMEM_LIMIT_BYTES = 56 * 1024 * 1024

TOKEN_TILE = 512
IN_PROJ_TILE = 1024
DILATED_BLOCKS_PER_STEP = 8
STICK_Q_TILES = 4
STICK_TILE = 256
FF_CHUNK = 256
DOWN_GROUP = 3


def _t5_bucket(dist):
    max_exact = NUM_BUCKETS // 2
    n = np.asarray(dist, dtype=np.float32)
    large = max_exact + (np.log(np.maximum(n, 1.0) / max_exact)
                         / np.log(MAX_DISTANCE / max_exact)
                         * (NUM_BUCKETS - max_exact)).astype(np.int32)
    large = np.minimum(large, NUM_BUCKETS - 1)
    return np.where(n < max_exact, n.astype(np.int32), large).astype(np.int32)


def _rms_norm(x, g):
    return x * lax.rsqrt(jnp.mean(x * x, axis=-1, keepdims=True) + EPS) * g


def _compiler_params(semantics):
    return pltpu.CompilerParams(dimension_semantics=semantics, vmem_limit_bytes=VMEM_LIMIT_BYTES)


def _resident(shape):
    return pl.BlockSpec(shape, lambda *_: (0,) * len(shape), pipeline_mode=pl.Buffered(1))


def _in_proj_body(x_ref, g_ref, w_ref, *refs):
    a_refs = refs[:3 * N_GROUPS]
    qb_ref, kb_ref, vb_ref, gates_ref, slab_ref = refs[3 * N_GROUPS:]
    tile = x_ref.shape[0]
    h = _rms_norm(x_ref[...], g_ref[...]).astype(BF16)

    def proj(col, width):
        return jnp.dot(h, w_ref[:, col:col + width], preferred_element_type=F32)

    for t in range(3):
        scale = QK_SCALE if t == 0 else 1.0
        for g, (_, dil) in enumerate(DILATED_GROUPS):
            acc = proj(t * A_QKV + g * GROUP_W, GROUP_W) * scale
            out_ref = a_refs[3 * g + t]
            if dil == 1:
                out_ref[...] = acc.astype(BF16)
                continue
            rows = tile // dil
            for s in range(GROUP_W // LANES):
                slab_ref[s] = acc[:, s * LANES:(s + 1) * LANES]
            for r in range(dil):
                for s in range(GROUP_W // LANES):
                    lane0 = r * GROUP_W + s * LANES
                    out_ref[:, lane0:lane0 + LANES] = (
                        slab_ref[s, pl.ds(r, rows, stride=dil), :].astype(BF16))

    base = 3 * A_QKV
    qb_ref[...] = (proj(base, B_QKV) * QK_SCALE).astype(BF16)
    kb_ref[...] = proj(base + B_QKV, B_QKV).astype(BF16)
    vb_ref[...] = proj(base + 2 * B_QKV, B_QKV).astype(BF16)
    base += 3 * B_QKV
    gate_w = gates_ref.shape[1]
    for c in range(0, gate_w, 512):
        gates_ref[:, c:c + 512] = proj(base + c, 512).astype(BF16)


def _in_proj(x, g, w):
    batch, seq, d_model = x.shape
    tile = IN_PROJ_TILE
    n_in = w.shape[1]
    gate_w = n_in - 3 * A_QKV - 3 * B_QKV
    out_shape, out_specs = [], []
    for _, dil in DILATED_GROUPS:
        for _ in range(3):
            out_shape.append(jax.ShapeDtypeStruct((batch, seq // dil, dil * GROUP_W), BF16))
            out_specs.append(pl.BlockSpec((None, tile // dil, dil * GROUP_W), lambda b, i: (b, i, 0)))
    for _ in range(3):
        out_shape.append(jax.ShapeDtypeStruct((batch, seq, B_QKV), BF16))
        out_specs.append(pl.BlockSpec((None, tile, B_QKV), lambda b, i: (b, i, 0)))
    out_shape.append(jax.ShapeDtypeStruct((batch, seq, gate_w), BF16))
    out_specs.append(pl.BlockSpec((None, tile, gate_w), lambda b, i: (b, i, 0)))
    return pl.pallas_call(
        _in_proj_body,
        grid=(batch, seq // tile),
        in_specs=[pl.BlockSpec((None, tile, d_model), lambda b, i: (b, i, 0)),
                  _resident((1, d_model)),
                  _resident((d_model, n_in))],
        out_specs=out_specs,
        out_shape=out_shape,
        scratch_shapes=[pltpu.VMEM((GROUP_W // LANES, tile, LANES), F32)],
        compiler_params=_compiler_params(("parallel", "parallel")),
        name="in_proj",
    )(x, g, w)


def _dilated_body(q_ref, k_ref, v_ref, kp_ref, vp_ref, bias_ref, o_ref, lse_ref, *,
                  blocks, residues):
    heads = HEADS_PER_GROUP
    missing_cols = jnp.where(pl.program_id(2) == 0, BLOCK, 0)
    lane_head = lax.broadcasted_iota(jnp.int32, (BLOCK, GROUP_W), 1) // HEAD_DIM
    key_col = lax.broadcasted_iota(jnp.int32, (heads * BLOCK, 2 * BLOCK), 1)
    bias = bias_ref[...]

    def keys_values(res):
        lanes = slice(res * GROUP_W, (res + 1) * GROUP_W)
        k_all = jnp.concatenate([kp_ref[:, lanes], k_ref[:, lanes]], axis=0)
        v_all = jnp.concatenate([vp_ref[:, lanes], v_ref[:, lanes]], axis=0)
        v_lane_head = lax.broadcasted_iota(jnp.int32, v_all.shape, 1) // HEAD_DIM
        v_heads = [jnp.where(v_lane_head == hd, v_all, jnp.zeros_like(v_all))
                   for hd in range(heads)]
        return k_all, v_heads

    def scores(res, nb, k_all):
        q = q_ref[nb * BLOCK:(nb + 1) * BLOCK, res * GROUP_W:(res + 1) * GROUP_W]
        q_stack = jnp.concatenate(
            [jnp.where(lane_head == hd, q, jnp.zeros_like(q)) for hd in range(heads)], axis=0)
        s = lax.dot_general(q_stack, k_all[nb * BLOCK:(nb + 2) * BLOCK], (((1,), (1,)), ((), ())),
                            preferred_element_type=F32) + bias
        if nb == 0:
            s = jnp.where(key_col < missing_cols, MASKED, s)
        return s

    def finish(res, nb, s, v_heads):
        mx = jnp.max(s, axis=-1, keepdims=True)
        e = jnp.exp(s - mx)
        den = jnp.sum(e, axis=-1, keepdims=True)
        p = (e * (1.0 / den)).astype(BF16)
        p_cat = jnp.concatenate([p[hd * BLOCK:(hd + 1) * BLOCK] for hd in range(heads)], axis=1)
        v_stack = jnp.concatenate([vh[nb * BLOCK:(nb + 2) * BLOCK] for vh in v_heads], axis=0)
        o = jnp.dot(p_cat, v_stack, preferred_element_type=F32)
        lse = mx + jnp.log(den)
        lse_full = jnp.broadcast_to(lse[:BLOCK], (BLOCK, GROUP_W))
        for hd in range(1, heads):
            lse_full = jnp.where(lane_head == hd, lse[hd * BLOCK:(hd + 1) * BLOCK], lse_full)
        rows = slice(nb * BLOCK, (nb + 1) * BLOCK)
        lanes = slice(res * GROUP_W, (res + 1) * GROUP_W)
        o_ref[rows, lanes] = o.astype(o_ref.dtype)
        lse_ref[rows, lanes] = lse_full

    units = [(res, nb) for res in range(residues) for nb in range(blocks)]
    kv = {res: keys_values(res) for res in range(residues)}
    s_next = scores(*units[0], kv[units[0][0]][0])
    for i, (res, nb) in enumerate(units):
        s = s_next
        if i + 1 < len(units):
            s_next = scores(*units[i + 1], kv[units[i + 1][0]][0])
        finish(res, nb, s, kv[res][1])


def _dilated_group(q, k, v, bias_band, dil):
    batch, sub_seq, _ = q.shape
    blocks = min(DILATED_BLOCKS_PER_STEP, sub_seq // BLOCK)
    residues = min(DILATED_BLOCKS_PER_STEP // blocks, dil)
    tile = blocks * BLOCK
    width = residues * GROUP_W
    cur = pl.BlockSpec((None, tile, width), lambda b, r, n: (b, n, r))
    prev = pl.BlockSpec((None, BLOCK, width),
                        lambda b, r, n: (b, jnp.maximum(n * blocks - 1, 0), r))
    return pl.pallas_call(
        functools.partial(_dilated_body, blocks=blocks, residues=residues),
        grid=(batch, dil // residues, sub_seq // tile),
        in_specs=[cur, cur, cur, prev, prev, _resident(bias_band.shape)],
        out_specs=[cur, cur],
        out_shape=[jax.ShapeDtypeStruct(q.shape, BF16), jax.ShapeDtypeStruct(q.shape, F32)],
        compiler_params=_compiler_params(("parallel", "parallel", "parallel")),
        name=f"dilated_d{dil}",
    )(q, k, v, k, v, bias_band)


def _bias_band(rel_bias, group, dil, n_back):
    heads = slice(group * HEADS_PER_GROUP, (group + 1) * HEADS_PER_GROUP)
    buckets = _t5_bucket(np.arange(n_back + 1) * dil)
    bias = rel_bias[buckets][:, heads].T.astype(F32)
    period = 2 * BLOCK + 1
    j = BLOCK - np.arange(period)
    local = (j >= 0) & (j <= n_back)
    diag = jnp.where(local[None], bias[:, np.clip(j, 0, n_back)], MASKED)
    band = jnp.tile(diag, (1, BLOCK))[:, :BLOCK * 2 * BLOCK]
    return band.reshape(HEADS_PER_GROUP * BLOCK, 2 * BLOCK)


class _StickStep(NamedTuple):
    q: jax.Array
    k: jax.Array
    v: jax.Array
    mask: Optional[jax.Array]
    carry: Any
    after: Optional[int]


def _neg_abs(x):
    bits = lax.bitcast_convert_type(x, jnp.uint32) | jnp.uint32(0x80000000)
    return lax.bitcast_convert_type(bits, F32)


def _stick_steps(steps, upper):
    z = [lax.dot_general(st.q, st.k, (((1,), (1,)), ((), ())), preferred_element_type=F32)
         for st in steps]
    z = [s if st.mask is None else jnp.where(st.mask, s, MASKED) for s, st in zip(z, steps)]
    soft = [jnp.maximum(s, 0.0) + jnp.log(1.0 + jnp.exp(_neg_abs(s))) for s in z]
    log_beta = [s - p for s, p in zip(z, soft)]
    tail = [jnp.dot(p.astype(BF16), upper, preferred_element_type=F32) for p in soft]
    sums = [jnp.sum(p, axis=-1, keepdims=True) for p in soft]
    outs = []
    for i, st in enumerate(steps):
        carry = st.carry if st.after is None else st.carry + sums[st.after]
        w = jnp.exp(log_beta[i] - (tail[i] + carry))
        outs.append(w.astype(BF16))
    outs = [jnp.dot(w, st.v, preferred_element_type=F32) for w, st in zip(outs, steps)]
    return outs, sums


def _stick_body(q_ref, k_ref, v_ref, upper_ref, o_ref, acc_ref, carry_ref):
    tile = upper_ref.shape[0]
    q_tiles = q_ref.shape[0] // tile
    heads = LANES // HEAD_DIM
    first_tile = pl.program_id(2) * q_tiles
    lane_head = lax.broadcasted_iota(jnp.int32, (tile, LANES), 1) // HEAD_DIM
    row = lax.broadcasted_iota(jnp.int32, (tile, tile), 0)
    col = lax.broadcasted_iota(jnp.int32, (tile, tile), 1)
    strictly_before = col < row
    upper = upper_ref[...]
    chains = []
    for t in range(q_tiles):
        q = q_ref[t * tile:(t + 1) * tile, :]
        for hd in range(heads):
            chains.append((t, hd, jnp.where(lane_head == hd, q, jnp.zeros_like(q))))

    def key_value(kt):
        start = pl.multiple_of(jnp.maximum(kt, 0) * tile, tile)
        v = v_ref[pl.ds(start, tile), :]
        return k_ref[pl.ds(start, tile), :], jnp.where(kt >= 0, v, jnp.zeros_like(v))

    diag = [key_value(first_tile + t) for t in range(q_tiles)]
    prev = [key_value(first_tile + t - 1) for t in range(q_tiles)]
    steps = []
    for t, hd, qh in chains:
        steps.append(_StickStep(qh, *diag[t], strictly_before, 0.0, None))
        steps.append(_StickStep(qh, *prev[t], None, 0.0, len(steps) - 1))
    outs, sums = _stick_steps(steps, upper)
    for c in range(len(chains)):
        acc_ref[c] = outs[2 * c] + outs[2 * c + 1]
        carry_ref[c] = sums[2 * c] + sums[2 * c + 1]

    def any_weight_left():
        return (jnp.min(carry_ref[...]) < F32_EXP_UNDERFLOW).astype(jnp.int32)

    def more_tiles(state):
        j, live = state
        return jnp.logical_and(j < first_tile + q_tiles - 1, live > 0)

    def earlier(state):
        j, _ = state
        tiles = [key_value(first_tile + t - 1 - j) for t in range(q_tiles)]
        outs, sums = _stick_steps(
            [_StickStep(qh, *tiles[t], None, carry_ref[c], None)
             for c, (t, _, qh) in enumerate(chains)], upper)
        for c in range(len(chains)):
            acc_ref[c] += outs[c]
            carry_ref[c] += sums[c]
        return j + 1, any_weight_left()

    lax.while_loop(more_tiles, earlier, (jnp.int32(1), any_weight_left()))
    for t in range(q_tiles):
        out = acc_ref[t * heads]
        for hd in range(1, heads):
            out = jnp.where(lane_head == hd, acc_ref[t * heads + hd], out)
        o_ref[t * tile:(t + 1) * tile, :] = out.astype(o_ref.dtype)


def _stick_breaking(q, k, v):
    batch, seq, width = q.shape
    tile = STICK_TILE
    rows = STICK_Q_TILES * tile
    chains = STICK_Q_TILES * (LANES // HEAD_DIM)
    j = np.arange(tile)
    upper = jnp.asarray((j[:, None] > j[None, :]), BF16)
    return pl.pallas_call(
        _stick_body,
        grid=(batch, width // LANES, seq // rows),
        in_specs=[pl.BlockSpec((None, rows, LANES), lambda b, p, i: (b, i, p)),
                  pl.BlockSpec((None, seq, LANES), lambda b, p, i: (b, 0, p)),
                  pl.BlockSpec((None, seq, LANES), lambda b, p, i: (b, 0, p)),
                  _resident((tile, tile))],
        out_specs=pl.BlockSpec((None, rows, LANES), lambda b, p, i: (b, i, p)),
        out_shape=jax.ShapeDtypeStruct(q.shape, BF16),
        scratch_shapes=[pltpu.VMEM((chains, tile, LANES), F32),
                        pltpu.VMEM((chains, tile, 1), F32)],
        compiler_params=_compiler_params(("parallel", "parallel", "parallel")),
        name="stick_breaking",
    )(q, k, v, upper)


def _post_mix_body(x_ref, gates_ref, yb_ref, *refs):
    group_refs = refs[:2 * N_GROUPS]
    wa_ref, wb_ref, wo_ref, g_ref, out_ref, slab_ref = refs[2 * N_GROUPS:]
    tile, d_model = x_ref.shape

    def natural_order(ref, dil):
        if dil == 1:
            return ref[...].astype(F32)
        rows = tile // dil
        for r in range(dil):
            for s in range(GROUP_W // LANES):
                lane0 = r * GROUP_W + s * LANES
                slab_ref[s, pl.ds(r, rows, stride=dil), :] = (
                    ref[:, lane0:lane0 + LANES].astype(F32))
        return jnp.concatenate([slab_ref[s] for s in range(GROUP_W // LANES)], axis=-1)

    def sigmoid(v):
        return 0.5 * jnp.tanh(0.5 * v) + 0.5

    branch_b = jnp.dot(yb_ref[...], wb_ref[...], preferred_element_type=F32)
    outs, lses = [], []
    for g, (_, dil) in enumerate(DILATED_GROUPS):
        outs.append(natural_order(group_refs[2 * g], dil))
        lses.append(natural_order(group_refs[2 * g + 1], dil))
    top = functools.reduce(jnp.maximum, lses)
    wts = [jnp.exp(l - top) for l in lses]
    ya = sum(w * o for w, o in zip(wts, outs)) / sum(wts)

    branch_a = jnp.dot(ya.astype(BF16), wa_ref[...], preferred_element_type=F32)
    gate_a = gates_ref[:, :d_model].astype(F32)
    gate_b = gates_ref[:, d_model:].astype(F32)
    merged = sigmoid(gate_a) * branch_a + sigmoid(gate_b) * branch_b
    mixed = jnp.dot(merged.astype(BF16), wo_ref[...], preferred_element_type=F32)
    out_ref[...] = x_ref[...] + _rms_norm(mixed, g_ref[...])


def _post_mix(x, gates, yb, groups, wa, wb, wo, g):
    batch, seq, d_model = x.shape
    tile = TOKEN_TILE
    row_spec = lambda width: pl.BlockSpec((None, tile, width), lambda b, i: (b, i, 0))
    in_specs = [row_spec(d_model), row_spec(gates.shape[-1]), row_spec(yb.shape[-1])]
    operands = [x, gates, yb]
    for (o, lse), (_, dil) in zip(groups, DILATED_GROUPS):
        for arr in (o, lse):
            in_specs.append(pl.BlockSpec((None, tile // dil, dil * GROUP_W), lambda b, i: (b, i, 0)))
            operands.append(arr)
    for arr in (wa, wb, wo, g):
        in_specs.append(_resident(arr.shape))
        operands.append(arr)
    return pl.pallas_call(
        _post_mix_body,
        grid=(batch, seq // tile),
        in_specs=in_specs,
        out_specs=row_spec(d_model),
        out_shape=jax.ShapeDtypeStruct(x.shape, F32),
        scratch_shapes=[pltpu.VMEM((GROUP_W // LANES, tile, LANES), F32)],
        compiler_params=_compiler_params(("parallel", "parallel")),
        name="post_mix",
    )(*operands)


def _ffn_body(x_ref, g_pre_ref, w_up_ref, cw_ref, cb_ref, w_down_ref, g_post_ref, out_ref,
              halo_ref, u_ref, act_ref, acc_ref, slab_ref):
    tile, d_model = x_ref.shape
    d_ff = w_down_ref.shape[0]
    seg = tile // SUBLANES
    n_slabs = d_model // LANES
    halo_rows = (CONV_WIDTH - 1) * SUBLANES

    @pl.when(pl.program_id(1) == 0)
    def _():
        halo_ref[...] = jnp.zeros_like(halo_ref)

    for c in range(n_slabs):
        for s in range(SUBLANES):
            slab_ref[c, pl.ds(s, seg, stride=SUBLANES), :] = (
                x_ref[s * seg:(s + 1) * seg, c * LANES:(c + 1) * LANES])
    x = jnp.concatenate([slab_ref[c] for c in range(n_slabs)], axis=-1)
    h = _rms_norm(x, g_pre_ref[...]).astype(BF16)
    sublane = lax.broadcasted_iota(jnp.int32, (SUBLANES, FF_CHUNK), 0)

    def up(c):
        return [jnp.dot(h, w_up_ref[:, col:col + FF_CHUNK], preferred_element_type=F32)
                for col in (c * FF_CHUNK, d_ff + c * FF_CHUNK)]

    def conv(u, col, slot):
        cols = slice(col, col + FF_CHUNK)
        last = u[tile - halo_rows:, :]
        for k in range(CONV_WIDTH - 1):
            group = slice(k * SUBLANES, (k + 1) * SUBLANES)
            mixed = jnp.where(sublane == SUBLANES - 1, halo_ref[group, cols], last[group, :])
            u_ref[slot, group, :] = pltpu.roll(mixed, 1, axis=0)
        u_ref[slot, halo_rows:, :] = u
        halo_ref[:, cols] = last
        y = cb_ref[:, cols]
        for t in range(CONV_WIDTH):
            start = t * SUBLANES
            y = y + cw_ref[t:t + 1, cols] * u_ref[slot, start:start + tile, :]
        return y

    n_chunks = d_ff // FF_CHUNK
    u_next = up(0)
    first = 0
    for c in range(n_chunks):
        u_gate, u_val = u_next
        if c + 1 < n_chunks:
            u_next = up(c + 1)
        gate = conv(u_gate, c * FF_CHUNK, 0)
        val = conv(u_val, d_ff + c * FF_CHUNK, 1)
        act_ref[:, c * FF_CHUNK:(c + 1) * FF_CHUNK] = (
            jax.nn.gelu(gate, approximate=True) * val).astype(BF16)
        if (c + 1) % DOWN_GROUP == 0 or c + 1 == n_chunks:
            rows = slice(first * FF_CHUNK, (c + 1) * FF_CHUNK)
            part = jnp.dot(act_ref[:, rows], w_down_ref[rows, :], preferred_element_type=F32)
            if first == 0:
                acc_ref[...] = part
            else:
                acc_ref[...] += part
            first = c + 1
    y = x + _rms_norm(acc_ref[...], g_post_ref[...])
    for c in range(n_slabs):
        slab_ref[c] = y[:, c * LANES:(c + 1) * LANES]
    for c in range(n_slabs):
        for s in range(SUBLANES):
            out_ref[s * seg:(s + 1) * seg, c * LANES:(c + 1) * LANES] = (
                slab_ref[c, pl.ds(s, seg, stride=SUBLANES), :])


def _ffn(x, g_pre, w_up, conv_w, conv_b, w_down, g_post):
    batch, seq, d_model = x.shape
    tile = TOKEN_TILE
    d_ff = w_down.shape[0]
    row_spec = pl.BlockSpec((None, tile, d_model), lambda b, i: (b, i, 0))
    consts = (g_pre, w_up, conv_w, conv_b, w_down, g_post)
    halo_rows = (CONV_WIDTH - 1) * SUBLANES
    return pl.pallas_call(
        _ffn_body,
        grid=(batch, seq // tile),
        in_specs=[row_spec] + [_resident(a.shape) for a in consts],
        out_specs=row_spec,
        out_shape=jax.ShapeDtypeStruct(x.shape, F32),
        scratch_shapes=[pltpu.VMEM((halo_rows, 2 * d_ff), F32),
                        pltpu.VMEM((2, halo_rows + tile, FF_CHUNK), F32),
                        pltpu.VMEM((tile, d_ff), BF16),
                        pltpu.VMEM((tile, d_model), F32),
                        pltpu.VMEM((d_model // LANES, tile, LANES), F32)],
        compiler_params=_compiler_params(("arbitrary", "arbitrary")),
        name="ffn",
    )(x, *consts)


def kernel(x, g_pre_mix, w_in, rel_bias, w_branch_a, w_branch_b, w_out, g_post_mix, g_pre_ffn,
           w_up, conv_w, conv_b, w_down, g_post_ffn):
    depth = w_in.shape[0]
    for l in range(depth):
        proj = _in_proj(x, g_pre_mix[l][None], w_in[l].astype(BF16))
        qb, kb, vb, gates = proj[3 * N_GROUPS:]
        groups = []
        for g, (window, dil) in enumerate(DILATED_GROUPS):
            q, k, v = proj[3 * g:3 * g + 3]
            band = _bias_band(rel_bias, g, dil, window // dil)
            groups.append(_dilated_group(q, k, v, band, dil))
        yb = _stick_breaking(qb, kb, vb)
        x = _post_mix(x, gates, yb, groups, w_branch_a[l].astype(BF16), w_branch_b[l].astype(BF16),
                      w_out[l].astype(BF16), g_post_mix[l][None])
        x = _ffn(x, g_pre_ffn[l][None], w_up[l].astype(BF16), conv_w[l], conv_b[l][None],
                 w_down[l].astype(BF16), g_post_ffn[l][None])
    return x
```

```python
import functools
import math
from typing import Any, NamedTuple, Optional

import numpy as np
import jax
import jax.numpy as jnp
from jax import lax
from jax.experimental import pallas as pl
from jax.experimental.pallas import tpu as pltpu

F32 = jnp.float32
BF16 = jnp.bfloat16

HEAD_DIM = 64
DILATED_GROUPS = ((128, 1), (512, 4), (2048, 16))
HEADS_PER_GROUP = 4
N_GROUPS = len(DILATED_GROUPS)
N_HEADS_A = HEADS_PER_GROUP * N_GROUPS
N_HEADS_B = 8
GROUP_W = HEADS_PER_GROUP * HEAD_DIM
A_QKV = N_HEADS_A * HEAD_DIM
B_QKV = N_HEADS_B * HEAD_DIM
BLOCK = 128
NUM_BUCKETS = 32
MAX_DISTANCE = 2048
CONV_WIDTH = 3
EPS = 1e-6
QK_SCALE = 1.0 / math.sqrt(HEAD_DIM)
MASKED = -1e30
F32_EXP_UNDERFLOW = 104.0

LANES = 128
SUBLANES = 8
VMEM_LIMIT_BYTES = 56 * 1024 * 1024

TOKEN_TILE = 512
IN_PROJ_TILE = 1024
DILATED_BLOCKS_PER_STEP = 8
STICK_Q_TILES = 4
STICK_TILE = 256
FF_CHUNK = 256
DOWN_GROUP = 3


def _t5_bucket(dist):
    max_exact = NUM_BUCKETS // 2
    n = np.asarray(dist, dtype=np.float32)
    large = max_exact + (np.log(np.maximum(n, 1.0) / max_exact)
                         / np.log(MAX_DISTANCE / max_exact)
                         * (NUM_BUCKETS - max_exact)).astype(np.int32)
    large = np.minimum(large, NUM_BUCKETS - 1)
    return np.where(n < max_exact, n.astype(np.int32), large).astype(np.int32)


def _rms_norm(x, g):
    return x * lax.rsqrt(jnp.mean(x * x, axis=-1, keepdims=True) + EPS) * g


def _compiler_params(semantics):
    return pltpu.CompilerParams(dimension_semantics=semantics, vmem_limit_bytes=VMEM_LIMIT_BYTES)


def _resident(shape):
    return pl.BlockSpec(shape, lambda *_: (0,) * len(shape), pipeline_mode=pl.Buffered(1))


def _in_proj_body(x_ref, g_ref, w_ref, *refs):
    a_refs = refs[:3 * N_GROUPS]
    qb_ref, kb_ref, vb_ref, gates_ref, slab_ref = refs[3 * N_GROUPS:]
    tile = x_ref.shape[0]
    h = _rms_norm(x_ref[...], g_ref[...]).astype(BF16)

    def proj(col, width):
        return jnp.dot(h, w_ref[:, col:col + width], preferred_element_type=F32)

    for t in range(3):
        scale = QK_SCALE if t == 0 else 1.0
        for g, (_, dil) in enumerate(DILATED_GROUPS):
            acc = proj(t * A_QKV + g * GROUP_W, GROUP_W) * scale
            out_ref = a_refs[3 * g + t]
            if dil == 1:
                out_ref[...] = acc.astype(BF16)
                continue
            rows = tile // dil
            for s in range(GROUP_W // LANES):
                slab_ref[s] = acc[:, s * LANES:(s + 1) * LANES]
            for r in range(dil):
                for s in range(GROUP_W // LANES):
                    lane0 = r * GROUP_W + s * LANES
                    out_ref[:, lane0:lane0 + LANES] = (
                        slab_ref[s, pl.ds(r, rows, stride=dil), :].astype(BF16))

    base = 3 * A_QKV
    qb_ref[...] = (proj(base, B_QKV) * QK_SCALE).astype(BF16)
    kb_ref[...] = proj(base + B_QKV, B_QKV).astype(BF16)
    vb_ref[...] = proj(base + 2 * B_QKV, B_QKV).astype(BF16)
    base += 3 * B_QKV
    gate_w = gates_ref.shape[1]
    for c in range(0, gate_w, 512):
        gates_ref[:, c:c + 512] = proj(base + c, 512).astype(BF16)


def _in_proj(x, g, w):
    batch, seq, d_model = x.shape
    tile = IN_PROJ_TILE
    n_in = w.shape[1]
    gate_w = n_in - 3 * A_QKV - 3 * B_QKV
    out_shape, out_specs = [], []
    for _, dil in DILATED_GROUPS:
        for _ in range(3):
            out_shape.append(jax.ShapeDtypeStruct((batch, seq // dil, dil * GROUP_W), BF16))
            out_specs.append(pl.BlockSpec((None, tile // dil, dil * GROUP_W), lambda b, i: (b, i, 0)))
    for _ in range(3):
        out_shape.append(jax.ShapeDtypeStruct((batch, seq, B_QKV), BF16))
        out_specs.append(pl.BlockSpec((None, tile, B_QKV), lambda b, i: (b, i, 0)))
    out_shape.append(jax.ShapeDtypeStruct((batch, seq, gate_w), BF16))
    out_specs.append(pl.BlockSpec((None, tile, gate_w), lambda b, i: (b, i, 0)))
    return pl.pallas_call(
        _in_proj_body,
        grid=(batch, seq // tile),
        in_specs=[pl.BlockSpec((None, tile, d_model), lambda b, i: (b, i, 0)),
                  _resident((1, d_model)),
                  _resident((d_model, n_in))],
        out_specs=out_specs,
        out_shape=out_shape,
        scratch_shapes=[pltpu.VMEM((GROUP_W // LANES, tile, LANES), F32)],
        compiler_params=_compiler_params(("parallel", "parallel")),
        name="in_proj",
    )(x, g, w)


def _dilated_body(q_ref, k_ref, v_ref, kp_ref, vp_ref, bias_ref, o_ref, lse_ref, *,
                  blocks, residues):
    heads = HEADS_PER_GROUP
    missing_cols = jnp.where(pl.program_id(2) == 0, BLOCK, 0)
    lane_head = lax.broadcasted_iota(jnp.int32, (BLOCK, GROUP_W), 1) // HEAD_DIM
    key_col = lax.broadcasted_iota(jnp.int32, (heads * BLOCK, 2 * BLOCK), 1)
    bias = bias_ref[...]

    def keys_values(res):
        lanes = slice(res * GROUP_W, (res + 1) * GROUP_W)
        k_all = jnp.concatenate([kp_ref[:, lanes], k_ref[:, lanes]], axis=0)
        v_all = jnp.concatenate([vp_ref[:, lanes], v_ref[:, lanes]], axis=0)
        v_lane_head = lax.broadcasted_iota(jnp.int32, v_all.shape, 1) // HEAD_DIM
        v_heads = [jnp.where(v_lane_head == hd, v_all, jnp.zeros_like(v_all))
                   for hd in range(heads)]
        return k_all, v_heads

    def scores(res, nb, k_all):
        q = q_ref[nb * BLOCK:(nb + 1) * BLOCK, res * GROUP_W:(res + 1) * GROUP_W]
        q_stack = jnp.concatenate(
            [jnp.where(lane_head == hd, q, jnp.zeros_like(q)) for hd in range(heads)], axis=0)
        s = lax.dot_general(q_stack, k_all[nb * BLOCK:(nb + 2) * BLOCK], (((1,), (1,)), ((), ())),
                            preferred_element_type=F32) + bias
        if nb == 0:
            s = jnp.where(key_col < missing_cols, MASKED, s)
        return s

    def finish(res, nb, s, v_heads):
        mx = jnp.max(s, axis=-1, keepdims=True)
        e = jnp.exp(s - mx)
        den = jnp.sum(e, axis=-1, keepdims=True)
        p = (e * (1.0 / den)).astype(BF16)
        p_cat = jnp.concatenate([p[hd * BLOCK:(hd + 1) * BLOCK] for hd in range(heads)], axis=1)
        v_stack = jnp.concatenate([vh[nb * BLOCK:(nb + 2) * BLOCK] for vh in v_heads], axis=0)
        o = jnp.dot(p_cat, v_stack, preferred_element_type=F32)
        lse = mx + jnp.log(den)
        lse_full = jnp.broadcast_to(lse[:BLOCK], (BLOCK, GROUP_W))
        for hd in range(1, heads):
            lse_full = jnp.where(lane_head == hd, lse[hd * BLOCK:(hd + 1) * BLOCK], lse_full)
        rows = slice(nb * BLOCK, (nb + 1) * BLOCK)
        lanes = slice(res * GROUP_W, (res + 1) * GROUP_W)
        o_ref[rows, lanes] = o.astype(o_ref.dtype)
        lse_ref[rows, lanes] = lse_full

    units = [(res, nb) for res in range(residues) for nb in range(blocks)]
    kv = {res: keys_values(res) for res in range(residues)}
    s_next = scores(*units[0], kv[units[0][0]][0])
    for i, (res, nb) in enumerate(units):
        s = s_next
        if i + 1 < len(units):
            s_next = scores(*units[i + 1], kv[units[i + 1][0]][0])
        finish(res, nb, s, kv[res][1])


def _dilated_group(q, k, v, bias_band, dil):
    batch, sub_seq, _ = q.shape
    blocks = min(DILATED_BLOCKS_PER_STEP, sub_seq // BLOCK)
    residues = min(DILATED_BLOCKS_PER_STEP // blocks, dil)
    tile = blocks * BLOCK
    width = residues * GROUP_W
    cur = pl.BlockSpec((None, tile, width), lambda b, r, n: (b, n, r))
    prev = pl.BlockSpec((None, BLOCK, width),
                        lambda b, r, n: (b, jnp.maximum(n * blocks - 1, 0), r))
    return pl.pallas_call(
        functools.partial(_dilated_body, blocks=blocks, residues=residues),
        grid=(batch, dil // residues, sub_seq // tile),
        in_specs=[cur, cur, cur, prev, prev, _resident(bias_band.shape)],
        out_specs=[cur, cur],
        out_shape=[jax.ShapeDtypeStruct(q.shape, BF16), jax.ShapeDtypeStruct(q.shape, F32)],
        compiler_params=_compiler_params(("parallel", "parallel", "parallel")),
        name=f"dilated_d{dil}",
    )(q, k, v, k, v, bias_band)


def _bias_band(rel_bias, group, dil, n_back):
    heads = slice(group * HEADS_PER_GROUP, (group + 1) * HEADS_PER_GROUP)
    buckets = _t5_bucket(np.arange(n_back + 1) * dil)
    bias = rel_bias[buckets][:, heads].T.astype(F32)
    period = 2 * BLOCK + 1
    j = BLOCK - np.arange(period)
    local = (j >= 0) & (j <= n_back)
    diag = jnp.where(local[None], bias[:, np.clip(j, 0, n_back)], MASKED)
    band = jnp.tile(diag, (1, BLOCK))[:, :BLOCK * 2 * BLOCK]
    return band.reshape(HEADS_PER_GROUP * BLOCK, 2 * BLOCK)


class _StickStep(NamedTuple):
    q: jax.Array
    k: jax.Array
    v: jax.Array
    mask: Optional[jax.Array]
    carry: Any
    after: Optional[int]


def _neg_abs(x):
    bits = lax.bitcast_convert_type(x, jnp.uint32) | jnp.uint32(0x80000000)
    return lax.bitcast_convert_type(bits, F32)


def _stick_steps(steps, upper):
    z = [lax.dot_general(st.q, st.k, (((1,), (1,)), ((), ())), preferred_element_type=F32)
         for st in steps]
    z = [s if st.mask is None else jnp.where(st.mask, s, MASKED) for s, st in zip(z, steps)]
    soft = [jnp.maximum(s, 0.0) + jnp.log(1.0 + jnp.exp(_neg_abs(s))) for s in z]
    log_beta = [s - p for s, p in zip(z, soft)]
    tail = [jnp.dot(p.astype(BF16), upper, preferred_element_type=F32) for p in soft]
    sums = [jnp.sum(p, axis=-1, keepdims=True) for p in soft]
    outs = []
    for i, st in enumerate(steps):
        carry = st.carry if st.after is None else st.carry + sums[st.after]
        w = jnp.exp(log_beta[i] - (tail[i] + carry))
        outs.append(w.astype(BF16))
    outs = [jnp.dot(w, st.v, preferred_element_type=F32) for w, st in zip(outs, steps)]
    return outs, sums


def _stick_body(q_ref, k_ref, v_ref, upper_ref, o_ref, acc_ref, carry_ref):
    tile = upper_ref.shape[0]
    q_tiles = q_ref.shape[0] // tile
    heads = LANES // HEAD_DIM
    first_tile = pl.program_id(2) * q_tiles
    lane_head = lax.broadcasted_iota(jnp.int32, (tile, LANES), 1) // HEAD_DIM
    row = lax.broadcasted_iota(jnp.int32, (tile, tile), 0)
    col = lax.broadcasted_iota(jnp.int32, (tile, tile), 1)
    strictly_before = col < row
    upper = upper_ref[...]
    chains = []
    for t in range(q_tiles):
        q = q_ref[t * tile:(t + 1) * tile, :]
        for hd in range(heads):
            chains.append((t, hd, jnp.where(lane_head == hd, q, jnp.zeros_like(q))))

    def key_value(kt):
        start = pl.multiple_of(jnp.maximum(kt, 0) * tile, tile)
        v = v_ref[pl.ds(start, tile), :]
        return k_ref[pl.ds(start, tile), :], jnp.where(kt >= 0, v, jnp.zeros_like(v))

    diag = [key_value(first_tile + t) for t in range(q_tiles)]
    prev = [key_value(first_tile + t - 1) for t in range(q_tiles)]
    steps = []
    for t, hd, qh in chains:
        steps.append(_StickStep(qh, *diag[t], strictly_before, 0.0, None))
        steps.append(_StickStep(qh, *prev[t], None, 0.0, len(steps) - 1))
    outs, sums = _stick_steps(steps, upper)
    for c in range(len(chains)):
        acc_ref[c] = outs[2 * c] + outs[2 * c + 1]
        carry_ref[c] = sums[2 * c] + sums[2 * c + 1]

    def any_weight_left():
        return (jnp.min(carry_ref[...]) < F32_EXP_UNDERFLOW).astype(jnp.int32)

    def more_tiles(state):
        j, live = state
        return jnp.logical_and(j < first_tile + q_tiles - 1, live > 0)

    def earlier(state):
        j, _ = state
        tiles = [key_value(first_tile + t - 1 - j) for t in range(q_tiles)]
        outs, sums = _stick_steps(
            [_StickStep(qh, *tiles[t], None, carry_ref[c], None)
             for c, (t, _, qh) in enumerate(chains)], upper)
        for c in range(len(chains)):
            acc_ref[c] += outs[c]
            carry_ref[c] += sums[c]
        return j + 1, any_weight_left()

    lax.while_loop(more_tiles, earlier, (jnp.int32(1), any_weight_left()))
    for t in range(q_tiles):
        out = acc_ref[t * heads]
        for hd in range(1, heads):
            out = jnp.where(lane_head == hd, acc_ref[t * heads + hd], out)
        o_ref[t * tile:(t + 1) * tile, :] = out.astype(o_ref.dtype)


def _stick_breaking(q, k, v):
    batch, seq, width = q.shape
    tile = STICK_TILE
    rows = STICK_Q_TILES * tile
    chains = STICK_Q_TILES * (LANES // HEAD_DIM)
    j = np.arange(tile)
    upper = jnp.asarray((j[:, None] > j[None, :]), BF16)
    return pl.pallas_call(
        _stick_body,
        grid=(batch, width // LANES, seq // rows),
        in_specs=[pl.BlockSpec((None, rows, LANES), lambda b, p, i: (b, i, p)),
                  pl.BlockSpec((None, seq, LANES), lambda b, p, i: (b, 0, p)),
                  pl.BlockSpec((None, seq, LANES), lambda b, p, i: (b, 0, p)),
                  _resident((tile, tile))],
        out_specs=pl.BlockSpec((None, rows, LANES), lambda b, p, i: (b, i, p)),
        out_shape=jax.ShapeDtypeStruct(q.shape, BF16),
        scratch_shapes=[pltpu.VMEM((chains, tile, LANES), F32),
                        pltpu.VMEM((chains, tile, 1), F32)],
        compiler_params=_compiler_params(("parallel", "parallel", "parallel")),
        name="stick_breaking",
    )(q, k, v, upper)


def _mix_tile(x_ref, gates_ref, yb_ref, group_refs, wa_ref, wb_ref, wo_ref, g_ref, slab_ref):
    tile, d_model = x_ref.shape

    def natural_order(ref, dil):
        if dil == 1:
            return ref[...].astype(F32)
        rows = tile // dil
        for r in range(dil):
            for s in range(GROUP_W // LANES):
                lane0 = r * GROUP_W + s * LANES
                slab_ref[s, pl.ds(r, rows, stride=dil), :] = (
                    ref[:, lane0:lane0 + LANES].astype(F32))
        return jnp.concatenate([slab_ref[s] for s in range(GROUP_W // LANES)], axis=-1)

    def sigmoid(v):
        return 0.5 * jnp.tanh(0.5 * v) + 0.5

    branch_b = jnp.dot(yb_ref[...], wb_ref[...], preferred_element_type=F32)
    outs, lses = [], []
    for g, (_, dil) in enumerate(DILATED_GROUPS):
        outs.append(natural_order(group_refs[2 * g], dil))
        lses.append(natural_order(group_refs[2 * g + 1], dil))
    top = functools.reduce(jnp.maximum, lses)
    wts = [jnp.exp(l - top) for l in lses]
    ya = sum(w * o for w, o in zip(wts, outs)) / sum(wts)

    branch_a = jnp.dot(ya.astype(BF16), wa_ref[...], preferred_element_type=F32)
    gate_a = gates_ref[:, :d_model].astype(F32)
    gate_b = gates_ref[:, d_model:].astype(F32)
    merged = sigmoid(gate_a) * branch_a + sigmoid(gate_b) * branch_b
    mixed = jnp.dot(merged.astype(BF16), wo_ref[...], preferred_element_type=F32)
    return x_ref[...] + _rms_norm(mixed, g_ref[...])


def _ffn_tile(x_mid, g_pre_ref, w_up_ref, cw_ref, cb_ref, w_down_ref, g_post_ref, out_ref,
              halo_ref, u_ref, act_ref, acc_ref, slab_ref):
    tile, d_model = x_mid.shape
    d_ff = w_down_ref.shape[0]
    seg = tile // SUBLANES
    n_slabs = d_model // LANES
    halo_rows = (CONV_WIDTH - 1) * SUBLANES

    @pl.when(pl.program_id(1) == 0)
    def _():
        halo_ref[...] = jnp.zeros_like(halo_ref)

    for c in range(n_slabs):
        for s in range(SUBLANES):
            slab_ref[c, pl.ds(s, seg, stride=SUBLANES), :] = (
                x_mid[s * seg:(s + 1) * seg, c * LANES:(c + 1) * LANES])
    x = jnp.concatenate([slab_ref[c] for c in range(n_slabs)], axis=-1)
    h = _rms_norm(x, g_pre_ref[...]).astype(BF16)
    sublane = lax.broadcasted_iota(jnp.int32, (SUBLANES, FF_CHUNK), 0)

    def up(c):
        return [jnp.dot(h, w_up_ref[:, col:col + FF_CHUNK], preferred_element_type=F32)
                for col in (c * FF_CHUNK, d_ff + c * FF_CHUNK)]

    def conv(u, col, slot):
        cols = slice(col, col + FF_CHUNK)
        last = u[tile - halo_rows:, :]
        for k in range(CONV_WIDTH - 1):
            group = slice(k * SUBLANES, (k + 1) * SUBLANES)
            mixed = jnp.where(sublane == SUBLANES - 1, halo_ref[group, cols], last[group, :])
            u_ref[slot, group, :] = pltpu.roll(mixed, 1, axis=0)
        u_ref[slot, halo_rows:, :] = u
        halo_ref[:, cols] = last
        y = cb_ref[:, cols]
        for t in range(CONV_WIDTH):
            start = t * SUBLANES
            y = y + cw_ref[t:t + 1, cols] * u_ref[slot, start:start + tile, :]
        return y

    n_chunks = d_ff // FF_CHUNK
    u_next = up(0)
    first = 0
    for c in range(n_chunks):
        u_gate, u_val = u_next
        if c + 1 < n_chunks:
            u_next = up(c + 1)
        gate = conv(u_gate, c * FF_CHUNK, 0)
        val = conv(u_val, d_ff + c * FF_CHUNK, 1)
        act_ref[:, c * FF_CHUNK:(c + 1) * FF_CHUNK] = (
            jax.nn.gelu(gate, approximate=True) * val).astype(BF16)
        if (c + 1) % DOWN_GROUP == 0 or c + 1 == n_chunks:
            rows = slice(first * FF_CHUNK, (c + 1) * FF_CHUNK)
            part = jnp.dot(act_ref[:, rows], w_down_ref[rows, :], preferred_element_type=F32)
            if first == 0:
                acc_ref[...] = part
            else:
                acc_ref[...] += part
            first = c + 1
    y = x + _rms_norm(acc_ref[...], g_post_ref[...])
    for c in range(n_slabs):
        slab_ref[c] = y[:, c * LANES:(c + 1) * LANES]
    for c in range(n_slabs):
        for s in range(SUBLANES):
            out_ref[s * seg:(s + 1) * seg, c * LANES:(c + 1) * LANES] = (
                slab_ref[c, pl.ds(s, seg, stride=SUBLANES), :])


def _post_attention_body(x_ref, gates_ref, yb_ref, *refs):
    group_refs = refs[:2 * N_GROUPS]
    (wa_ref, wb_ref, wo_ref, g_post_mix_ref, g_pre_ref, w_up_ref, cw_ref, cb_ref, w_down_ref,
     g_post_ref, out_ref, group_slab_ref, halo_ref, u_ref, act_ref, acc_ref,
     slab_ref) = refs[2 * N_GROUPS:]
    x_mid = _mix_tile(x_ref, gates_ref, yb_ref, group_refs, wa_ref, wb_ref, wo_ref,
                      g_post_mix_ref, group_slab_ref)
    _ffn_tile(x_mid, g_pre_ref, w_up_ref, cw_ref, cb_ref, w_down_ref, g_post_ref, out_ref,
              halo_ref, u_ref, act_ref, acc_ref, slab_ref)


def _post_attention(x, gates, yb, groups, mix_consts, ffn_consts):
    batch, seq, d_model = x.shape
    tile = TOKEN_TILE
    d_ff = ffn_consts[4].shape[0]
    row_spec = lambda width: pl.BlockSpec((None, tile, width), lambda b, i: (b, i, 0))
    in_specs = [row_spec(d_model), row_spec(gates.shape[-1]), row_spec(yb.shape[-1])]
    operands = [x, gates, yb]
    for (o, lse), (_, dil) in zip(groups, DILATED_GROUPS):
        for arr in (o, lse):
            in_specs.append(pl.BlockSpec((None, tile // dil, dil * GROUP_W), lambda b, i: (b, i, 0)))
            operands.append(arr)
    for arr in (*mix_consts, *ffn_consts):
        in_specs.append(_resident(arr.shape))
        operands.append(arr)
    halo_rows = (CONV_WIDTH - 1) * SUBLANES
    return pl.pallas_call(
        _post_attention_body,
        grid=(batch, seq // tile),
        in_specs=in_specs,
        out_specs=row_spec(d_model),
        out_shape=jax.ShapeDtypeStruct(x.shape, F32),
        scratch_shapes=[pltpu.VMEM((GROUP_W // LANES, tile, LANES), F32),
                        pltpu.VMEM((halo_rows, 2 * d_ff), F32),
                        pltpu.VMEM((2, halo_rows + tile, FF_CHUNK), F32),
                        pltpu.VMEM((tile, d_ff), BF16),
                        pltpu.VMEM((tile, d_model), F32),
                        pltpu.VMEM((d_model // LANES, tile, LANES), F32)],
        compiler_params=_compiler_params(("arbitrary", "arbitrary")),
        name="post_attention",
    )(*operands)


def kernel(x, g_pre_mix, w_in, rel_bias, w_branch_a, w_branch_b, w_out, g_post_mix, g_pre_ffn,
           w_up, conv_w, conv_b, w_down, g_post_ffn):
    depth = w_in.shape[0]
    for l in range(depth):
        proj = _in_proj(x, g_pre_mix[l][None], w_in[l].astype(BF16))
        qb, kb, vb, gates = proj[3 * N_GROUPS:]
        groups = []
        for g, (window, dil) in enumerate(DILATED_GROUPS):
            q, k, v = proj[3 * g:3 * g + 3]
            band = _bias_band(rel_bias, g, dil, window // dil)
            groups.append(_dilated_group(q, k, v, band, dil))
        yb = _stick_breaking(qb, kb, vb)
        mix_consts = (w_branch_a[l].astype(BF16), w_branch_b[l].astype(BF16),
                      w_out[l].astype(BF16), g_post_mix[l][None])
        ffn_consts = (g_pre_ffn[l][None], w_up[l].astype(BF16), conv_w[l], conv_b[l][None],
                      w_down[l].astype(BF16), g_post_ffn[l][None])
        x = _post_attention(x, gates, yb, groups, mix_consts, ffn_consts)
    return x
```

```python
import functools
import math
from typing import Any, NamedTuple, Optional

import numpy as np
import jax
import jax.numpy as jnp
from jax import lax
from jax.experimental import pallas as pl
from jax.experimental.pallas import tpu as pltpu

F32 = jnp.float32
BF16 = jnp.bfloat16

HEAD_DIM = 64
DILATED_GROUPS = ((128, 1), (512, 4), (2048, 16))
HEADS_PER_GROUP = 4
N_GROUPS = len(DILATED_GROUPS)
N_HEADS_A = HEADS_PER_GROUP * N_GROUPS
N_HEADS_B = 8
GROUP_W = HEADS_PER_GROUP * HEAD_DIM
A_QKV = N_HEADS_A * HEAD_DIM
B_QKV = N_HEADS_B * HEAD_DIM
BLOCK = 128
NUM_BUCKETS = 32
MAX_DISTANCE = 2048
CONV_WIDTH = 3
EPS = 1e-6
QK_SCALE = 1.0 / math.sqrt(HEAD_DIM)
LOG2E = math.log2(math.e)
MASKED = -1e30
F32_EXP_UNDERFLOW = 104.0

LANES = 128
SUBLANES = 8
VMEM_LIMIT_BYTES = 56 * 1024 * 1024

TOKEN_TILE = 512
IN_PROJ_TILE = 1024
DILATED_BLOCKS_PER_STEP = 8
STICK_Q_TILES = 8
STICK_TILE = 256
FF_CHUNK = 256
DOWN_GROUP = 3


def _t5_bucket(dist):
    max_exact = NUM_BUCKETS // 2
    n = np.asarray(dist, dtype=np.float32)
    large = max_exact + (np.log(np.maximum(n, 1.0) / max_exact)
                         / np.log(MAX_DISTANCE / max_exact)
                         * (NUM_BUCKETS - max_exact)).astype(np.int32)
    large = np.minimum(large, NUM_BUCKETS - 1)
    return np.where(n < max_exact, n.astype(np.int32), large).astype(np.int32)


def _rms_norm(x, g):
    return x * lax.rsqrt(jnp.mean(x * x, axis=-1, keepdims=True) + EPS) * g


def _compiler_params(semantics):
    return pltpu.CompilerParams(dimension_semantics=semantics, vmem_limit_bytes=VMEM_LIMIT_BYTES)


def _resident(shape):
    return pl.BlockSpec(shape, lambda *_: (0,) * len(shape), pipeline_mode=pl.Buffered(1))


def _in_proj_body(x_ref, g_ref, w_ref, *refs):
    a_refs = refs[:3 * N_GROUPS]
    qb_ref, kb_ref, vb_ref, gates_ref, slab_ref = refs[3 * N_GROUPS:]
    tile = x_ref.shape[0]
    h = _rms_norm(x_ref[...], g_ref[...]).astype(BF16)

    def proj(col, width):
        return jnp.dot(h, w_ref[:, col:col + width], preferred_element_type=F32)

    for t in range(3):
        scale = QK_SCALE * LOG2E if t == 0 else 1.0
        for g, (_, dil) in enumerate(DILATED_GROUPS):
            acc = proj(t * A_QKV + g * GROUP_W, GROUP_W) * scale
            out_ref = a_refs[3 * g + t]
            if dil == 1:
                out_ref[...] = acc.astype(BF16)
                continue
            rows = tile // dil
            for s in range(GROUP_W // LANES):
                slab_ref[s] = acc[:, s * LANES:(s + 1) * LANES]
            for r in range(dil):
                for s in range(GROUP_W // LANES):
                    lane0 = r * GROUP_W + s * LANES
                    out_ref[:, lane0:lane0 + LANES] = (
                        slab_ref[s, pl.ds(r, rows, stride=dil), :].astype(BF16))

    base = 3 * A_QKV
    qb_ref[...] = (proj(base, B_QKV) * QK_SCALE).astype(BF16)
    kb_ref[...] = proj(base + B_QKV, B_QKV).astype(BF16)
    vb_ref[...] = proj(base + 2 * B_QKV, B_QKV).astype(BF16)
    base += 3 * B_QKV
    gate_w = gates_ref.shape[1]
    for c in range(0, gate_w, 512):
        gates_ref[:, c:c + 512] = proj(base + c, 512).astype(BF16)


def _in_proj(x, g, w):
    batch, seq, d_model = x.shape
    tile = IN_PROJ_TILE
    n_in = w.shape[1]
    gate_w = n_in - 3 * A_QKV - 3 * B_QKV
    out_shape, out_specs = [], []
    for _, dil in DILATED_GROUPS:
        for _ in range(3):
            out_shape.append(jax.ShapeDtypeStruct((batch, seq // dil, dil * GROUP_W), BF16))
            out_specs.append(pl.BlockSpec((None, tile // dil, dil * GROUP_W), lambda b, i: (b, i, 0)))
    for _ in range(3):
        out_shape.append(jax.ShapeDtypeStruct((batch, seq, B_QKV), BF16))
        out_specs.append(pl.BlockSpec((None, tile, B_QKV), lambda b, i: (b, i, 0)))
    out_shape.append(jax.ShapeDtypeStruct((batch, seq, gate_w), BF16))
    out_specs.append(pl.BlockSpec((None, tile, gate_w), lambda b, i: (b, i, 0)))
    return pl.pallas_call(
        _in_proj_body,
        grid=(batch, seq // tile),
        in_specs=[pl.BlockSpec((None, tile, d_model), lambda b, i: (b, i, 0)),
                  _resident((1, d_model)),
                  _resident((d_model, n_in))],
        out_specs=out_specs,
        out_shape=out_shape,
        scratch_shapes=[pltpu.VMEM((GROUP_W // LANES, tile, LANES), F32)],
        compiler_params=_compiler_params(("parallel", "parallel")),
        name="in_proj",
    )(x, g, w)


def _dilated_body(q_ref, k_ref, v_ref, kp_ref, vp_ref, bias_ref, o_ref, lse_ref, *,
                  blocks, residues):
    heads = HEADS_PER_GROUP
    missing_cols = jnp.where(pl.program_id(2) == 0, BLOCK, 0)
    lane_head = lax.broadcasted_iota(jnp.int32, (BLOCK, GROUP_W), 1) // HEAD_DIM
    key_col = lax.broadcasted_iota(jnp.int32, (heads * BLOCK, 2 * BLOCK), 1)
    bias = bias_ref[...]

    def keys_values(res):
        lanes = slice(res * GROUP_W, (res + 1) * GROUP_W)
        k_all = jnp.concatenate([kp_ref[:, lanes], k_ref[:, lanes]], axis=0)
        v_all = jnp.concatenate([vp_ref[:, lanes], v_ref[:, lanes]], axis=0)
        v_lane_head = lax.broadcasted_iota(jnp.int32, v_all.shape, 1) // HEAD_DIM
        v_heads = [jnp.where(v_lane_head == hd, v_all, jnp.zeros_like(v_all))
                   for hd in range(heads)]
        return k_all, v_heads

    def scores(res, nb, k_all):
        q = q_ref[nb * BLOCK:(nb + 1) * BLOCK, res * GROUP_W:(res + 1) * GROUP_W]
        q_stack = jnp.concatenate(
            [jnp.where(lane_head == hd, q, jnp.zeros_like(q)) for hd in range(heads)], axis=0)
        s = lax.dot_general(q_stack, k_all[nb * BLOCK:(nb + 2) * BLOCK], (((1,), (1,)), ((), ())),
                            preferred_element_type=F32) + bias
        if nb == 0:
            s = jnp.where(key_col < missing_cols, MASKED, s)
        return s

    def finish(res, nb, s, v_heads):
        mx = jnp.max(s, axis=-1, keepdims=True)
        e = jnp.exp2(s - mx)
        den = jnp.sum(e, axis=-1, keepdims=True)
        p = (e * (1.0 / den)).astype(BF16)
        p_cat = jnp.concatenate([p[hd * BLOCK:(hd + 1) * BLOCK] for hd in range(heads)], axis=1)
        v_stack = jnp.concatenate([vh[nb * BLOCK:(nb + 2) * BLOCK] for vh in v_heads], axis=0)
        o = jnp.dot(p_cat, v_stack, preferred_element_type=F32)
        lse = mx + jnp.log2(den)
        lse_full = jnp.broadcast_to(lse[:BLOCK], (BLOCK, GROUP_W))
        for hd in range(1, heads):
            lse_full = jnp.where(lane_head == hd, lse[hd * BLOCK:(hd + 1) * BLOCK], lse_full)
        rows = slice(nb * BLOCK, (nb + 1) * BLOCK)
        lanes = slice(res * GROUP_W, (res + 1) * GROUP_W)
        o_ref[rows, lanes] = o.astype(o_ref.dtype)
        lse_ref[rows, lanes] = lse_full

    units = [(res, nb) for res in range(residues) for nb in range(blocks)]
    kv = {res: keys_values(res) for res in range(residues)}
    s_next = scores(*units[0], kv[units[0][0]][0])
    for i, (res, nb) in enumerate(units):
        s = s_next
        if i + 1 < len(units):
            s_next = scores(*units[i + 1], kv[units[i + 1][0]][0])
        finish(res, nb, s, kv[res][1])


def _dilated_group(q, k, v, bias_band, dil):
    batch, sub_seq, _ = q.shape
    blocks = min(DILATED_BLOCKS_PER_STEP, sub_seq // BLOCK)
    residues = min(DILATED_BLOCKS_PER_STEP // blocks, dil)
    tile = blocks * BLOCK
    width = residues * GROUP_W
    cur = pl.BlockSpec((None, tile, width), lambda b, r, n: (b, n, r))
    prev = pl.BlockSpec((None, BLOCK, width),
                        lambda b, r, n: (b, jnp.maximum(n * blocks - 1, 0), r))
    return pl.pallas_call(
        functools.partial(_dilated_body, blocks=blocks, residues=residues),
        grid=(batch, dil // residues, sub_seq // tile),
        in_specs=[cur, cur, cur, prev, prev, _resident(bias_band.shape)],
        out_specs=[cur, cur],
        out_shape=[jax.ShapeDtypeStruct(q.shape, BF16), jax.ShapeDtypeStruct(q.shape, F32)],
        compiler_params=_compiler_params(("parallel", "parallel", "parallel")),
        name=f"dilated_d{dil}",
    )(q, k, v, k, v, bias_band)


def _bias_band(rel_bias, group, dil, n_back):
    heads = slice(group * HEADS_PER_GROUP, (group + 1) * HEADS_PER_GROUP)
    buckets = _t5_bucket(np.arange(n_back + 1) * dil)
    bias = rel_bias[buckets][:, heads].T.astype(F32) * LOG2E
    period = 2 * BLOCK + 1
    j = BLOCK - np.arange(period)
    local = (j >= 0) & (j <= n_back)
    diag = jnp.where(local[None], bias[:, np.clip(j, 0, n_back)], MASKED)
    band = jnp.tile(diag, (1, BLOCK))[:, :BLOCK * 2 * BLOCK]
    return band.reshape(HEADS_PER_GROUP * BLOCK, 2 * BLOCK)


class _StickStep(NamedTuple):
    q: jax.Array
    k: jax.Array
    v: jax.Array
    mask: Optional[jax.Array]
    carry: Any
    after: Optional[int]


def _neg_abs(x):
    bits = lax.bitcast_convert_type(x, jnp.uint32) | jnp.uint32(0x80000000)
    return lax.bitcast_convert_type(bits, F32)


def _stick_steps(steps, upper):
    z = [lax.dot_general(st.q, st.k, (((1,), (1,)), ((), ())), preferred_element_type=F32)
         for st in steps]
    z = [s if st.mask is None else jnp.where(st.mask, s, MASKED) for s, st in zip(z, steps)]
    soft = [jnp.maximum(s, 0.0) + jnp.log(1.0 + jnp.exp(_neg_abs(s))) for s in z]
    log_beta = [s - p for s, p in zip(z, soft)]
    tail = [jnp.dot(p.astype(BF16), upper, preferred_element_type=F32) for p in soft]
    sums = [jnp.sum(p, axis=-1, keepdims=True) for p in soft]
    outs = []
    for i, st in enumerate(steps):
        carry = st.carry if st.after is None else st.carry + sums[st.after]
        w = jnp.exp(log_beta[i] - (tail[i] + carry))
        outs.append(w.astype(BF16))
    outs = [jnp.dot(w, st.v, preferred_element_type=F32) for w, st in zip(outs, steps)]
    return outs, sums


def _stick_body(q_ref, k_ref, v_ref, upper_ref, o_ref, acc_ref, carry_ref):
    tile = upper_ref.shape[0]
    q_tiles = q_ref.shape[0] // tile
    heads = LANES // HEAD_DIM
    first_tile = pl.program_id(2) * q_tiles
    lane_head = lax.broadcasted_iota(jnp.int32, (tile, LANES), 1) // HEAD_DIM
    row = lax.broadcasted_iota(jnp.int32, (tile, tile), 0)
    col = lax.broadcasted_iota(jnp.int32, (tile, tile), 1)
    strictly_before = col < row
    upper = upper_ref[...]
    chains = []
    for t in range(q_tiles):
        q = q_ref[t * tile:(t + 1) * tile, :]
        for hd in range(heads):
            chains.append((t, hd, jnp.where(lane_head == hd, q, jnp.zeros_like(q))))

    def key_value(kt):
        start = pl.multiple_of(jnp.maximum(kt, 0) * tile, tile)
        v = v_ref[pl.ds(start, tile), :]
        return k_ref[pl.ds(start, tile), :], jnp.where(kt >= 0, v, jnp.zeros_like(v))

    diag = [key_value(first_tile + t) for t in range(q_tiles)]
    prev = [key_value(first_tile + t - 1) for t in range(q_tiles)]
    steps = []
    for t, hd, qh in chains:
        steps.append(_StickStep(qh, *diag[t], strictly_before, 0.0, None))
        steps.append(_StickStep(qh, *prev[t], None, 0.0, len(steps) - 1))
    outs, sums = _stick_steps(steps, upper)
    for c in range(len(chains)):
        acc_ref[c] = outs[2 * c] + outs[2 * c + 1]
        carry_ref[c] = sums[2 * c] + sums[2 * c + 1]

    def any_weight_left():
        return (jnp.min(carry_ref[...]) < F32_EXP_UNDERFLOW).astype(jnp.int32)

    def more_tiles(state):
        j, live = state
        return jnp.logical_and(j < first_tile + q_tiles - 1, live > 0)

    def earlier(state):
        j, _ = state
        tiles = [key_value(first_tile + t - 1 - j) for t in range(q_tiles)]
        outs, sums = _stick_steps(
            [_StickStep(qh, *tiles[t], None, carry_ref[c], None)
             for c, (t, _, qh) in enumerate(chains)], upper)
        for c in range(len(chains)):
            acc_ref[c] += outs[c]
            carry_ref[c] += sums[c]
        return j + 1, any_weight_left()

    lax.while_loop(more_tiles, earlier, (jnp.int32(1), any_weight_left()))
    for t in range(q_tiles):
        out = acc_ref[t * heads]
        for hd in range(1, heads):
            out = jnp.where(lane_head == hd, acc_ref[t * heads + hd], out)
        o_ref[t * tile:(t + 1) * tile, :] = out.astype(o_ref.dtype)


def _stick_breaking(q, k, v):
    batch, seq, width = q.shape
    tile = STICK_TILE
    rows = STICK_Q_TILES * tile
    chains = STICK_Q_TILES * (LANES // HEAD_DIM)
    j = np.arange(tile)
    upper = jnp.asarray((j[:, None] > j[None, :]), BF16)
    return pl.pallas_call(
        _stick_body,
        grid=(batch, width // LANES, seq // rows),
        in_specs=[pl.BlockSpec((None, rows, LANES), lambda b, p, i: (b, i, p)),
                  pl.BlockSpec((None, seq, LANES), lambda b, p, i: (b, 0, p)),
                  pl.BlockSpec((None, seq, LANES), lambda b, p, i: (b, 0, p)),
                  _resident((tile, tile))],
        out_specs=pl.BlockSpec((None, rows, LANES), lambda b, p, i: (b, i, p)),
        out_shape=jax.ShapeDtypeStruct(q.shape, BF16),
        scratch_shapes=[pltpu.VMEM((chains, tile, LANES), F32),
                        pltpu.VMEM((chains, tile, 1), F32)],
        compiler_params=_compiler_params(("parallel", "parallel", "parallel")),
        name="stick_breaking",
    )(q, k, v, upper)


def _mix_tile(x_ref, gates_ref, yb_ref, group_refs, wa_ref, wb_ref, wo_ref, g_ref, slab_ref):
    tile, d_model = x_ref.shape

    def natural_order(ref, dil):
        if dil == 1:
            return ref[...].astype(F32)
        rows = tile // dil
        for r in range(dil):
            for s in range(GROUP_W // LANES):
                lane0 = r * GROUP_W + s * LANES
                slab_ref[s, pl.ds(r, rows, stride=dil), :] = (
                    ref[:, lane0:lane0 + LANES].astype(F32))
        return jnp.concatenate([slab_ref[s] for s in range(GROUP_W // LANES)], axis=-1)

    def sigmoid(v):
        return 0.5 * jnp.tanh(0.5 * v) + 0.5

    branch_b = jnp.dot(yb_ref[...], wb_ref[...], preferred_element_type=F32)
    outs, lses = [], []
    for g, (_, dil) in enumerate(DILATED_GROUPS):
        outs.append(natural_order(group_refs[2 * g], dil))
        lses.append(natural_order(group_refs[2 * g + 1], dil))
    top = functools.reduce(jnp.maximum, lses)
    wts = [jnp.exp2(l - top) for l in lses]
    ya = sum(w * o for w, o in zip(wts, outs)) / sum(wts)

    branch_a = jnp.dot(ya.astype(BF16), wa_ref[...], preferred_element_type=F32)
    gate_a = gates_ref[:, :d_model].astype(F32)
    gate_b = gates_ref[:, d_model:].astype(F32)
    merged = sigmoid(gate_a) * branch_a + sigmoid(gate_b) * branch_b
    mixed = jnp.dot(merged.astype(BF16), wo_ref[...], preferred_element_type=F32)
    return x_ref[...] + _rms_norm(mixed, g_ref[...])


def _ffn_tile(x_mid, g_pre_ref, w_up_ref, cw_ref, cb_ref, w_down_ref, g_post_ref, out_ref,
              halo_ref, u_ref, act_ref, acc_ref, slab_ref):
    tile, d_model = x_mid.shape
    d_ff = w_down_ref.shape[0]
    seg = tile // SUBLANES
    n_slabs = d_model // LANES
    halo_rows = (CONV_WIDTH - 1) * SUBLANES

    @pl.when(pl.program_id(1) == 0)
    def _():
        halo_ref[...] = jnp.zeros_like(halo_ref)

    for c in range(n_slabs):
        for s in range(SUBLANES):
            slab_ref[c, pl.ds(s, seg, stride=SUBLANES), :] = (
                x_mid[s * seg:(s + 1) * seg, c * LANES:(c + 1) * LANES])
    x = jnp.concatenate([slab_ref[c] for c in range(n_slabs)], axis=-1)
    h = _rms_norm(x, g_pre_ref[...]).astype(BF16)
    sublane = lax.broadcasted_iota(jnp.int32, (SUBLANES, FF_CHUNK), 0)

    def up(c):
        return [jnp.dot(h, w_up_ref[:, col:col + FF_CHUNK], preferred_element_type=F32)
                for col in (c * FF_CHUNK, d_ff + c * FF_CHUNK)]

    def conv(u, col, slot):
        cols = slice(col, col + FF_CHUNK)
        last = u[tile - halo_rows:, :]
        for k in range(CONV_WIDTH - 1):
            group = slice(k * SUBLANES, (k + 1) * SUBLANES)
            mixed = jnp.where(sublane == SUBLANES - 1, halo_ref[group, cols], last[group, :])
            u_ref[slot, group, :] = pltpu.roll(mixed, 1, axis=0)
        u_ref[slot, halo_rows:, :] = u
        halo_ref[:, cols] = last
        y = cb_ref[:, cols]
        for t in range(CONV_WIDTH):
            start = t * SUBLANES
            y = y + cw_ref[t:t + 1, cols] * u_ref[slot, start:start + tile, :]
        return y

    n_chunks = d_ff // FF_CHUNK
    u_next = up(0)
    first = 0
    for c in range(n_chunks):
        u_gate, u_val = u_next
        if c + 1 < n_chunks:
            u_next = up(c + 1)
        gate = conv(u_gate, c * FF_CHUNK, 0)
        val = conv(u_val, d_ff + c * FF_CHUNK, 1)
        act_ref[:, c * FF_CHUNK:(c + 1) * FF_CHUNK] = (
            jax.nn.gelu(gate, approximate=True) * val).astype(BF16)
        if (c + 1) % DOWN_GROUP == 0 or c + 1 == n_chunks:
            rows = slice(first * FF_CHUNK, (c + 1) * FF_CHUNK)
            part = jnp.dot(act_ref[:, rows], w_down_ref[rows, :], preferred_element_type=F32)
            if first == 0:
                acc_ref[...] = part
            else:
                acc_ref[...] += part
            first = c + 1
    y = x + _rms_norm(acc_ref[...], g_post_ref[...])
    for c in range(n_slabs):
        slab_ref[c] = y[:, c * LANES:(c + 1) * LANES]
    for c in range(n_slabs):
        for s in range(SUBLANES):
            out_ref[s * seg:(s + 1) * seg, c * LANES:(c + 1) * LANES] = (
                slab_ref[c, pl.ds(s, seg, stride=SUBLANES), :])


def _post_attention_body(x_ref, gates_ref, yb_ref, *refs):
    group_refs = refs[:2 * N_GROUPS]
    (wa_ref, wb_ref, wo_ref, g_post_mix_ref, g_pre_ref, w_up_ref, cw_ref, cb_ref, w_down_ref,
     g_post_ref, out_ref, group_slab_ref, halo_ref, u_ref, act_ref, acc_ref,
     slab_ref) = refs[2 * N_GROUPS:]
    x_mid = _mix_tile(x_ref, gates_ref, yb_ref, group_refs, wa_ref, wb_ref, wo_ref,
                      g_post_mix_ref, group_slab_ref)
    _ffn_tile(x_mid, g_pre_ref, w_up_ref, cw_ref, cb_ref, w_down_ref, g_post_ref, out_ref,
              halo_ref, u_ref, act_ref, acc_ref, slab_ref)


def _post_attention(x, gates, yb, groups, mix_consts, ffn_consts):
    batch, seq, d_model = x.shape
    tile = TOKEN_TILE
    d_ff = ffn_consts[4].shape[0]
    row_spec = lambda width: pl.BlockSpec((None, tile, width), lambda b, i: (b, i, 0))
    in_specs = [row_spec(d_model), row_spec(gates.shape[-1]), row_spec(yb.shape[-1])]
    operands = [x, gates, yb]
    for (o, lse), (_, dil) in zip(groups, DILATED_GROUPS):
        for arr in (o, lse):
            in_specs.append(pl.BlockSpec((None, tile // dil, dil * GROUP_W), lambda b, i: (b, i, 0)))
            operands.append(arr)
    for arr in (*mix_consts, *ffn_consts):
        in_specs.append(_resident(arr.shape))
        operands.append(arr)
    halo_rows = (CONV_WIDTH - 1) * SUBLANES
    return pl.pallas_call(
        _post_attention_body,
        grid=(batch, seq // tile),
        in_specs=in_specs,
        out_specs=row_spec(d_model),
        out_shape=jax.ShapeDtypeStruct(x.shape, F32),
        scratch_shapes=[pltpu.VMEM((GROUP_W // LANES, tile, LANES), F32),
                        pltpu.VMEM((halo_rows, 2 * d_ff), F32),
                        pltpu.VMEM((2, halo_rows + tile, FF_CHUNK), F32),
                        pltpu.VMEM((tile, d_ff), BF16),
                        pltpu.VMEM((tile, d_model), F32),
                        pltpu.VMEM((d_model // LANES, tile, LANES), F32)],
        compiler_params=_compiler_params(("arbitrary", "arbitrary")),
        name="post_attention",
    )(*operands)


def kernel(x, g_pre_mix, w_in, rel_bias, w_branch_a, w_branch_b, w_out, g_post_mix, g_pre_ffn,
           w_up, conv_w, conv_b, w_down, g_post_ffn):
    depth = w_in.shape[0]
    for l in range(depth):
        proj = _in_proj(x, g_pre_mix[l][None], w_in[l].astype(BF16))
        qb, kb, vb, gates = proj[3 * N_GROUPS:]
        groups = []
        for g, (window, dil) in enumerate(DILATED_GROUPS):
            q, k, v = proj[3 * g:3 * g + 3]
            band = _bias_band(rel_bias, g, dil, window // dil)
            groups.append(_dilated_group(q, k, v, band, dil))
        yb = _stick_breaking(qb, kb, vb)
        mix_consts = (w_branch_a[l].astype(BF16), w_branch_b[l].astype(BF16),
                      w_out[l].astype(BF16), g_post_mix[l][None])
        ffn_consts = (g_pre_ffn[l][None], w_up[l].astype(BF16), conv_w[l], conv_b[l][None],
                      w_down[l].astype(BF16), g_post_ffn[l][None])
        x = _post_attention(x, gates, yb, groups, mix_consts, ffn_consts)
    return x
```

```python
import functools
import math
from typing import Any, NamedTuple, Optional

import numpy as np
import jax
import jax.numpy as jnp
from jax import lax
from jax.experimental import pallas as pl
from jax.experimental.pallas import tpu as pltpu

F32 = jnp.float32
BF16 = jnp.bfloat16

HEAD_DIM = 64
DILATED_GROUPS = ((128, 1), (512, 4), (2048, 16))
HEADS_PER_GROUP = 4
N_GROUPS = len(DILATED_GROUPS)
N_HEADS_A = HEADS_PER_GROUP * N_GROUPS
N_HEADS_B = 8
GROUP_W = HEADS_PER_GROUP * HEAD_DIM
A_QKV = N_HEADS_A * HEAD_DIM
B_QKV = N_HEADS_B * HEAD_DIM
BLOCK = 128
NUM_BUCKETS = 32
MAX_DISTANCE = 2048
CONV_WIDTH = 3
EPS = 1e-6
QK_SCALE = 1.0 / math.sqrt(HEAD_DIM)
LOG2E = math.log2(math.e)
MASKED = -1e30
F32_EXP2_UNDERFLOW = 150.1

LANES = 128
SUBLANES = 8
VMEM_LIMIT_BYTES = 56 * 1024 * 1024

TOKEN_TILE = 512
IN_PROJ_TILE = 1024
GATE_CHUNK = 512
DILATED_BLOCKS_PER_STEP = 8
STICK_Q_TILES = 8
STICK_TILE = 256
FF_CHUNK = 256
DOWN_GROUP = 3


def _t5_bucket(dist):
    max_exact = NUM_BUCKETS // 2
    n = np.asarray(dist, dtype=np.float32)
    large = max_exact + (np.log(np.maximum(n, 1.0) / max_exact)
                         / np.log(MAX_DISTANCE / max_exact)
                         * (NUM_BUCKETS - max_exact)).astype(np.int32)
    large = np.minimum(large, NUM_BUCKETS - 1)
    return np.where(n < max_exact, n.astype(np.int32), large).astype(np.int32)


def _rms_norm(x, g):
    return x * lax.rsqrt(jnp.mean(x * x, axis=-1, keepdims=True) + EPS) * g


def _compiler_params(semantics):
    return pltpu.CompilerParams(dimension_semantics=semantics, vmem_limit_bytes=VMEM_LIMIT_BYTES)


def _resident(shape):
    return pl.BlockSpec(shape, lambda *_: (0,) * len(shape), pipeline_mode=pl.Buffered(1))


def _in_proj_body(x_ref, g_ref, w_ref, *refs):
    a_refs = refs[:3 * N_GROUPS]
    qb_ref, kb_ref, vb_ref, gates_ref, slab_ref = refs[3 * N_GROUPS:]
    tile = x_ref.shape[0]
    h = _rms_norm(x_ref[...], g_ref[...]).astype(BF16)

    def proj(col, width):
        return jnp.dot(h, w_ref[:, col:col + width], preferred_element_type=F32)

    for t in range(3):
        scale = QK_SCALE * LOG2E if t == 0 else 1.0
        for g, (_, dil) in enumerate(DILATED_GROUPS):
            acc = proj(t * A_QKV + g * GROUP_W, GROUP_W) * scale
            out_ref = a_refs[3 * g + t]
            if dil == 1:
                out_ref[...] = acc.astype(BF16)
                continue
            rows = tile // dil
            for s in range(GROUP_W // LANES):
                slab_ref[s] = acc[:, s * LANES:(s + 1) * LANES]
            for r in range(dil):
                for s in range(GROUP_W // LANES):
                    lane0 = r * GROUP_W + s * LANES
                    out_ref[:, lane0:lane0 + LANES] = (
                        slab_ref[s, pl.ds(r, rows, stride=dil), :].astype(BF16))

    base = 3 * A_QKV
    qb_ref[...] = (proj(base, B_QKV) * (QK_SCALE * LOG2E)).astype(BF16)
    kb_ref[...] = proj(base + B_QKV, B_QKV).astype(BF16)
    vb_ref[...] = proj(base + 2 * B_QKV, B_QKV).astype(BF16)
    base += 3 * B_QKV
    gate_w = gates_ref.shape[1]
    for c in range(0, gate_w, GATE_CHUNK):
        gates_ref[:, c:c + GATE_CHUNK] = proj(base + c, GATE_CHUNK).astype(BF16)


def _in_proj(x, g, w):
    batch, seq, d_model = x.shape
    tile = IN_PROJ_TILE
    n_in = w.shape[1]
    gate_w = n_in - 3 * A_QKV - 3 * B_QKV
    out_shape, out_specs = [], []
    for _, dil in DILATED_GROUPS:
        for _ in range(3):
            out_shape.append(jax.ShapeDtypeStruct((batch, seq // dil, dil * GROUP_W), BF16))
            out_specs.append(pl.BlockSpec((None, tile // dil, dil * GROUP_W), lambda b, i: (b, i, 0)))
    for _ in range(3):
        out_shape.append(jax.ShapeDtypeStruct((batch, seq, B_QKV), BF16))
        out_specs.append(pl.BlockSpec((None, tile, B_QKV), lambda b, i: (b, i, 0)))
    out_shape.append(jax.ShapeDtypeStruct((batch, seq, gate_w), BF16))
    out_specs.append(pl.BlockSpec((None, tile, gate_w), lambda b, i: (b, i, 0)))
    return pl.pallas_call(
        _in_proj_body,
        grid=(batch, seq // tile),
        in_specs=[pl.BlockSpec((None, tile, d_model), lambda b, i: (b, i, 0)),
                  _resident((1, d_model)),
                  _resident((d_model, n_in))],
        out_specs=out_specs,
        out_shape=out_shape,
        scratch_shapes=[pltpu.VMEM((GROUP_W // LANES, tile, LANES), F32)],
        compiler_params=_compiler_params(("parallel", "parallel")),
        name="in_proj",
    )(x, g, w)


def _dilated_body(q_ref, k_ref, v_ref, kp_ref, vp_ref, bias_ref, o_ref, lse_ref, *,
                  blocks, residues):
    heads = HEADS_PER_GROUP
    missing_cols = jnp.where(pl.program_id(2) == 0, BLOCK, 0)
    lane_head = lax.broadcasted_iota(jnp.int32, (BLOCK, GROUP_W), 1) // HEAD_DIM
    key_col = lax.broadcasted_iota(jnp.int32, (heads * BLOCK, 2 * BLOCK), 1)
    bias = bias_ref[...]

    def keys_values(res):
        lanes = slice(res * GROUP_W, (res + 1) * GROUP_W)
        k_all = jnp.concatenate([kp_ref[:, lanes], k_ref[:, lanes]], axis=0)
        v_all = jnp.concatenate([vp_ref[:, lanes], v_ref[:, lanes]], axis=0)
        v_lane_head = lax.broadcasted_iota(jnp.int32, v_all.shape, 1) // HEAD_DIM
        v_heads = [jnp.where(v_lane_head == hd, v_all, jnp.zeros_like(v_all))
                   for hd in range(heads)]
        return k_all, v_heads

    def scores(res, nb, k_all):
        q = q_ref[nb * BLOCK:(nb + 1) * BLOCK, res * GROUP_W:(res + 1) * GROUP_W]
        q_stack = jnp.concatenate(
            [jnp.where(lane_head == hd, q, jnp.zeros_like(q)) for hd in range(heads)], axis=0)
        s = lax.dot_general(q_stack, k_all[nb * BLOCK:(nb + 2) * BLOCK], (((1,), (1,)), ((), ())),
                            preferred_element_type=F32) + bias
        if nb == 0:
            s = jnp.where(key_col < missing_cols, MASKED, s)
        return s

    def finish(res, nb, s, v_heads):
        mx = jnp.max(s, axis=-1, keepdims=True)
        e = jnp.exp2(s - mx)
        den = jnp.sum(e, axis=-1, keepdims=True)
        p = (e * (1.0 / den)).astype(BF16)
        p_cat = jnp.concatenate([p[hd * BLOCK:(hd + 1) * BLOCK] for hd in range(heads)], axis=1)
        v_stack = jnp.concatenate([vh[nb * BLOCK:(nb + 2) * BLOCK] for vh in v_heads], axis=0)
        o = jnp.dot(p_cat, v_stack, preferred_element_type=F32)
        lse = mx + jnp.log2(den)
        lse_full = jnp.broadcast_to(lse[:BLOCK], (BLOCK, GROUP_W))
        for hd in range(1, heads):
            lse_full = jnp.where(lane_head == hd, lse[hd * BLOCK:(hd + 1) * BLOCK], lse_full)
        rows = slice(nb * BLOCK, (nb + 1) * BLOCK)
        lanes = slice(res * GROUP_W, (res + 1) * GROUP_W)
        o_ref[rows, lanes] = o.astype(o_ref.dtype)
        lse_ref[rows, lanes] = lse_full

    units = [(res, nb) for res in range(residues) for nb in range(blocks)]
    kv = {res: keys_values(res) for res in range(residues)}
    s_next = scores(*units[0], kv[units[0][0]][0])
    for i, (res, nb) in enumerate(units):
        s = s_next
        if i + 1 < len(units):
            s_next = scores(*units[i + 1], kv[units[i + 1][0]][0])
        finish(res, nb, s, kv[res][1])


def _dilated_group(q, k, v, bias_band, dil):
    batch, sub_seq, _ = q.shape
    blocks = min(DILATED_BLOCKS_PER_STEP, sub_seq // BLOCK)
    residues = min(DILATED_BLOCKS_PER_STEP // blocks, dil)
    tile = blocks * BLOCK
    width = residues * GROUP_W
    cur = pl.BlockSpec((None, tile, width), lambda b, r, n: (b, n, r))
    prev = pl.BlockSpec((None, BLOCK, width),
                        lambda b, r, n: (b, jnp.maximum(n * blocks - 1, 0), r))
    return pl.pallas_call(
        functools.partial(_dilated_body, blocks=blocks, residues=residues),
        grid=(batch, dil // residues, sub_seq // tile),
        in_specs=[cur, cur, cur, prev, prev, _resident(bias_band.shape)],
        out_specs=[cur, cur],
        out_shape=[jax.ShapeDtypeStruct(q.shape, BF16), jax.ShapeDtypeStruct(q.shape, F32)],
        compiler_params=_compiler_params(("parallel", "parallel", "parallel")),
        name=f"dilated_d{dil}",
    )(q, k, v, k, v, bias_band)


def _bias_band(rel_bias, group, dil, n_back):
    heads = slice(group * HEADS_PER_GROUP, (group + 1) * HEADS_PER_GROUP)
    buckets = _t5_bucket(np.arange(n_back + 1) * dil)
    bias = rel_bias[buckets][:, heads].T.astype(F32) * LOG2E
    period = 2 * BLOCK + 1
    j = BLOCK - np.arange(period)
    local = (j >= 0) & (j <= n_back)
    diag = jnp.where(local[None], bias[:, np.clip(j, 0, n_back)], MASKED)
    band = jnp.tile(diag, (1, BLOCK))[:, :BLOCK * 2 * BLOCK]
    return band.reshape(HEADS_PER_GROUP * BLOCK, 2 * BLOCK)


class _StickStep(NamedTuple):
    q: jax.Array
    k: jax.Array
    v: jax.Array
    mask: Optional[jax.Array]
    carry: Any
    after: Optional[int]


def _neg_abs(x):
    bits = lax.bitcast_convert_type(x, jnp.uint32) | jnp.uint32(0x80000000)
    return lax.bitcast_convert_type(bits, F32)


def _stick_steps(steps, upper):
    z = [lax.dot_general(st.q, st.k, (((1,), (1,)), ((), ())), preferred_element_type=F32)
         for st in steps]
    z = [s if st.mask is None else jnp.where(st.mask, s, MASKED) for s, st in zip(z, steps)]
    soft = [jnp.maximum(s, 0.0) + jnp.log2(1.0 + jnp.exp2(_neg_abs(s))) for s in z]
    log_beta = [s - p for s, p in zip(z, soft)]
    tail = [jnp.dot(p.astype(BF16), upper, preferred_element_type=F32) for p in soft]
    sums = [jnp.sum(p, axis=-1, keepdims=True) for p in soft]
    outs = []
    for i, st in enumerate(steps):
        carry = st.carry if st.after is None else st.carry + sums[st.after]
        w = jnp.exp2(log_beta[i] - (tail[i] + carry))
        outs.append(w.astype(BF16))
    outs = [jnp.dot(w, st.v, preferred_element_type=F32) for w, st in zip(outs, steps)]
    return outs, sums


def _stick_body(q_ref, k_ref, v_ref, upper_ref, o_ref, acc_ref, carry_ref):
    tile = upper_ref.shape[0]
    q_tiles = q_ref.shape[0] // tile
    heads = LANES // HEAD_DIM
    first_tile = pl.program_id(2) * q_tiles
    lane_head = lax.broadcasted_iota(jnp.int32, (tile, LANES), 1) // HEAD_DIM
    row = lax.broadcasted_iota(jnp.int32, (tile, tile), 0)
    col = lax.broadcasted_iota(jnp.int32, (tile, tile), 1)
    strictly_before = col < row
    upper = upper_ref[...]
    chains = []
    for t in range(q_tiles):
        q = q_ref[t * tile:(t + 1) * tile, :]
        for hd in range(heads):
            chains.append((t, hd, jnp.where(lane_head == hd, q, jnp.zeros_like(q))))

    def key_value(kt):
        start = pl.multiple_of(jnp.maximum(kt, 0) * tile, tile)
        v = v_ref[pl.ds(start, tile), :]
        return k_ref[pl.ds(start, tile), :], jnp.where(kt >= 0, v, jnp.zeros_like(v))

    diag = [key_value(first_tile + t) for t in range(q_tiles)]
    prev = [key_value(first_tile + t - 1) for t in range(q_tiles)]
    steps = []
    for t, hd, qh in chains:
        steps.append(_StickStep(qh, *diag[t], strictly_before, 0.0, None))
        steps.append(_StickStep(qh, *prev[t], None, 0.0, len(steps) - 1))
    outs, sums = _stick_steps(steps, upper)
    for c in range(len(chains)):
        acc_ref[c] = outs[2 * c] + outs[2 * c + 1]
        carry_ref[c] = sums[2 * c] + sums[2 * c + 1]

    def any_weight_left():
        return (jnp.min(carry_ref[...]) < F32_EXP2_UNDERFLOW).astype(jnp.int32)

    def more_tiles(state):
        j, live = state
        return jnp.logical_and(j < first_tile + q_tiles - 1, live > 0)

    def earlier(state):
        j, _ = state
        tiles = [key_value(first_tile + t - 1 - j) for t in range(q_tiles)]
        outs, sums = _stick_steps(
            [_StickStep(qh, *tiles[t], None, carry_ref[c], None)
             for c, (t, _, qh) in enumerate(chains)], upper)
        for c in range(len(chains)):
            acc_ref[c] += outs[c]
            carry_ref[c] += sums[c]
        return j + 1, any_weight_left()

    lax.while_loop(more_tiles, earlier, (jnp.int32(1), any_weight_left()))
    for t in range(q_tiles):
        out = acc_ref[t * heads]
        for hd in range(1, heads):
            out = jnp.where(lane_head == hd, acc_ref[t * heads + hd], out)
        o_ref[t * tile:(t + 1) * tile, :] = out.astype(o_ref.dtype)


def _stick_breaking(q, k, v):
    batch, seq, width = q.shape
    tile = STICK_TILE
    rows = STICK_Q_TILES * tile
    chains = STICK_Q_TILES * (LANES // HEAD_DIM)
    j = np.arange(tile)
    upper = jnp.asarray((j[:, None] > j[None, :]), BF16)
    return pl.pallas_call(
        _stick_body,
        grid=(batch, width // LANES, seq // rows),
        in_specs=[pl.BlockSpec((None, rows, LANES), lambda b, p, i: (b, i, p)),
                  pl.BlockSpec((None, seq, LANES), lambda b, p, i: (b, 0, p)),
                  pl.BlockSpec((None, seq, LANES), lambda b, p, i: (b, 0, p)),
                  _resident((tile, tile))],
        out_specs=pl.BlockSpec((None, rows, LANES), lambda b, p, i: (b, i, p)),
        out_shape=jax.ShapeDtypeStruct(q.shape, BF16),
        scratch_shapes=[pltpu.VMEM((chains, tile, LANES), F32),
                        pltpu.VMEM((chains, tile, 1), F32)],
        compiler_params=_compiler_params(("parallel", "parallel", "parallel")),
        name="stick_breaking",
    )(q, k, v, upper)


def _mix_tile(x_ref, gates_ref, yb_ref, group_refs, wa_ref, wb_ref, wo_ref, g_ref, slab_ref):
    tile, d_model = x_ref.shape

    def natural_order(ref, dil):
        if dil == 1:
            return ref[...].astype(F32)
        rows = tile // dil
        for r in range(dil):
            for s in range(GROUP_W // LANES):
                lane0 = r * GROUP_W + s * LANES
                slab_ref[s, pl.ds(r, rows, stride=dil), :] = (
                    ref[:, lane0:lane0 + LANES].astype(F32))
        return jnp.concatenate([slab_ref[s] for s in range(GROUP_W // LANES)], axis=-1)

    def sigmoid(v):
        return 0.5 * jnp.tanh(0.5 * v) + 0.5

    branch_b = jnp.dot(yb_ref[...], wb_ref[...], preferred_element_type=F32)
    outs, lses = [], []
    for g, (_, dil) in enumerate(DILATED_GROUPS):
        outs.append(natural_order(group_refs[2 * g], dil))
        lses.append(natural_order(group_refs[2 * g + 1], dil))
    top = functools.reduce(jnp.maximum, lses)
    wts = [jnp.exp2(l - top) for l in lses]
    ya = sum(w * o for w, o in zip(wts, outs)) / sum(wts)

    branch_a = jnp.dot(ya.astype(BF16), wa_ref[...], preferred_element_type=F32)
    gate_a = gates_ref[:, :d_model].astype(F32)
    gate_b = gates_ref[:, d_model:].astype(F32)
    merged = sigmoid(gate_a) * branch_a + sigmoid(gate_b) * branch_b
    mixed = jnp.dot(merged.astype(BF16), wo_ref[...], preferred_element_type=F32)
    return x_ref[...] + _rms_norm(mixed, g_ref[...])


def _ffn_tile(x_mid, g_pre_ref, w_up_ref, cw_ref, cb_ref, w_down_ref, g_post_ref, out_ref,
              halo_ref, u_ref, act_ref, acc_ref, slab_ref):
    tile, d_model = x_mid.shape
    d_ff = w_down_ref.shape[0]
    seg = tile // SUBLANES
    n_slabs = d_model // LANES
    halo_rows = (CONV_WIDTH - 1) * SUBLANES

    @pl.when(pl.program_id(1) == 0)
    def _():
        halo_ref[...] = jnp.zeros_like(halo_ref)

    for c in range(n_slabs):
        for s in range(SUBLANES):
            slab_ref[c, pl.ds(s, seg, stride=SUBLANES), :] = (
                x_mid[s * seg:(s + 1) * seg, c * LANES:(c + 1) * LANES])
    x = jnp.concatenate([slab_ref[c] for c in range(n_slabs)], axis=-1)
    h = _rms_norm(x, g_pre_ref[...]).astype(BF16)
    sublane = lax.broadcasted_iota(jnp.int32, (SUBLANES, FF_CHUNK), 0)

    def up(c):
        return [jnp.dot(h, w_up_ref[:, col:col + FF_CHUNK], preferred_element_type=F32)
                for col in (c * FF_CHUNK, d_ff + c * FF_CHUNK)]

    def conv(u, col, slot):
        cols = slice(col, col + FF_CHUNK)
        last = u[tile - halo_rows:, :]
        for k in range(CONV_WIDTH - 1):
            group = slice(k * SUBLANES, (k + 1) * SUBLANES)
            mixed = jnp.where(sublane == SUBLANES - 1, halo_ref[group, cols], last[group, :])
            u_ref[slot, group, :] = pltpu.roll(mixed, 1, axis=0)
        u_ref[slot, halo_rows:, :] = u
        halo_ref[:, cols] = last
        y = cb_ref[:, cols]
        for t in range(CONV_WIDTH):
            start = t * SUBLANES
            y = y + cw_ref[t:t + 1, cols] * u_ref[slot, start:start + tile, :]
        return y

    n_chunks = d_ff // FF_CHUNK
    u_next = up(0)
    first = 0
    for c in range(n_chunks):
        u_gate, u_val = u_next
        if c + 1 < n_chunks:
            u_next = up(c + 1)
        gate = conv(u_gate, c * FF_CHUNK, 0)
        val = conv(u_val, d_ff + c * FF_CHUNK, 1)
        act_ref[:, c * FF_CHUNK:(c + 1) * FF_CHUNK] = (
            jax.nn.gelu(gate, approximate=True) * val).astype(BF16)
        if (c + 1) % DOWN_GROUP == 0 or c + 1 == n_chunks:
            rows = slice(first * FF_CHUNK, (c + 1) * FF_CHUNK)
            part = jnp.dot(act_ref[:, rows], w_down_ref[rows, :], preferred_element_type=F32)
            if first == 0:
                acc_ref[...] = part
            else:
                acc_ref[...] += part
            first = c + 1
    y = x + _rms_norm(acc_ref[...], g_post_ref[...])
    for c in range(n_slabs):
        slab_ref[c] = y[:, c * LANES:(c + 1) * LANES]
    for c in range(n_slabs):
        for s in range(SUBLANES):
            out_ref[s * seg:(s + 1) * seg, c * LANES:(c + 1) * LANES] = (
                slab_ref[c, pl.ds(s, seg, stride=SUBLANES), :])


def _post_attention_body(x_ref, gates_ref, yb_ref, *refs):
    group_refs = refs[:2 * N_GROUPS]
    (wa_ref, wb_ref, wo_ref, g_post_mix_ref, g_pre_ref, w_up_ref, cw_ref, cb_ref, w_down_ref,
     g_post_ref, out_ref, group_slab_ref, halo_ref, u_ref, act_ref, acc_ref,
     slab_ref) = refs[2 * N_GROUPS:]
    x_mid = _mix_tile(x_ref, gates_ref, yb_ref, group_refs, wa_ref, wb_ref, wo_ref,
                      g_post_mix_ref, group_slab_ref)
    _ffn_tile(x_mid, g_pre_ref, w_up_ref, cw_ref, cb_ref, w_down_ref, g_post_ref, out_ref,
              halo_ref, u_ref, act_ref, acc_ref, slab_ref)


def _post_attention(x, gates, yb, groups, mix_consts, ffn_consts):
    batch, seq, d_model = x.shape
    tile = TOKEN_TILE
    d_ff = ffn_consts[4].shape[0]
    row_spec = lambda width: pl.BlockSpec((None, tile, width), lambda b, i: (b, i, 0))
    in_specs = [row_spec(d_model), row_spec(gates.shape[-1]), row_spec(yb.shape[-1])]
    operands = [x, gates, yb]
    for (o, lse), (_, dil) in zip(groups, DILATED_GROUPS):
        for arr in (o, lse):
            in_specs.append(pl.BlockSpec((None, tile // dil, dil * GROUP_W), lambda b, i: (b, i, 0)))
            operands.append(arr)
    for arr in (*mix_consts, *ffn_consts):
        in_specs.append(_resident(arr.shape))
        operands.append(arr)
    halo_rows = (CONV_WIDTH - 1) * SUBLANES
    return pl.pallas_call(
        _post_attention_body,
        grid=(batch, seq // tile),
        in_specs=in_specs,
        out_specs=row_spec(d_model),
        out_shape=jax.ShapeDtypeStruct(x.shape, F32),
        scratch_shapes=[pltpu.VMEM((GROUP_W // LANES, tile, LANES), F32),
                        pltpu.VMEM((halo_rows, 2 * d_ff), F32),
                        pltpu.VMEM((2, halo_rows + tile, FF_CHUNK), F32),
                        pltpu.VMEM((tile, d_ff), BF16),
                        pltpu.VMEM((tile, d_model), F32),
                        pltpu.VMEM((d_model // LANES, tile, LANES), F32)],
        compiler_params=_compiler_params(("arbitrary", "arbitrary")),
        name="post_attention",
    )(*operands)


def kernel(x, g_pre_mix, w_in, rel_bias, w_branch_a, w_branch_b, w_out, g_post_mix, g_pre_ffn,
           w_up, conv_w, conv_b, w_down, g_post_ffn):
    depth = w_in.shape[0]
    for l in range(depth):
        proj = _in_proj(x, g_pre_mix[l][None], w_in[l].astype(BF16))
        qb, kb, vb, gates = proj[3 * N_GROUPS:]
        groups = []
        for g, (window, dil) in enumerate(DILATED_GROUPS):
            q, k, v = proj[3 * g:3 * g + 3]
            band = _bias_band(rel_bias, g, dil, window // dil)
            groups.append(_dilated_group(q, k, v, band, dil))
        yb = _stick_breaking(qb, kb, vb)
        mix_consts = (w_branch_a[l].astype(BF16), w_branch_b[l].astype(BF16),
                      w_out[l].astype(BF16), g_post_mix[l][None])
        ffn_consts = (g_pre_ffn[l][None], w_up[l].astype(BF16), conv_w[l], conv_b[l][None],
                      w_down[l].astype(BF16), g_post_ffn[l][None])
        x = _post_attention(x, gates, yb, groups, mix_consts, ffn_consts)
    return x
```

```python
import functools
import math
from typing import Any, NamedTuple, Optional

import numpy as np
import jax
import jax.numpy as jnp
from jax import lax
from jax.experimental import pallas as pl
from jax.experimental.pallas import tpu as pltpu

F32 = jnp.float32
BF16 = jnp.bfloat16

HEAD_DIM = 64
DILATED_GROUPS = ((128, 1), (512, 4), (2048, 16))
HEADS_PER_GROUP = 4
N_GROUPS = len(DILATED_GROUPS)
N_HEADS_A = HEADS_PER_GROUP * N_GROUPS
N_HEADS_B = 8
GROUP_W = HEADS_PER_GROUP * HEAD_DIM
A_QKV = N_HEADS_A * HEAD_DIM
B_QKV = N_HEADS_B * HEAD_DIM
BLOCK = 128
NUM_BUCKETS = 32
MAX_DISTANCE = 2048
CONV_WIDTH = 3
EPS = 1e-6
QK_SCALE = 1.0 / math.sqrt(HEAD_DIM)
LOG2E = math.log2(math.e)
MASKED = -1e30
F32_EXP2_UNDERFLOW = 150.1

LANES = 128
SUBLANES = 8
VMEM_LIMIT_BYTES = 56 * 1024 * 1024

TOKEN_TILE = 512
IN_PROJ_TILE = 1024
GATE_CHUNK = 512
DILATED_BLOCKS_PER_STEP = 8
STICK_Q_TILES = 8
STICK_TILE = 256
FF_CHUNK = 256
DOWN_PARTS = 2


def _t5_bucket(dist):
    max_exact = NUM_BUCKETS // 2
    n = np.asarray(dist, dtype=np.float32)
    large = max_exact + (np.log(np.maximum(n, 1.0) / max_exact)
                         / np.log(MAX_DISTANCE / max_exact)
                         * (NUM_BUCKETS - max_exact)).astype(np.int32)
    large = np.minimum(large, NUM_BUCKETS - 1)
    return np.where(n < max_exact, n.astype(np.int32), large).astype(np.int32)


def _rms_norm(x, g):
    return x * lax.rsqrt(jnp.mean(x * x, axis=-1, keepdims=True) + EPS) * g


def _compiler_params(semantics):
    return pltpu.CompilerParams(dimension_semantics=semantics, vmem_limit_bytes=VMEM_LIMIT_BYTES)


def _resident(shape):
    return pl.BlockSpec(shape, lambda *_: (0,) * len(shape), pipeline_mode=pl.Buffered(1))


def _in_proj_body(x_ref, g_ref, w_ref, *refs):
    a_refs = refs[:3 * N_GROUPS]
    qb_ref, kb_ref, vb_ref, gates_ref, slab_ref = refs[3 * N_GROUPS:]
    tile = x_ref.shape[0]
    h = _rms_norm(x_ref[...], g_ref[...]).astype(BF16)

    def proj(col, width):
        return jnp.dot(h, w_ref[:, col:col + width], preferred_element_type=F32)

    for t in range(3):
        scale = QK_SCALE * LOG2E if t == 0 else 1.0
        for g, (_, dil) in enumerate(DILATED_GROUPS):
            acc = proj(t * A_QKV + g * GROUP_W, GROUP_W) * scale
            out_ref = a_refs[3 * g + t]
            if dil == 1:
                out_ref[...] = acc.astype(BF16)
                continue
            rows = tile // dil
            for s in range(GROUP_W // LANES):
                slab_ref[s] = acc[:, s * LANES:(s + 1) * LANES]
            for r in range(dil):
                for s in range(GROUP_W // LANES):
                    lane0 = r * GROUP_W + s * LANES
                    out_ref[:, lane0:lane0 + LANES] = (
                        slab_ref[s, pl.ds(r, rows, stride=dil), :].astype(BF16))

    base = 3 * A_QKV
    qb_ref[...] = (proj(base, B_QKV) * (QK_SCALE * LOG2E)).astype(BF16)
    kb_ref[...] = proj(base + B_QKV, B_QKV).astype(BF16)
    vb_ref[...] = proj(base + 2 * B_QKV, B_QKV).astype(BF16)
    base += 3 * B_QKV
    gate_w = gates_ref.shape[1]
    for c in range(0, gate_w, GATE_CHUNK):
        gates_ref[:, c:c + GATE_CHUNK] = proj(base + c, GATE_CHUNK).astype(BF16)


def _in_proj(x, g, w):
    batch, seq, d_model = x.shape
    tile = IN_PROJ_TILE
    n_in = w.shape[1]
    gate_w = n_in - 3 * A_QKV - 3 * B_QKV
    out_shape, out_specs = [], []
    for _, dil in DILATED_GROUPS:
        for _ in range(3):
            out_shape.append(jax.ShapeDtypeStruct((batch, seq // dil, dil * GROUP_W), BF16))
            out_specs.append(pl.BlockSpec((None, tile // dil, dil * GROUP_W), lambda b, i: (b, i, 0)))
    for _ in range(3):
        out_shape.append(jax.ShapeDtypeStruct((batch, seq, B_QKV), BF16))
        out_specs.append(pl.BlockSpec((None, tile, B_QKV), lambda b, i: (b, i, 0)))
    out_shape.append(jax.ShapeDtypeStruct((batch, seq, gate_w), BF16))
    out_specs.append(pl.BlockSpec((None, tile, gate_w), lambda b, i: (b, i, 0)))
    return pl.pallas_call(
        _in_proj_body,
        grid=(batch, seq // tile),
        in_specs=[pl.BlockSpec((None, tile, d_model), lambda b, i: (b, i, 0)),
                  _resident((1, d_model)),
                  _resident((d_model, n_in))],
        out_specs=out_specs,
        out_shape=out_shape,
        scratch_shapes=[pltpu.VMEM((GROUP_W // LANES, tile, LANES), F32)],
        compiler_params=_compiler_params(("parallel", "parallel")),
        name="in_proj",
    )(x, g, w)


def _dilated_body(q_ref, k_ref, v_ref, kp_ref, vp_ref, bias_ref, o_ref, lse_ref, *,
                  blocks, residues):
    heads = HEADS_PER_GROUP
    missing_cols = jnp.where(pl.program_id(2) == 0, BLOCK, 0)
    lane_head = lax.broadcasted_iota(jnp.int32, (BLOCK, GROUP_W), 1) // HEAD_DIM
    key_col = lax.broadcasted_iota(jnp.int32, (heads * BLOCK, 2 * BLOCK), 1)
    bias = bias_ref[...]

    def keys_values(res):
        lanes = slice(res * GROUP_W, (res + 1) * GROUP_W)
        k_all = jnp.concatenate([kp_ref[:, lanes], k_ref[:, lanes]], axis=0)
        v_all = jnp.concatenate([vp_ref[:, lanes], v_ref[:, lanes]], axis=0)
        v_lane_head = lax.broadcasted_iota(jnp.int32, v_all.shape, 1) // HEAD_DIM
        v_heads = [jnp.where(v_lane_head == hd, v_all, jnp.zeros_like(v_all))
                   for hd in range(heads)]
        return k_all, v_heads

    def scores(res, nb, k_all):
        q = q_ref[nb * BLOCK:(nb + 1) * BLOCK, res * GROUP_W:(res + 1) * GROUP_W]
        q_stack = jnp.concatenate(
            [jnp.where(lane_head == hd, q, jnp.zeros_like(q)) for hd in range(heads)], axis=0)
        s = lax.dot_general(q_stack, k_all[nb * BLOCK:(nb + 2) * BLOCK], (((1,), (1,)), ((), ())),
                            preferred_element_type=F32) + bias
        if nb == 0:
            s = jnp.where(key_col < missing_cols, MASKED, s)
        return s

    def finish(res, nb, s, v_heads):
        mx = jnp.max(s, axis=-1, keepdims=True)
        e = jnp.exp2(s - mx)
        den = jnp.sum(e, axis=-1, keepdims=True)
        p = (e * (1.0 / den)).astype(BF16)
        p_cat = jnp.concatenate([p[hd * BLOCK:(hd + 1) * BLOCK] for hd in range(heads)], axis=1)
        v_stack = jnp.concatenate([vh[nb * BLOCK:(nb + 2) * BLOCK] for vh in v_heads], axis=0)
        o = jnp.dot(p_cat, v_stack, preferred_element_type=F32)
        lse = mx + jnp.log2(den)
        lse_full = jnp.broadcast_to(lse[:BLOCK], (BLOCK, GROUP_W))
        for hd in range(1, heads):
            lse_full = jnp.where(lane_head == hd, lse[hd * BLOCK:(hd + 1) * BLOCK], lse_full)
        rows = slice(nb * BLOCK, (nb + 1) * BLOCK)
        lanes = slice(res * GROUP_W, (res + 1) * GROUP_W)
        o_ref[rows, lanes] = o.astype(o_ref.dtype)
        lse_ref[rows, lanes] = lse_full

    units = [(res, nb) for res in range(residues) for nb in range(blocks)]
    kv = {res: keys_values(res) for res in range(residues)}
    s_next = scores(*units[0], kv[units[0][0]][0])
    for i, (res, nb) in enumerate(units):
        s = s_next
        if i + 1 < len(units):
            s_next = scores(*units[i + 1], kv[units[i + 1][0]][0])
        finish(res, nb, s, kv[res][1])


def _dilated_group(q, k, v, bias_band, dil):
    batch, sub_seq, _ = q.shape
    blocks = min(DILATED_BLOCKS_PER_STEP, sub_seq // BLOCK)
    residues = min(DILATED_BLOCKS_PER_STEP // blocks, dil)
    tile = blocks * BLOCK
    width = residues * GROUP_W
    cur = pl.BlockSpec((None, tile, width), lambda b, r, n: (b, n, r))
    prev = pl.BlockSpec((None, BLOCK, width),
                        lambda b, r, n: (b, jnp.maximum(n * blocks - 1, 0), r))
    return pl.pallas_call(
        functools.partial(_dilated_body, blocks=blocks, residues=residues),
        grid=(batch, dil // residues, sub_seq // tile),
        in_specs=[cur, cur, cur, prev, prev, _resident(bias_band.shape)],
        out_specs=[cur, cur],
        out_shape=[jax.ShapeDtypeStruct(q.shape, BF16), jax.ShapeDtypeStruct(q.shape, F32)],
        compiler_params=_compiler_params(("parallel", "parallel", "parallel")),
        name=f"dilated_d{dil}",
    )(q, k, v, k, v, bias_band)


def _bias_band(rel_bias, group, dil, n_back):
    heads = slice(group * HEADS_PER_GROUP, (group + 1) * HEADS_PER_GROUP)
    buckets = _t5_bucket(np.arange(n_back + 1) * dil)
    bias = rel_bias[buckets][:, heads].T.astype(F32) * LOG2E
    period = 2 * BLOCK + 1
    j = BLOCK - np.arange(period)
    local = (j >= 0) & (j <= n_back)
    diag = jnp.where(local[None], bias[:, np.clip(j, 0, n_back)], MASKED)
    band = jnp.tile(diag, (1, BLOCK))[:, :BLOCK * 2 * BLOCK]
    return band.reshape(HEADS_PER_GROUP * BLOCK, 2 * BLOCK)


class _StickStep(NamedTuple):
    q: jax.Array
    k: jax.Array
    v: jax.Array
    mask: Optional[jax.Array]
    carry: Any
    after: Optional[int]


def _neg_abs(x):
    bits = lax.bitcast_convert_type(x, jnp.uint32) | jnp.uint32(0x80000000)
    return lax.bitcast_convert_type(bits, F32)


def _stick_steps(steps, upper):
    z = [lax.dot_general(st.q, st.k, (((1,), (1,)), ((), ())), preferred_element_type=F32)
         for st in steps]
    z = [s if st.mask is None else jnp.where(st.mask, s, MASKED) for s, st in zip(z, steps)]
    soft = [jnp.maximum(s, 0.0) + jnp.log2(1.0 + jnp.exp2(_neg_abs(s))) for s in z]
    log_beta = [s - p for s, p in zip(z, soft)]
    tail = [jnp.dot(p.astype(BF16), upper, preferred_element_type=F32) for p in soft]
    sums = [jnp.sum(p, axis=-1, keepdims=True) for p in soft]
    outs = []
    for i, st in enumerate(steps):
        carry = st.carry if st.after is None else st.carry + sums[st.after]
        w = jnp.exp2(log_beta[i] - (tail[i] + carry))
        outs.append(w.astype(BF16))
    outs = [jnp.dot(w, st.v, preferred_element_type=F32) for w, st in zip(outs, steps)]
    return outs, sums


def _stick_body(q_ref, k_ref, v_ref, upper_ref, o_ref, acc_ref, carry_ref):
    tile = upper_ref.shape[0]
    q_tiles = q_ref.shape[0] // tile
    heads = LANES // HEAD_DIM
    first_tile = pl.program_id(2) * q_tiles
    lane_head = lax.broadcasted_iota(jnp.int32, (tile, LANES), 1) // HEAD_DIM
    row = lax.broadcasted_iota(jnp.int32, (tile, tile), 0)
    col = lax.broadcasted_iota(jnp.int32, (tile, tile), 1)
    strictly_before = col < row
    upper = upper_ref[...]
    chains = []
    for t in range(q_tiles):
        q = q_ref[t * tile:(t + 1) * tile, :]
        for hd in range(heads):
            chains.append((t, hd, jnp.where(lane_head == hd, q, jnp.zeros_like(q))))

    def key_value(kt):
        start = pl.multiple_of(jnp.maximum(kt, 0) * tile, tile)
        v = v_ref[pl.ds(start, tile), :]
        return k_ref[pl.ds(start, tile), :], jnp.where(kt >= 0, v, jnp.zeros_like(v))

    diag = [key_value(first_tile + t) for t in range(q_tiles)]
    prev = [key_value(first_tile + t - 1) for t in range(q_tiles)]
    steps = []
    for t, hd, qh in chains:
        steps.append(_StickStep(qh, *diag[t], strictly_before, 0.0, None))
        steps.append(_StickStep(qh, *prev[t], None, 0.0, len(steps) - 1))
    outs, sums = _stick_steps(steps, upper)
    for c in range(len(chains)):
        acc_ref[c] = outs[2 * c] + outs[2 * c + 1]
        carry_ref[c] = sums[2 * c] + sums[2 * c + 1]

    def any_weight_left():
        return (jnp.min(carry_ref[...]) < F32_EXP2_UNDERFLOW).astype(jnp.int32)

    def more_tiles(state):
        j, live = state
        return jnp.logical_and(j < first_tile + q_tiles - 1, live > 0)

    def earlier(state):
        j, _ = state
        tiles = [key_value(first_tile + t - 1 - j) for t in range(q_tiles)]
        outs, sums = _stick_steps(
            [_StickStep(qh, *tiles[t], None, carry_ref[c], None)
             for c, (t, _, qh) in enumerate(chains)], upper)
        for c in range(len(chains)):
            acc_ref[c] += outs[c]
            carry_ref[c] += sums[c]
        return j + 1, any_weight_left()

    lax.while_loop(more_tiles, earlier, (jnp.int32(1), any_weight_left()))
    for t in range(q_tiles):
        out = acc_ref[t * heads]
        for hd in range(1, heads):
            out = jnp.where(lane_head == hd, acc_ref[t * heads + hd], out)
        o_ref[t * tile:(t + 1) * tile, :] = out.astype(o_ref.dtype)


def _stick_breaking(q, k, v):
    batch, seq, width = q.shape
    tile = STICK_TILE
    rows = STICK_Q_TILES * tile
    chains = STICK_Q_TILES * (LANES // HEAD_DIM)
    j = np.arange(tile)
    upper = jnp.asarray((j[:, None] > j[None, :]), BF16)
    return pl.pallas_call(
        _stick_body,
        grid=(batch, width // LANES, seq // rows),
        in_specs=[pl.BlockSpec((None, rows, LANES), lambda b, p, i: (b, i, p)),
                  pl.BlockSpec((None, seq, LANES), lambda b, p, i: (b, 0, p)),
                  pl.BlockSpec((None, seq, LANES), lambda b, p, i: (b, 0, p)),
                  _resident((tile, tile))],
        out_specs=pl.BlockSpec((None, rows, LANES), lambda b, p, i: (b, i, p)),
        out_shape=jax.ShapeDtypeStruct(q.shape, BF16),
        scratch_shapes=[pltpu.VMEM((chains, tile, LANES), F32),
                        pltpu.VMEM((chains, tile, 1), F32)],
        compiler_params=_compiler_params(("parallel", "parallel", "parallel")),
        name="stick_breaking",
    )(q, k, v, upper)


def _mix_tile(x_ref, gates_ref, yb_ref, group_refs, wa_ref, wb_ref, wo_ref, g_ref, slab_ref):
    tile, d_model = x_ref.shape

    def natural_order(ref, dil):
        if dil == 1:
            return ref[...].astype(F32)
        rows = tile // dil
        for r in range(dil):
            for s in range(GROUP_W // LANES):
                lane0 = r * GROUP_W + s * LANES
                slab_ref[s, pl.ds(r, rows, stride=dil), :] = (
                    ref[:, lane0:lane0 + LANES].astype(F32))
        return jnp.concatenate([slab_ref[s] for s in range(GROUP_W // LANES)], axis=-1)

    def sigmoid(v):
        return 0.5 * jnp.tanh(0.5 * v) + 0.5

    branch_b = jnp.dot(yb_ref[...], wb_ref[...], preferred_element_type=F32)
    outs, lses = [], []
    for g, (_, dil) in enumerate(DILATED_GROUPS):
        outs.append(natural_order(group_refs[2 * g], dil))
        lses.append(natural_order(group_refs[2 * g + 1], dil))
    top = functools.reduce(jnp.maximum, lses)
    wts = [jnp.exp2(l - top) for l in lses]
    ya = sum(w * o for w, o in zip(wts, outs)) / sum(wts)

    branch_a = jnp.dot(ya.astype(BF16), wa_ref[...], preferred_element_type=F32)
    gate_a = gates_ref[:, :d_model].astype(F32)
    gate_b = gates_ref[:, d_model:].astype(F32)
    merged = sigmoid(gate_a) * branch_a + sigmoid(gate_b) * branch_b
    mixed = jnp.dot(merged.astype(BF16), wo_ref[...], preferred_element_type=F32)
    return x_ref[...] + _rms_norm(mixed, g_ref[...])


def _ffn_tile(x_mid, g_pre_ref, w_up_ref, cw_ref, cb_ref, w_down_ref, g_post_ref, out_ref,
              halo_ref, u_ref, act_ref, slab_ref):
    tile, d_model = x_mid.shape
    d_ff = w_down_ref.shape[0]
    seg = tile // SUBLANES
    n_slabs = d_model // LANES
    halo_rows = (CONV_WIDTH - 1) * SUBLANES

    @pl.when(pl.program_id(1) == 0)
    def _():
        halo_ref[...] = jnp.zeros_like(halo_ref)

    for c in range(n_slabs):
        for s in range(SUBLANES):
            slab_ref[c, pl.ds(s, seg, stride=SUBLANES), :] = (
                x_mid[s * seg:(s + 1) * seg, c * LANES:(c + 1) * LANES])
    x = jnp.concatenate([slab_ref[c] for c in range(n_slabs)], axis=-1)
    h = _rms_norm(x, g_pre_ref[...]).astype(BF16)
    sublane = lax.broadcasted_iota(jnp.int32, (SUBLANES, FF_CHUNK), 0)

    def up(c):
        return [jnp.dot(h, w_up_ref[:, col:col + FF_CHUNK], preferred_element_type=F32)
                for col in (c * FF_CHUNK, d_ff + c * FF_CHUNK)]

    def conv(u, col, slot):
        cols = slice(col, col + FF_CHUNK)
        last = u[tile - halo_rows:, :]
        for k in range(CONV_WIDTH - 1):
            group = slice(k * SUBLANES, (k + 1) * SUBLANES)
            mixed = jnp.where(sublane == SUBLANES - 1, halo_ref[group, cols], last[group, :])
            u_ref[slot, group, :] = pltpu.roll(mixed, 1, axis=0)
        u_ref[slot, halo_rows:, :] = u
        halo_ref[:, cols] = last
        y = cb_ref[:, cols]
        for t in range(CONV_WIDTH):
            start = t * SUBLANES
            y = y + cw_ref[t:t + 1, cols] * u_ref[slot, start:start + tile, :]
        return y

    n_chunks = d_ff // FF_CHUNK
    u_next = up(0)
    for c in range(n_chunks):
        u_gate, u_val = u_next
        if c + 1 < n_chunks:
            u_next = up(c + 1)
        gate = conv(u_gate, c * FF_CHUNK, 0)
        val = conv(u_val, d_ff + c * FF_CHUNK, 1)
        act_ref[:, c * FF_CHUNK:(c + 1) * FF_CHUNK] = (
            jax.nn.gelu(gate, approximate=True) * val).astype(BF16)
    part_rows = tile // DOWN_PARTS
    groups = part_rows // SUBLANES
    down = [jnp.dot(act_ref[p * part_rows:(p + 1) * part_rows, :], w_down_ref[...],
                    preferred_element_type=F32) for p in range(DOWN_PARTS)]
    for p, acc in enumerate(down):
        rows = slice(p * part_rows, (p + 1) * part_rows)
        y = x[rows, :] + _rms_norm(acc, g_post_ref[...])
        for c in range(n_slabs):
            slab_ref[c, rows, :] = y[:, c * LANES:(c + 1) * LANES]
        for c in range(n_slabs):
            for s in range(SUBLANES):
                out_ref[s * seg + p * groups:s * seg + (p + 1) * groups,
                        c * LANES:(c + 1) * LANES] = (
                    slab_ref[c, pl.ds(p * part_rows + s, groups, stride=SUBLANES), :])


def _post_attention_body(x_ref, gates_ref, yb_ref, *refs):
    group_refs = refs[:2 * N_GROUPS]
    (wa_ref, wb_ref, wo_ref, g_post_mix_ref, g_pre_ref, w_up_ref, cw_ref, cb_ref, w_down_ref,
     g_post_ref, out_ref, group_slab_ref, halo_ref, u_ref, act_ref,
     slab_ref) = refs[2 * N_GROUPS:]
    x_mid = _mix_tile(x_ref, gates_ref, yb_ref, group_refs, wa_ref, wb_ref, wo_ref,
                      g_post_mix_ref, group_slab_ref)
    _ffn_tile(x_mid, g_pre_ref, w_up_ref, cw_ref, cb_ref, w_down_ref, g_post_ref, out_ref,
              halo_ref, u_ref, act_ref, slab_ref)


def _post_attention(x, gates, yb, groups, mix_consts, ffn_consts):
    batch, seq, d_model = x.shape
    tile = TOKEN_TILE
    d_ff = ffn_consts[4].shape[0]
    row_spec = lambda width: pl.BlockSpec((None, tile, width), lambda b, i: (b, i, 0))
    in_specs = [row_spec(d_model), row_spec(gates.shape[-1]), row_spec(yb.shape[-1])]
    operands = [x, gates, yb]
    for (o, lse), (_, dil) in zip(groups, DILATED_GROUPS):
        for arr in (o, lse):
            in_specs.append(pl.BlockSpec((None, tile // dil, dil * GROUP_W), lambda b, i: (b, i, 0)))
            operands.append(arr)
    for arr in (*mix_consts, *ffn_consts):
        in_specs.append(_resident(arr.shape))
        operands.append(arr)
    halo_rows = (CONV_WIDTH - 1) * SUBLANES
    return pl.pallas_call(
        _post_attention_body,
        grid=(batch, seq // tile),
        in_specs=in_specs,
        out_specs=row_spec(d_model),
        out_shape=jax.ShapeDtypeStruct(x.shape, F32),
        scratch_shapes=[pltpu.VMEM((GROUP_W // LANES, tile, LANES), F32),
                        pltpu.VMEM((halo_rows, 2 * d_ff), F32),
                        pltpu.VMEM((2, halo_rows + tile, FF_CHUNK), F32),
                        pltpu.VMEM((tile, d_ff), BF16),
                        pltpu.VMEM((d_model // LANES, tile, LANES), F32)],
        compiler_params=_compiler_params(("arbitrary", "arbitrary")),
        name="post_attention",
    )(*operands)


def kernel(x, g_pre_mix, w_in, rel_bias, w_branch_a, w_branch_b, w_out, g_post_mix, g_pre_ffn,
           w_up, conv_w, conv_b, w_down, g_post_ffn):
    depth = w_in.shape[0]
    for l in range(depth):
        proj = _in_proj(x, g_pre_mix[l][None], w_in[l].astype(BF16))
        qb, kb, vb, gates = proj[3 * N_GROUPS:]
        groups = []
        for g, (window, dil) in enumerate(DILATED_GROUPS):
            q, k, v = proj[3 * g:3 * g + 3]
            band = _bias_band(rel_bias, g, dil, window // dil)
            groups.append(_dilated_group(q, k, v, band, dil))
        yb = _stick_breaking(qb, kb, vb)
        mix_consts = (w_branch_a[l].astype(BF16), w_branch_b[l].astype(BF16),
                      w_out[l].astype(BF16), g_post_mix[l][None])
        ffn_consts = (g_pre_ffn[l][None], w_up[l].astype(BF16), conv_w[l], conv_b[l][None],
                      w_down[l].astype(BF16), g_post_ffn[l][None])
        x = _post_attention(x, gates, yb, groups, mix_consts, ffn_consts)
    return x
```

```python
import functools
import math
from typing import Any, NamedTuple, Optional

import numpy as np
import jax
import jax.numpy as jnp
from jax import lax
from jax.experimental import pallas as pl
from jax.experimental.pallas import tpu as pltpu

F32 = jnp.float32
BF16 = jnp.bfloat16

HEAD_DIM = 64
DILATED_GROUPS = ((128, 1), (512, 4), (2048, 16))
HEADS_PER_GROUP = 4
N_GROUPS = len(DILATED_GROUPS)
N_HEADS_A = HEADS_PER_GROUP * N_GROUPS
N_HEADS_B = 8
GROUP_W = HEADS_PER_GROUP * HEAD_DIM
A_QKV = N_HEADS_A * HEAD_DIM
B_QKV = N_HEADS_B * HEAD_DIM
BLOCK = 128
NUM_BUCKETS = 32
MAX_DISTANCE = 2048
CONV_WIDTH = 3
EPS = 1e-6
QK_SCALE = 1.0 / math.sqrt(HEAD_DIM)
LOG2E = math.log2(math.e)
MASKED = -1e30
F32_EXP2_UNDERFLOW = 150.1

LANES = 128
SUBLANES = 8
VMEM_LIMIT_BYTES = 56 * 1024 * 1024

TOKEN_TILE = 512
IN_PROJ_TILE = 1024
GATE_CHUNK = 512
DILATED_BLOCKS_PER_STEP = 16
STICK_Q_TILES = 8
STICK_TILE = 256
FF_CHUNK = 256
DOWN_PARTS = 2


def _t5_bucket(dist):
    max_exact = NUM_BUCKETS // 2
    n = np.asarray(dist, dtype=np.float32)
    large = max_exact + (np.log(np.maximum(n, 1.0) / max_exact)
                         / np.log(MAX_DISTANCE / max_exact)
                         * (NUM_BUCKETS - max_exact)).astype(np.int32)
    large = np.minimum(large, NUM_BUCKETS - 1)
    return np.where(n < max_exact, n.astype(np.int32), large).astype(np.int32)


def _rms_norm(x, g):
    return x * lax.rsqrt(jnp.mean(x * x, axis=-1, keepdims=True) + EPS) * g


def _compiler_params(semantics):
    return pltpu.CompilerParams(dimension_semantics=semantics, vmem_limit_bytes=VMEM_LIMIT_BYTES)


def _resident(shape):
    return pl.BlockSpec(shape, lambda *_: (0,) * len(shape), pipeline_mode=pl.Buffered(1))


def _in_proj_body(x_ref, g_ref, w_ref, *refs):
    a_refs = refs[:3 * N_GROUPS]
    qb_ref, kb_ref, vb_ref, gates_ref, slab_ref = refs[3 * N_GROUPS:]
    tile = x_ref.shape[0]
    h = _rms_norm(x_ref[...], g_ref[...]).astype(BF16)

    def proj(col, width):
        return jnp.dot(h, w_ref[:, col:col + width], preferred_element_type=F32)

    for t in range(3):
        scale = QK_SCALE * LOG2E if t == 0 else 1.0
        for g, (_, dil) in enumerate(DILATED_GROUPS):
            acc = proj(t * A_QKV + g * GROUP_W, GROUP_W) * scale
            out_ref = a_refs[3 * g + t]
            if dil == 1:
                out_ref[...] = acc.astype(BF16)
                continue
            rows = tile // dil
            for s in range(GROUP_W // LANES):
                slab_ref[s] = acc[:, s * LANES:(s + 1) * LANES]
            for r in range(dil):
                for s in range(GROUP_W // LANES):
                    lane0 = r * GROUP_W + s * LANES
                    out_ref[:, lane0:lane0 + LANES] = (
                        slab_ref[s, pl.ds(r, rows, stride=dil), :].astype(BF16))

    base = 3 * A_QKV
    qb_ref[...] = (proj(base, B_QKV) * (QK_SCALE * LOG2E)).astype(BF16)
    kb_ref[...] = proj(base + B_QKV, B_QKV).astype(BF16)
    vb_ref[...] = proj(base + 2 * B_QKV, B_QKV).astype(BF16)
    base += 3 * B_QKV
    gate_w = gates_ref.shape[1]
    for c in range(0, gate_w, GATE_CHUNK):
        gates_ref[:, c:c + GATE_CHUNK] = proj(base + c, GATE_CHUNK).astype(BF16)


def _in_proj(x, g, w):
    batch, seq, d_model = x.shape
    tile = IN_PROJ_TILE
    n_in = w.shape[1]
    gate_w = n_in - 3 * A_QKV - 3 * B_QKV
    out_shape, out_specs = [], []
    for _, dil in DILATED_GROUPS:
        for _ in range(3):
            out_shape.append(jax.ShapeDtypeStruct((batch, seq // dil, dil * GROUP_W), BF16))
            out_specs.append(pl.BlockSpec((None, tile // dil, dil * GROUP_W), lambda b, i: (b, i, 0)))
    for _ in range(3):
        out_shape.append(jax.ShapeDtypeStruct((batch, seq, B_QKV), BF16))
        out_specs.append(pl.BlockSpec((None, tile, B_QKV), lambda b, i: (b, i, 0)))
    out_shape.append(jax.ShapeDtypeStruct((batch, seq, gate_w), BF16))
    out_specs.append(pl.BlockSpec((None, tile, gate_w), lambda b, i: (b, i, 0)))
    return pl.pallas_call(
        _in_proj_body,
        grid=(batch, seq // tile),
        in_specs=[pl.BlockSpec((None, tile, d_model), lambda b, i: (b, i, 0)),
                  _resident((1, d_model)),
                  _resident((d_model, n_in))],
        out_specs=out_specs,
        out_shape=out_shape,
        scratch_shapes=[pltpu.VMEM((GROUP_W // LANES, tile, LANES), F32)],
        compiler_params=_compiler_params(("parallel", "parallel")),
        name="in_proj",
    )(x, g, w)


def _dilated_body(q_ref, k_ref, v_ref, kp_ref, vp_ref, bias_ref, o_ref, lse_ref, *,
                  blocks, residues):
    heads = HEADS_PER_GROUP
    missing_cols = jnp.where(pl.program_id(2) == 0, BLOCK, 0)
    lane_head = lax.broadcasted_iota(jnp.int32, (BLOCK, GROUP_W), 1) // HEAD_DIM
    key_col = lax.broadcasted_iota(jnp.int32, (heads * BLOCK, 2 * BLOCK), 1)
    bias = bias_ref[...]

    def keys_values(res):
        lanes = slice(res * GROUP_W, (res + 1) * GROUP_W)
        k_all = jnp.concatenate([kp_ref[:, lanes], k_ref[:, lanes]], axis=0)
        v_all = jnp.concatenate([vp_ref[:, lanes], v_ref[:, lanes]], axis=0)
        v_lane_head = lax.broadcasted_iota(jnp.int32, v_all.shape, 1) // HEAD_DIM
        v_heads = [jnp.where(v_lane_head == hd, v_all, jnp.zeros_like(v_all))
                   for hd in range(heads)]
        return k_all, v_heads

    def scores(res, nb, k_all):
        q = q_ref[nb * BLOCK:(nb + 1) * BLOCK, res * GROUP_W:(res + 1) * GROUP_W]
        q_stack = jnp.concatenate(
            [jnp.where(lane_head == hd, q, jnp.zeros_like(q)) for hd in range(heads)], axis=0)
        s = lax.dot_general(q_stack, k_all[nb * BLOCK:(nb + 2) * BLOCK], (((1,), (1,)), ((), ())),
                            preferred_element_type=F32) + bias
        if nb == 0:
            s = jnp.where(key_col < missing_cols, MASKED, s)
        return s

    def finish(res, nb, s, v_heads):
        mx = jnp.max(s, axis=-1, keepdims=True)
        e = jnp.exp2(s - mx)
        den = jnp.sum(e, axis=-1, keepdims=True)
        p = (e * (1.0 / den)).astype(BF16)
        p_cat = jnp.concatenate([p[hd * BLOCK:(hd + 1) * BLOCK] for hd in range(heads)], axis=1)
        v_stack = jnp.concatenate([vh[nb * BLOCK:(nb + 2) * BLOCK] for vh in v_heads], axis=0)
        o = jnp.dot(p_cat, v_stack, preferred_element_type=F32)
        lse = mx + jnp.log2(den)
        lse_full = jnp.broadcast_to(lse[:BLOCK], (BLOCK, GROUP_W))
        for hd in range(1, heads):
            lse_full = jnp.where(lane_head == hd, lse[hd * BLOCK:(hd + 1) * BLOCK], lse_full)
        rows = slice(nb * BLOCK, (nb + 1) * BLOCK)
        lanes = slice(res * GROUP_W, (res + 1) * GROUP_W)
        o_ref[rows, lanes] = o.astype(o_ref.dtype)
        lse_ref[rows, lanes] = lse_full

    units = [(res, nb) for res in range(residues) for nb in range(blocks)]
    kv = {res: keys_values(res) for res in range(residues)}
    s_next = scores(*units[0], kv[units[0][0]][0])
    for i, (res, nb) in enumerate(units):
        s = s_next
        if i + 1 < len(units):
            s_next = scores(*units[i + 1], kv[units[i + 1][0]][0])
        finish(res, nb, s, kv[res][1])


def _dilated_group(q, k, v, bias_band, dil):
    batch, sub_seq, _ = q.shape
    blocks = min(DILATED_BLOCKS_PER_STEP, sub_seq // BLOCK)
    residues = min(DILATED_BLOCKS_PER_STEP // blocks, dil)
    tile = blocks * BLOCK
    width = residues * GROUP_W
    cur = pl.BlockSpec((None, tile, width), lambda b, r, n: (b, n, r))
    prev = pl.BlockSpec((None, BLOCK, width),
                        lambda b, r, n: (b, jnp.maximum(n * blocks - 1, 0), r))
    return pl.pallas_call(
        functools.partial(_dilated_body, blocks=blocks, residues=residues),
        grid=(batch, dil // residues, sub_seq // tile),
        in_specs=[cur, cur, cur, prev, prev, _resident(bias_band.shape)],
        out_specs=[cur, cur],
        out_shape=[jax.ShapeDtypeStruct(q.shape, BF16), jax.ShapeDtypeStruct(q.shape, F32)],
        compiler_params=_compiler_params(("parallel", "parallel", "parallel")),
        name=f"dilated_d{dil}",
    )(q, k, v, k, v, bias_band)


def _bias_band(rel_bias, group, dil, n_back):
    heads = slice(group * HEADS_PER_GROUP, (group + 1) * HEADS_PER_GROUP)
    buckets = _t5_bucket(np.arange(n_back + 1) * dil)
    bias = rel_bias[buckets][:, heads].T.astype(F32) * LOG2E
    period = 2 * BLOCK + 1
    j = BLOCK - np.arange(period)
    local = (j >= 0) & (j <= n_back)
    diag = jnp.where(local[None], bias[:, np.clip(j, 0, n_back)], MASKED)
    band = jnp.tile(diag, (1, BLOCK))[:, :BLOCK * 2 * BLOCK]
    return band.reshape(HEADS_PER_GROUP * BLOCK, 2 * BLOCK)


class _StickStep(NamedTuple):
    q: jax.Array
    k: jax.Array
    v: jax.Array
    mask: Optional[jax.Array]
    carry: Any
    after: Optional[int]


def _neg_abs(x):
    bits = lax.bitcast_convert_type(x, jnp.uint32) | jnp.uint32(0x80000000)
    return lax.bitcast_convert_type(bits, F32)


def _stick_steps(steps, upper):
    z = [lax.dot_general(st.q, st.k, (((1,), (1,)), ((), ())), preferred_element_type=F32)
         for st in steps]
    z = [s if st.mask is None else jnp.where(st.mask, s, MASKED) for s, st in zip(z, steps)]
    soft = [jnp.maximum(s, 0.0) + jnp.log2(1.0 + jnp.exp2(_neg_abs(s))) for s in z]
    log_beta = [s - p for s, p in zip(z, soft)]
    tail = [jnp.dot(p.astype(BF16), upper, preferred_element_type=F32) for p in soft]
    sums = [jnp.sum(p, axis=-1, keepdims=True) for p in soft]
    outs = []
    for i, st in enumerate(steps):
        carry = st.carry if st.after is None else st.carry + sums[st.after]
        w = jnp.exp2(log_beta[i] - (tail[i] + carry))
        outs.append(w.astype(BF16))
    outs = [jnp.dot(w, st.v, preferred_element_type=F32) for w, st in zip(outs, steps)]
    return outs, sums


def _stick_body(q_ref, k_ref, v_ref, upper_ref, o_ref, acc_ref, carry_ref):
    tile = upper_ref.shape[0]
    q_tiles = q_ref.shape[0] // tile
    heads = LANES // HEAD_DIM
    first_tile = pl.program_id(2) * q_tiles
    lane_head = lax.broadcasted_iota(jnp.int32, (tile, LANES), 1) // HEAD_DIM
    row = lax.broadcasted_iota(jnp.int32, (tile, tile), 0)
    col = lax.broadcasted_iota(jnp.int32, (tile, tile), 1)
    strictly_before = col < row
    upper = upper_ref[...]
    chains = []
    for t in range(q_tiles):
        q = q_ref[t * tile:(t + 1) * tile, :]
        for hd in range(heads):
            chains.append((t, hd, jnp.where(lane_head == hd, q, jnp.zeros_like(q))))

    def key_value(kt):
        start = pl.multiple_of(jnp.maximum(kt, 0) * tile, tile)
        v = v_ref[pl.ds(start, tile), :]
        return k_ref[pl.ds(start, tile), :], jnp.where(kt >= 0, v, jnp.zeros_like(v))

    diag = [key_value(first_tile + t) for t in range(q_tiles)]
    prev = [key_value(first_tile + t - 1) for t in range(q_tiles)]
    steps = []
    for t, hd, qh in chains:
        steps.append(_StickStep(qh, *diag[t], strictly_before, 0.0, None))
        steps.append(_StickStep(qh, *prev[t], None, 0.0, len(steps) - 1))
    outs, sums = _stick_steps(steps, upper)
    for c in range(len(chains)):
        acc_ref[c] = outs[2 * c] + outs[2 * c + 1]
        carry_ref[c] = sums[2 * c] + sums[2 * c + 1]

    def any_weight_left():
        return (jnp.min(carry_ref[...]) < F32_EXP2_UNDERFLOW).astype(jnp.int32)

    def more_tiles(state):
        j, live = state
        return jnp.logical_and(j < first_tile + q_tiles - 1, live > 0)

    def earlier(state):
        j, _ = state
        tiles = [key_value(first_tile + t - 1 - j) for t in range(q_tiles)]
        outs, sums = _stick_steps(
            [_StickStep(qh, *tiles[t], None, carry_ref[c], None)
             for c, (t, _, qh) in enumerate(chains)], upper)
        for c in range(len(chains)):
            acc_ref[c] += outs[c]
            carry_ref[c] += sums[c]
        return j + 1, any_weight_left()

    lax.while_loop(more_tiles, earlier, (jnp.int32(1), any_weight_left()))
    for t in range(q_tiles):
        out = acc_ref[t * heads]
        for hd in range(1, heads):
            out = jnp.where(lane_head == hd, acc_ref[t * heads + hd], out)
        o_ref[t * tile:(t + 1) * tile, :] = out.astype(o_ref.dtype)


def _stick_breaking(q, k, v):
    batch, seq, width = q.shape
    tile = STICK_TILE
    rows = STICK_Q_TILES * tile
    chains = STICK_Q_TILES * (LANES // HEAD_DIM)
    j = np.arange(tile)
    upper = jnp.asarray((j[:, None] > j[None, :]), BF16)
    return pl.pallas_call(
        _stick_body,
        grid=(batch, width // LANES, seq // rows),
        in_specs=[pl.BlockSpec((None, rows, LANES), lambda b, p, i: (b, i, p)),
                  pl.BlockSpec((None, seq, LANES), lambda b, p, i: (b, 0, p)),
                  pl.BlockSpec((None, seq, LANES), lambda b, p, i: (b, 0, p)),
                  _resident((tile, tile))],
        out_specs=pl.BlockSpec((None, rows, LANES), lambda b, p, i: (b, i, p)),
        out_shape=jax.ShapeDtypeStruct(q.shape, BF16),
        scratch_shapes=[pltpu.VMEM((chains, tile, LANES), F32),
                        pltpu.VMEM((chains, tile, 1), F32)],
        compiler_params=_compiler_params(("parallel", "parallel", "parallel")),
        name="stick_breaking",
    )(q, k, v, upper)


def _mix_tile(x_ref, gates_ref, yb_ref, group_refs, wa_ref, wb_ref, wo_ref, g_ref, slab_ref):
    tile, d_model = x_ref.shape

    def natural_order(ref, dil):
        if dil == 1:
            return ref[...].astype(F32)
        rows = tile // dil
        for r in range(dil):
            for s in range(GROUP_W // LANES):
                lane0 = r * GROUP_W + s * LANES
                slab_ref[s, pl.ds(r, rows, stride=dil), :] = (
                    ref[:, lane0:lane0 + LANES].astype(F32))
        return jnp.concatenate([slab_ref[s] for s in range(GROUP_W // LANES)], axis=-1)

    def sigmoid(v):
        return 0.5 * jnp.tanh(0.5 * v) + 0.5

    branch_b = jnp.dot(yb_ref[...], wb_ref[...], preferred_element_type=F32)
    outs, lses = [], []
    for g, (_, dil) in enumerate(DILATED_GROUPS):
        outs.append(natural_order(group_refs[2 * g], dil))
        lses.append(natural_order(group_refs[2 * g + 1], dil))
    top = functools.reduce(jnp.maximum, lses)
    wts = [jnp.exp2(l - top) for l in lses]
    ya = sum(w * o for w, o in zip(wts, outs)) / sum(wts)

    branch_a = jnp.dot(ya.astype(BF16), wa_ref[...], preferred_element_type=F32)
    gate_a = gates_ref[:, :d_model].astype(F32)
    gate_b = gates_ref[:, d_model:].astype(F32)
    merged = sigmoid(gate_a) * branch_a + sigmoid(gate_b) * branch_b
    mixed = jnp.dot(merged.astype(BF16), wo_ref[...], preferred_element_type=F32)
    return x_ref[...] + _rms_norm(mixed, g_ref[...])


def _ffn_tile(x_mid, g_pre_ref, w_up_ref, cw_ref, cb_ref, w_down_ref, g_post_ref, out_ref,
              halo_ref, u_ref, act_ref, slab_ref):
    tile, d_model = x_mid.shape
    d_ff = w_down_ref.shape[0]
    seg = tile // SUBLANES
    n_slabs = d_model // LANES
    halo_rows = (CONV_WIDTH - 1) * SUBLANES

    @pl.when(pl.program_id(1) == 0)
    def _():
        halo_ref[...] = jnp.zeros_like(halo_ref)

    for c in range(n_slabs):
        for s in range(SUBLANES):
            slab_ref[c, pl.ds(s, seg, stride=SUBLANES), :] = (
                x_mid[s * seg:(s + 1) * seg, c * LANES:(c + 1) * LANES])
    x = jnp.concatenate([slab_ref[c] for c in range(n_slabs)], axis=-1)
    h = _rms_norm(x, g_pre_ref[...]).astype(BF16)
    sublane = lax.broadcasted_iota(jnp.int32, (SUBLANES, FF_CHUNK), 0)

    def up(c):
        return [jnp.dot(h, w_up_ref[:, col:col + FF_CHUNK], preferred_element_type=F32)
                for col in (c * FF_CHUNK, d_ff + c * FF_CHUNK)]

    def conv(u, col, slot):
        cols = slice(col, col + FF_CHUNK)
        last = u[tile - halo_rows:, :]
        for k in range(CONV_WIDTH - 1):
            group = slice(k * SUBLANES, (k + 1) * SUBLANES)
            mixed = jnp.where(sublane == SUBLANES - 1, halo_ref[group, cols], last[group, :])
            u_ref[slot, group, :] = pltpu.roll(mixed, 1, axis=0)
        u_ref[slot, halo_rows:, :] = u
        halo_ref[:, cols] = last
        y = cb_ref[:, cols]
        for t in range(CONV_WIDTH):
            start = t * SUBLANES
            y = y + cw_ref[t:t + 1, cols] * u_ref[slot, start:start + tile, :]
        return y

    n_chunks = d_ff // FF_CHUNK
    u_next = up(0)
    for c in range(n_chunks):
        u_gate, u_val = u_next
        if c + 1 < n_chunks:
            u_next = up(c + 1)
        gate = conv(u_gate, c * FF_CHUNK, 0)
        val = conv(u_val, d_ff + c * FF_CHUNK, 1)
        act_ref[:, c * FF_CHUNK:(c + 1) * FF_CHUNK] = (
            jax.nn.gelu(gate, approximate=True) * val).astype(BF16)
    part_rows = tile // DOWN_PARTS
    groups = part_rows // SUBLANES
    down = [jnp.dot(act_ref[p * part_rows:(p + 1) * part_rows, :], w_down_ref[...],
                    preferred_element_type=F32) for p in range(DOWN_PARTS)]
    for p, acc in enumerate(down):
        rows = slice(p * part_rows, (p + 1) * part_rows)
        y = x[rows, :] + _rms_norm(acc, g_post_ref[...])
        for c in range(n_slabs):
            slab_ref[c, rows, :] = y[:, c * LANES:(c + 1) * LANES]
        for c in range(n_slabs):
            for s in range(SUBLANES):
                out_ref[s * seg + p * groups:s * seg + (p + 1) * groups,
                        c * LANES:(c + 1) * LANES] = (
                    slab_ref[c, pl.ds(p * part_rows + s, groups, stride=SUBLANES), :])


def _post_attention_body(x_ref, gates_ref, yb_ref, *refs):
    group_refs = refs[:2 * N_GROUPS]
    (wa_ref, wb_ref, wo_ref, g_post_mix_ref, g_pre_ref, w_up_ref, cw_ref, cb_ref, w_down_ref,
     g_post_ref, out_ref, group_slab_ref, halo_ref, u_ref, act_ref,
     slab_ref) = refs[2 * N_GROUPS:]
    x_mid = _mix_tile(x_ref, gates_ref, yb_ref, group_refs, wa_ref, wb_ref, wo_ref,
                      g_post_mix_ref, group_slab_ref)
    _ffn_tile(x_mid, g_pre_ref, w_up_ref, cw_ref, cb_ref, w_down_ref, g_post_ref, out_ref,
              halo_ref, u_ref, act_ref, slab_ref)


def _post_attention(x, gates, yb, groups, mix_consts, ffn_consts):
    batch, seq, d_model = x.shape
    tile = TOKEN_TILE
    d_ff = ffn_consts[4].shape[0]
    row_spec = lambda width: pl.BlockSpec((None, tile, width), lambda b, i: (b, i, 0))
    in_specs = [row_spec(d_model), row_spec(gates.shape[-1]), row_spec(yb.shape[-1])]
    operands = [x, gates, yb]
    for (o, lse), (_, dil) in zip(groups, DILATED_GROUPS):
        for arr in (o, lse):
            in_specs.append(pl.BlockSpec((None, tile // dil, dil * GROUP_W), lambda b, i: (b, i, 0)))
            operands.append(arr)
    for arr in (*mix_consts, *ffn_consts):
        in_specs.append(_resident(arr.shape))
        operands.append(arr)
    halo_rows = (CONV_WIDTH - 1) * SUBLANES
    return pl.pallas_call(
        _post_attention_body,
        grid=(batch, seq // tile),
        in_specs=in_specs,
        out_specs=row_spec(d_model),
        out_shape=jax.ShapeDtypeStruct(x.shape, F32),
        scratch_shapes=[pltpu.VMEM((GROUP_W // LANES, tile, LANES), F32),
                        pltpu.VMEM((halo_rows, 2 * d_ff), F32),
                        pltpu.VMEM((2, halo_rows + tile, FF_CHUNK), F32),
                        pltpu.VMEM((tile, d_ff), BF16),
                        pltpu.VMEM((d_model // LANES, tile, LANES), F32)],
        compiler_params=_compiler_params(("arbitrary", "arbitrary")),
        name="post_attention",
    )(*operands)


def kernel(x, g_pre_mix, w_in, rel_bias, w_branch_a, w_branch_b, w_out, g_post_mix, g_pre_ffn,
           w_up, conv_w, conv_b, w_down, g_post_ffn):
    depth = w_in.shape[0]
    for l in range(depth):
        proj = _in_proj(x, g_pre_mix[l][None], w_in[l].astype(BF16))
        qb, kb, vb, gates = proj[3 * N_GROUPS:]
        groups = []
        for g, (window, dil) in enumerate(DILATED_GROUPS):
            q, k, v = proj[3 * g:3 * g + 3]
            band = _bias_band(rel_bias, g, dil, window // dil)
            groups.append(_dilated_group(q, k, v, band, dil))
        yb = _stick_breaking(qb, kb, vb)
        mix_consts = (w_branch_a[l].astype(BF16), w_branch_b[l].astype(BF16),
                      w_out[l].astype(BF16), g_post_mix[l][None])
        ffn_consts = (g_pre_ffn[l][None], w_up[l].astype(BF16), conv_w[l], conv_b[l][None],
                      w_down[l].astype(BF16), g_post_ffn[l][None])
        x = _post_attention(x, gates, yb, groups, mix_consts, ffn_consts)
    return x
```

```python
import functools
import math
from typing import Any, NamedTuple, Optional

import numpy as np
import jax
import jax.numpy as jnp
from jax import lax
from jax.experimental import pallas as pl
from jax.experimental.pallas import tpu as pltpu

F32 = jnp.float32
BF16 = jnp.bfloat16

HEAD_DIM = 64
DILATED_GROUPS = ((128, 1), (512, 4), (2048, 16))
HEADS_PER_GROUP = 4
N_GROUPS = len(DILATED_GROUPS)
N_HEADS_A = HEADS_PER_GROUP * N_GROUPS
N_HEADS_B = 8
GROUP_W = HEADS_PER_GROUP * HEAD_DIM
A_QKV = N_HEADS_A * HEAD_DIM
B_QKV = N_HEADS_B * HEAD_DIM
BLOCK = 128
NUM_BUCKETS = 32
MAX_DISTANCE = 2048
CONV_WIDTH = 3
EPS = 1e-6
QK_SCALE = 1.0 / math.sqrt(HEAD_DIM)
LOG2E = math.log2(math.e)
MASKED = -1e30
F32_EXP2_UNDERFLOW = 150.1

LANES = 128
SUBLANES = 8
VMEM_LIMIT_BYTES = 56 * 1024 * 1024

TOKEN_TILE = 512
IN_PROJ_TILE = 1024
GATE_CHUNK = 512
DILATED_BLOCKS_PER_STEP = 32
STICK_Q_TILES = 8
STICK_TILE = 256
FF_CHUNK = 256
DOWN_PARTS = 2


def _t5_bucket(dist):
    max_exact = NUM_BUCKETS // 2
    n = np.asarray(dist, dtype=np.float32)
    large = max_exact + (np.log(np.maximum(n, 1.0) / max_exact)
                         / np.log(MAX_DISTANCE / max_exact)
                         * (NUM_BUCKETS - max_exact)).astype(np.int32)
    large = np.minimum(large, NUM_BUCKETS - 1)
    return np.where(n < max_exact, n.astype(np.int32), large).astype(np.int32)


def _rms_norm(x, g):
    return x * lax.rsqrt(jnp.mean(x * x, axis=-1, keepdims=True) + EPS) * g


def _compiler_params(semantics):
    return pltpu.CompilerParams(dimension_semantics=semantics, vmem_limit_bytes=VMEM_LIMIT_BYTES)


def _resident(shape):
    return pl.BlockSpec(shape, lambda *_: (0,) * len(shape), pipeline_mode=pl.Buffered(1))


def _in_proj_body(x_ref, g_ref, w_ref, *refs):
    a_refs = refs[:3 * N_GROUPS]
    qb_ref, kb_ref, vb_ref, gates_ref, slab_ref = refs[3 * N_GROUPS:]
    tile = x_ref.shape[0]
    h = _rms_norm(x_ref[...], g_ref[...]).astype(BF16)

    def proj(col, width):
        return jnp.dot(h, w_ref[:, col:col + width], preferred_element_type=F32)

    for t in range(3):
        scale = QK_SCALE * LOG2E if t == 0 else 1.0
        for g, (_, dil) in enumerate(DILATED_GROUPS):
            acc = proj(t * A_QKV + g * GROUP_W, GROUP_W) * scale
            out_ref = a_refs[3 * g + t]
            if dil == 1:
                out_ref[...] = acc.astype(BF16)
                continue
            rows = tile // dil
            for s in range(GROUP_W // LANES):
                slab_ref[s] = acc[:, s * LANES:(s + 1) * LANES]
            for r in range(dil):
                for s in range(GROUP_W // LANES):
                    lane0 = r * GROUP_W + s * LANES
                    out_ref[:, lane0:lane0 + LANES] = (
                        slab_ref[s, pl.ds(r, rows, stride=dil), :].astype(BF16))

    base = 3 * A_QKV
    qb_ref[...] = (proj(base, B_QKV) * (QK_SCALE * LOG2E)).astype(BF16)
    kb_ref[...] = proj(base + B_QKV, B_QKV).astype(BF16)
    vb_ref[...] = proj(base + 2 * B_QKV, B_QKV).astype(BF16)
    base += 3 * B_QKV
    gate_w = gates_ref.shape[1]
    for c in range(0, gate_w, GATE_CHUNK):
        gates_ref[:, c:c + GATE_CHUNK] = proj(base + c, GATE_CHUNK).astype(BF16)


def _in_proj(x, g, w):
    batch, seq, d_model = x.shape
    tile = IN_PROJ_TILE
    n_in = w.shape[1]
    gate_w = n_in - 3 * A_QKV - 3 * B_QKV
    out_shape, out_specs = [], []
    for _, dil in DILATED_GROUPS:
        for _ in range(3):
            out_shape.append(jax.ShapeDtypeStruct((batch, seq // dil, dil * GROUP_W), BF16))
            out_specs.append(pl.BlockSpec((None, tile // dil, dil * GROUP_W), lambda b, i: (b, i, 0)))
    for _ in range(3):
        out_shape.append(jax.ShapeDtypeStruct((batch, seq, B_QKV), BF16))
        out_specs.append(pl.BlockSpec((None, tile, B_QKV), lambda b, i: (b, i, 0)))
    out_shape.append(jax.ShapeDtypeStruct((batch, seq, gate_w), BF16))
    out_specs.append(pl.BlockSpec((None, tile, gate_w), lambda b, i: (b, i, 0)))
    return pl.pallas_call(
        _in_proj_body,
        grid=(batch, seq // tile),
        in_specs=[pl.BlockSpec((None, tile, d_model), lambda b, i: (b, i, 0)),
                  _resident((1, d_model)),
                  _resident((d_model, n_in))],
        out_specs=out_specs,
        out_shape=out_shape,
        scratch_shapes=[pltpu.VMEM((GROUP_W // LANES, tile, LANES), F32)],
        compiler_params=_compiler_params(("parallel", "parallel")),
        name="in_proj",
    )(x, g, w)


def _dilated_body(q_ref, k_ref, v_ref, kp_ref, vp_ref, bias_ref, o_ref, lse_ref, *,
                  blocks, residues):
    heads = HEADS_PER_GROUP
    missing_cols = jnp.where(pl.program_id(2) == 0, BLOCK, 0)
    lane_head = lax.broadcasted_iota(jnp.int32, (BLOCK, GROUP_W), 1) // HEAD_DIM
    key_col = lax.broadcasted_iota(jnp.int32, (heads * BLOCK, 2 * BLOCK), 1)
    bias = bias_ref[...]

    def keys_values(res):
        lanes = slice(res * GROUP_W, (res + 1) * GROUP_W)
        k_all = jnp.concatenate([kp_ref[:, lanes], k_ref[:, lanes]], axis=0)
        v_all = jnp.concatenate([vp_ref[:, lanes], v_ref[:, lanes]], axis=0)
        v_lane_head = lax.broadcasted_iota(jnp.int32, v_all.shape, 1) // HEAD_DIM
        v_heads = [jnp.where(v_lane_head == hd, v_all, jnp.zeros_like(v_all))
                   for hd in range(heads)]
        return k_all, v_heads

    def scores(res, nb, k_all):
        q = q_ref[nb * BLOCK:(nb + 1) * BLOCK, res * GROUP_W:(res + 1) * GROUP_W]
        q_stack = jnp.concatenate(
            [jnp.where(lane_head == hd, q, jnp.zeros_like(q)) for hd in range(heads)], axis=0)
        s = lax.dot_general(q_stack, k_all[nb * BLOCK:(nb + 2) * BLOCK], (((1,), (1,)), ((), ())),
                            preferred_element_type=F32) + bias
        if nb == 0:
            s = jnp.where(key_col < missing_cols, MASKED, s)
        return s

    def finish(res, nb, s, v_heads):
        mx = jnp.max(s, axis=-1, keepdims=True)
        e = jnp.exp2(s - mx)
        den = jnp.sum(e, axis=-1, keepdims=True)
        p = (e * (1.0 / den)).astype(BF16)
        p_cat = jnp.concatenate([p[hd * BLOCK:(hd + 1) * BLOCK] for hd in range(heads)], axis=1)
        v_stack = jnp.concatenate([vh[nb * BLOCK:(nb + 2) * BLOCK] for vh in v_heads], axis=0)
        o = jnp.dot(p_cat, v_stack, preferred_element_type=F32)
        lse = mx + jnp.log2(den)
        lse_full = jnp.broadcast_to(lse[:BLOCK], (BLOCK, GROUP_W))
        for hd in range(1, heads):
            lse_full = jnp.where(lane_head == hd, lse[hd * BLOCK:(hd + 1) * BLOCK], lse_full)
        rows = slice(nb * BLOCK, (nb + 1) * BLOCK)
        lanes = slice(res * GROUP_W, (res + 1) * GROUP_W)
        o_ref[rows, lanes] = o.astype(o_ref.dtype)
        lse_ref[rows, lanes] = lse_full

    units = [(res, nb) for res in range(residues) for nb in range(blocks)]
    kv = {res: keys_values(res) for res in range(residues)}
    s_next = scores(*units[0], kv[units[0][0]][0])
    for i, (res, nb) in enumerate(units):
        s = s_next
        if i + 1 < len(units):
            s_next = scores(*units[i + 1], kv[units[i + 1][0]][0])
        finish(res, nb, s, kv[res][1])


def _dilated_group(q, k, v, bias_band, dil):
    batch, sub_seq, _ = q.shape
    blocks = min(DILATED_BLOCKS_PER_STEP, sub_seq // BLOCK)
    residues = min(DILATED_BLOCKS_PER_STEP // blocks, dil)
    tile = blocks * BLOCK
    width = residues * GROUP_W
    cur = pl.BlockSpec((None, tile, width), lambda b, r, n: (b, n, r))
    prev = pl.BlockSpec((None, BLOCK, width),
                        lambda b, r, n: (b, jnp.maximum(n * blocks - 1, 0), r))
    return pl.pallas_call(
        functools.partial(_dilated_body, blocks=blocks, residues=residues),
        grid=(batch, dil // residues, sub_seq // tile),
        in_specs=[cur, cur, cur, prev, prev, _resident(bias_band.shape)],
        out_specs=[cur, cur],
        out_shape=[jax.ShapeDtypeStruct(q.shape, BF16), jax.ShapeDtypeStruct(q.shape, F32)],
        compiler_params=_compiler_params(("parallel", "parallel", "parallel")),
        name=f"dilated_d{dil}",
    )(q, k, v, k, v, bias_band)


def _bias_band(rel_bias, group, dil, n_back):
    heads = slice(group * HEADS_PER_GROUP, (group + 1) * HEADS_PER_GROUP)
    buckets = _t5_bucket(np.arange(n_back + 1) * dil)
    bias = rel_bias[buckets][:, heads].T.astype(F32) * LOG2E
    period = 2 * BLOCK + 1
    j = BLOCK - np.arange(period)
    local = (j >= 0) & (j <= n_back)
    diag = jnp.where(local[None], bias[:, np.clip(j, 0, n_back)], MASKED)
    band = jnp.tile(diag, (1, BLOCK))[:, :BLOCK * 2 * BLOCK]
    return band.reshape(HEADS_PER_GROUP * BLOCK, 2 * BLOCK)


class _StickStep(NamedTuple):
    q: jax.Array
    k: jax.Array
    v: jax.Array
    mask: Optional[jax.Array]
    carry: Any
    after: Optional[int]


def _neg_abs(x):
    bits = lax.bitcast_convert_type(x, jnp.uint32) | jnp.uint32(0x80000000)
    return lax.bitcast_convert_type(bits, F32)


def _stick_steps(steps, upper):
    z = [lax.dot_general(st.q, st.k, (((1,), (1,)), ((), ())), preferred_element_type=F32)
         for st in steps]
    z = [s if st.mask is None else jnp.where(st.mask, s, MASKED) for s, st in zip(z, steps)]
    soft = [jnp.maximum(s, 0.0) + jnp.log2(1.0 + jnp.exp2(_neg_abs(s))) for s in z]
    log_beta = [s - p for s, p in zip(z, soft)]
    tail = [jnp.dot(p.astype(BF16), upper, preferred_element_type=F32) for p in soft]
    sums = [jnp.sum(p, axis=-1, keepdims=True) for p in soft]
    outs = []
    for i, st in enumerate(steps):
        carry = st.carry if st.after is None else st.carry + sums[st.after]
        w = jnp.exp2(log_beta[i] - (tail[i] + carry))
        outs.append(w.astype(BF16))
    outs = [jnp.dot(w, st.v, preferred_element_type=F32) for w, st in zip(outs, steps)]
    return outs, sums


def _stick_body(q_ref, k_ref, v_ref, upper_ref, o_ref, acc_ref, carry_ref):
    tile = upper_ref.shape[0]
    q_tiles = q_ref.shape[0] // tile
    heads = LANES // HEAD_DIM
    first_tile = pl.program_id(2) * q_tiles
    lane_head = lax.broadcasted_iota(jnp.int32, (tile, LANES), 1) // HEAD_DIM
    row = lax.broadcasted_iota(jnp.int32, (tile, tile), 0)
    col = lax.broadcasted_iota(jnp.int32, (tile, tile), 1)
    strictly_before = col < row
    upper = upper_ref[...]
    chains = []
    for t in range(q_tiles):
        q = q_ref[t * tile:(t + 1) * tile, :]
        for hd in range(heads):
            chains.append((t, hd, jnp.where(lane_head == hd, q, jnp.zeros_like(q))))

    def key_value(kt):
        start = pl.multiple_of(jnp.maximum(kt, 0) * tile, tile)
        v = v_ref[pl.ds(start, tile), :]
        return k_ref[pl.ds(start, tile), :], jnp.where(kt >= 0, v, jnp.zeros_like(v))

    diag = [key_value(first_tile + t) for t in range(q_tiles)]
    prev = [key_value(first_tile + t - 1) for t in range(q_tiles)]
    steps = []
    for t, hd, qh in chains:
        steps.append(_StickStep(qh, *diag[t], strictly_before, 0.0, None))
        steps.append(_StickStep(qh, *prev[t], None, 0.0, len(steps) - 1))
    outs, sums = _stick_steps(steps, upper)
    for c in range(len(chains)):
        acc_ref[c] = outs[2 * c] + outs[2 * c + 1]
        carry_ref[c] = sums[2 * c] + sums[2 * c + 1]

    def any_weight_left():
        return (jnp.min(carry_ref[...]) < F32_EXP2_UNDERFLOW).astype(jnp.int32)

    def more_tiles(state):
        j, live = state
        return jnp.logical_and(j < first_tile + q_tiles - 1, live > 0)

    def earlier(state):
        j, _ = state
        tiles = [key_value(first_tile + t - 1 - j) for t in range(q_tiles)]
        outs, sums = _stick_steps(
            [_StickStep(qh, *tiles[t], None, carry_ref[c], None)
             for c, (t, _, qh) in enumerate(chains)], upper)
        for c in range(len(chains)):
            acc_ref[c] += outs[c]
            carry_ref[c] += sums[c]
        return j + 1, any_weight_left()

    lax.while_loop(more_tiles, earlier, (jnp.int32(1), any_weight_left()))
    for t in range(q_tiles):
        out = acc_ref[t * heads]
        for hd in range(1, heads):
            out = jnp.where(lane_head == hd, acc_ref[t * heads + hd], out)
        o_ref[t * tile:(t + 1) * tile, :] = out.astype(o_ref.dtype)


def _stick_breaking(q, k, v):
    batch, seq, width = q.shape
    tile = STICK_TILE
    rows = STICK_Q_TILES * tile
    chains = STICK_Q_TILES * (LANES // HEAD_DIM)
    j = np.arange(tile)
    upper = jnp.asarray((j[:, None] > j[None, :]), BF16)
    return pl.pallas_call(
        _stick_body,
        grid=(batch, width // LANES, seq // rows),
        in_specs=[pl.BlockSpec((None, rows, LANES), lambda b, p, i: (b, i, p)),
                  pl.BlockSpec((None, seq, LANES), lambda b, p, i: (b, 0, p)),
                  pl.BlockSpec((None, seq, LANES), lambda b, p, i: (b, 0, p)),
                  _resident((tile, tile))],
        out_specs=pl.BlockSpec((None, rows, LANES), lambda b, p, i: (b, i, p)),
        out_shape=jax.ShapeDtypeStruct(q.shape, BF16),
        scratch_shapes=[pltpu.VMEM((chains, tile, LANES), F32),
                        pltpu.VMEM((chains, tile, 1), F32)],
        compiler_params=_compiler_params(("parallel", "parallel", "parallel")),
        name="stick_breaking",
    )(q, k, v, upper)


def _mix_tile(x_ref, gates_ref, yb_ref, group_refs, wa_ref, wb_ref, wo_ref, g_ref, slab_ref):
    tile, d_model = x_ref.shape

    def natural_order(ref, dil):
        if dil == 1:
            return ref[...].astype(F32)
        rows = tile // dil
        for r in range(dil):
            for s in range(GROUP_W // LANES):
                lane0 = r * GROUP_W + s * LANES
                slab_ref[s, pl.ds(r, rows, stride=dil), :] = (
                    ref[:, lane0:lane0 + LANES].astype(F32))
        return jnp.concatenate([slab_ref[s] for s in range(GROUP_W // LANES)], axis=-1)

    def sigmoid(v):
        return 0.5 * jnp.tanh(0.5 * v) + 0.5

    branch_b = jnp.dot(yb_ref[...], wb_ref[...], preferred_element_type=F32)
    outs, lses = [], []
    for g, (_, dil) in enumerate(DILATED_GROUPS):
        outs.append(natural_order(group_refs[2 * g], dil))
        lses.append(natural_order(group_refs[2 * g + 1], dil))
    top = functools.reduce(jnp.maximum, lses)
    wts = [jnp.exp2(l - top) for l in lses]
    ya = sum(w * o for w, o in zip(wts, outs)) / sum(wts)

    branch_a = jnp.dot(ya.astype(BF16), wa_ref[...], preferred_element_type=F32)
    gate_a = gates_ref[:, :d_model].astype(F32)
    gate_b = gates_ref[:, d_model:].astype(F32)
    merged = sigmoid(gate_a) * branch_a + sigmoid(gate_b) * branch_b
    mixed = jnp.dot(merged.astype(BF16), wo_ref[...], preferred_element_type=F32)
    return x_ref[...] + _rms_norm(mixed, g_ref[...])


def _ffn_tile(x_mid, g_pre_ref, w_up_ref, cw_ref, cb_ref, w_down_ref, g_post_ref, out_ref,
              halo_ref, u_ref, act_ref, slab_ref):
    tile, d_model = x_mid.shape
    d_ff = w_down_ref.shape[0]
    seg = tile // SUBLANES
    n_slabs = d_model // LANES
    halo_rows = (CONV_WIDTH - 1) * SUBLANES

    @pl.when(pl.program_id(1) == 0)
    def _():
        halo_ref[...] = jnp.zeros_like(halo_ref)

    for c in range(n_slabs):
        for s in range(SUBLANES):
            slab_ref[c, pl.ds(s, seg, stride=SUBLANES), :] = (
                x_mid[s * seg:(s + 1) * seg, c * LANES:(c + 1) * LANES])
    x = jnp.concatenate([slab_ref[c] for c in range(n_slabs)], axis=-1)
    h = _rms_norm(x, g_pre_ref[...]).astype(BF16)
    sublane = lax.broadcasted_iota(jnp.int32, (SUBLANES, FF_CHUNK), 0)

    def up(c):
        return [jnp.dot(h, w_up_ref[:, col:col + FF_CHUNK], preferred_element_type=F32)
                for col in (c * FF_CHUNK, d_ff + c * FF_CHUNK)]

    def conv(u, col, slot):
        cols = slice(col, col + FF_CHUNK)
        last = u[tile - halo_rows:, :]
        for k in range(CONV_WIDTH - 1):
            group = slice(k * SUBLANES, (k + 1) * SUBLANES)
            mixed = jnp.where(sublane == SUBLANES - 1, halo_ref[group, cols], last[group, :])
            u_ref[slot, group, :] = pltpu.roll(mixed, 1, axis=0)
        u_ref[slot, halo_rows:, :] = u
        halo_ref[:, cols] = last
        y = cb_ref[:, cols]
        for t in range(CONV_WIDTH):
            start = t * SUBLANES
            y = y + cw_ref[t:t + 1, cols] * u_ref[slot, start:start + tile, :]
        return y

    n_chunks = d_ff // FF_CHUNK
    u_next = up(0)
    for c in range(n_chunks):
        u_gate, u_val = u_next
        if c + 1 < n_chunks:
            u_next = up(c + 1)
        gate = conv(u_gate, c * FF_CHUNK, 0)
        val = conv(u_val, d_ff + c * FF_CHUNK, 1)
        act_ref[:, c * FF_CHUNK:(c + 1) * FF_CHUNK] = (
            jax.nn.gelu(gate, approximate=True) * val).astype(BF16)
    part_rows = tile // DOWN_PARTS
    groups = part_rows // SUBLANES
    down = [jnp.dot(act_ref[p * part_rows:(p + 1) * part_rows, :], w_down_ref[...],
                    preferred_element_type=F32) for p in range(DOWN_PARTS)]
    for p, acc in enumerate(down):
        rows = slice(p * part_rows, (p + 1) * part_rows)
        y = x[rows, :] + _rms_norm(acc, g_post_ref[...])
        for c in range(n_slabs):
            slab_ref[c, rows, :] = y[:, c * LANES:(c + 1) * LANES]
        for c in range(n_slabs):
            for s in range(SUBLANES):
                out_ref[s * seg + p * groups:s * seg + (p + 1) * groups,
                        c * LANES:(c + 1) * LANES] = (
                    slab_ref[c, pl.ds(p * part_rows + s, groups, stride=SUBLANES), :])


def _post_attention_body(x_ref, gates_ref, yb_ref, *refs):
    group_refs = refs[:2 * N_GROUPS]
    (wa_ref, wb_ref, wo_ref, g_post_mix_ref, g_pre_ref, w_up_ref, cw_ref, cb_ref, w_down_ref,
     g_post_ref, out_ref, group_slab_ref, halo_ref, u_ref, act_ref,
     slab_ref) = refs[2 * N_GROUPS:]
    x_mid = _mix_tile(x_ref, gates_ref, yb_ref, group_refs, wa_ref, wb_ref, wo_ref,
                      g_post_mix_ref, group_slab_ref)
    _ffn_tile(x_mid, g_pre_ref, w_up_ref, cw_ref, cb_ref, w_down_ref, g_post_ref, out_ref,
              halo_ref, u_ref, act_ref, slab_ref)


def _post_attention(x, gates, yb, groups, mix_consts, ffn_consts):
    batch, seq, d_model = x.shape
    tile = TOKEN_TILE
    d_ff = ffn_consts[4].shape[0]
    row_spec = lambda width: pl.BlockSpec((None, tile, width), lambda b, i: (b, i, 0))
    in_specs = [row_spec(d_model), row_spec(gates.shape[-1]), row_spec(yb.shape[-1])]
    operands = [x, gates, yb]
    for (o, lse), (_, dil) in zip(groups, DILATED_GROUPS):
        for arr in (o, lse):
            in_specs.append(pl.BlockSpec((None, tile // dil, dil * GROUP_W), lambda b, i: (b, i, 0)))
            operands.append(arr)
    for arr in (*mix_consts, *ffn_consts):
        in_specs.append(_resident(arr.shape))
        operands.append(arr)
    halo_rows = (CONV_WIDTH - 1) * SUBLANES
    return pl.pallas_call(
        _post_attention_body,
        grid=(batch, seq // tile),
        in_specs=in_specs,
        out_specs=row_spec(d_model),
        out_shape=jax.ShapeDtypeStruct(x.shape, F32),
        scratch_shapes=[pltpu.VMEM((GROUP_W // LANES, tile, LANES), F32),
                        pltpu.VMEM((halo_rows, 2 * d_ff), F32),
                        pltpu.VMEM((2, halo_rows + tile, FF_CHUNK), F32),
                        pltpu.VMEM((tile, d_ff), BF16),
                        pltpu.VMEM((d_model // LANES, tile, LANES), F32)],
        compiler_params=_compiler_params(("arbitrary", "arbitrary")),
        name="post_attention",
    )(*operands)


def kernel(x, g_pre_mix, w_in, rel_bias, w_branch_a, w_branch_b, w_out, g_post_mix, g_pre_ffn,
           w_up, conv_w, conv_b, w_down, g_post_ffn):
    depth = w_in.shape[0]
    for l in range(depth):
        proj = _in_proj(x, g_pre_mix[l][None], w_in[l].astype(BF16))
        qb, kb, vb, gates = proj[3 * N_GROUPS:]
        groups = []
        for g, (window, dil) in enumerate(DILATED_GROUPS):
            q, k, v = proj[3 * g:3 * g + 3]
            band = _bias_band(rel_bias, g, dil, window // dil)
            groups.append(_dilated_group(q, k, v, band, dil))
        yb = _stick_breaking(qb, kb, vb)
        mix_consts = (w_branch_a[l].astype(BF16), w_branch_b[l].astype(BF16),
                      w_out[l].astype(BF16), g_post_mix[l][None])
        ffn_consts = (g_pre_ffn[l][None], w_up[l].astype(BF16), conv_w[l], conv_b[l][None],
                      w_down[l].astype(BF16), g_post_ffn[l][None])
        x = _post_attention(x, gates, yb, groups, mix_consts, ffn_consts)
    return x
```

```python
import functools
import math
from typing import Any, NamedTuple, Optional

import numpy as np
import jax
import jax.numpy as jnp
from jax import lax
from jax.experimental import pallas as pl
from jax.experimental.pallas import tpu as pltpu

F32 = jnp.float32
BF16 = jnp.bfloat16

HEAD_DIM = 64
DILATED_GROUPS = ((128, 1), (512, 4), (2048, 16))
HEADS_PER_GROUP = 4
N_GROUPS = len(DILATED_GROUPS)
N_HEADS_A = HEADS_PER_GROUP * N_GROUPS
N_HEADS_B = 8
GROUP_W = HEADS_PER_GROUP * HEAD_DIM
A_QKV = N_HEADS_A * HEAD_DIM
B_QKV = N_HEADS_B * HEAD_DIM
BLOCK = 128
NUM_BUCKETS = 32
MAX_DISTANCE = 2048
CONV_WIDTH = 3
EPS = 1e-6
QK_SCALE = 1.0 / math.sqrt(HEAD_DIM)
LOG2E = math.log2(math.e)
MASKED = -1e30
F32_EXP2_UNDERFLOW = 150.1

LANES = 128
SUBLANES = 8
STRIDE_STEP = 4
VMEM_LIMIT_BYTES = 56 * 1024 * 1024

TOKEN_TILE = 512
IN_PROJ_TILE = 1024
GATE_CHUNK = 512
DILATED_BLOCKS_PER_STEP = 16
STICK_Q_TILES = 8
STICK_TILE = 256
FF_CHUNK = 256
DOWN_PARTS = 2


def _t5_bucket(dist):
    max_exact = NUM_BUCKETS // 2
    n = np.asarray(dist, dtype=np.float32)
    large = max_exact + (np.log(np.maximum(n, 1.0) / max_exact)
                         / np.log(MAX_DISTANCE / max_exact)
                         * (NUM_BUCKETS - max_exact)).astype(np.int32)
    large = np.minimum(large, NUM_BUCKETS - 1)
    return np.where(n < max_exact, n.astype(np.int32), large).astype(np.int32)


def _rms_norm(x, g):
    return x * lax.rsqrt(jnp.mean(x * x, axis=-1, keepdims=True) + EPS) * g


def _compiler_params(semantics):
    return pltpu.CompilerParams(dimension_semantics=semantics, vmem_limit_bytes=VMEM_LIMIT_BYTES)


def _resident(shape):
    return pl.BlockSpec(shape, lambda *_: (0,) * len(shape), pipeline_mode=pl.Buffered(1))


def _in_proj_body(x_ref, g_ref, w_ref, *refs):
    a_refs = refs[:3 * N_GROUPS]
    qb_ref, kb_ref, vb_ref, gates_ref, slab_ref = refs[3 * N_GROUPS:]
    tile = x_ref.shape[0]
    h = _rms_norm(x_ref[...], g_ref[...]).astype(BF16)

    def proj(col, width):
        return jnp.dot(h, w_ref[:, col:col + width], preferred_element_type=F32)

    for t in range(3):
        scale = QK_SCALE * LOG2E if t == 0 else 1.0
        for g, (_, dil) in enumerate(DILATED_GROUPS):
            acc = proj(t * A_QKV + g * GROUP_W, GROUP_W) * scale
            out_ref = a_refs[3 * g + t]
            if dil == 1:
                out_ref[...] = acc.astype(BF16)
                continue
            rows = tile // dil
            for s in range(GROUP_W // LANES):
                slab_ref[s] = acc[:, s * LANES:(s + 1) * LANES]
            for r in range(dil):
                for s in range(GROUP_W // LANES):
                    lane0 = r * GROUP_W + s * LANES
                    out_ref[:, lane0:lane0 + LANES] = (
                        slab_ref[s, pl.ds(r, rows, stride=dil), :].astype(BF16))

    base = 3 * A_QKV
    qb_ref[...] = (proj(base, B_QKV) * (QK_SCALE * LOG2E)).astype(BF16)
    kb_ref[...] = proj(base + B_QKV, B_QKV).astype(BF16)
    vb_ref[...] = proj(base + 2 * B_QKV, B_QKV).astype(BF16)
    base += 3 * B_QKV
    gate_w = gates_ref.shape[1]
    for c in range(0, gate_w, GATE_CHUNK):
        gates_ref[:, c:c + GATE_CHUNK] = proj(base + c, GATE_CHUNK).astype(BF16)


def _in_proj(x, g, w):
    batch, seq, d_model = x.shape
    tile = IN_PROJ_TILE
    n_in = w.shape[1]
    gate_w = n_in - 3 * A_QKV - 3 * B_QKV
    out_shape, out_specs = [], []
    for _, dil in DILATED_GROUPS:
        for _ in range(3):
            out_shape.append(jax.ShapeDtypeStruct((batch, seq // dil, dil * GROUP_W), BF16))
            out_specs.append(pl.BlockSpec((None, tile // dil, dil * GROUP_W), lambda b, i: (b, i, 0)))
    for _ in range(3):
        out_shape.append(jax.ShapeDtypeStruct((batch, seq, B_QKV), BF16))
        out_specs.append(pl.BlockSpec((None, tile, B_QKV), lambda b, i: (b, i, 0)))
    out_shape.append(jax.ShapeDtypeStruct((batch, seq, gate_w), BF16))
    out_specs.append(pl.BlockSpec((None, tile, gate_w), lambda b, i: (b, i, 0)))
    return pl.pallas_call(
        _in_proj_body,
        grid=(batch, seq // tile),
        in_specs=[pl.BlockSpec((None, tile, d_model), lambda b, i: (b, i, 0)),
                  _resident((1, d_model)),
                  _resident((d_model, n_in))],
        out_specs=out_specs,
        out_shape=out_shape,
        scratch_shapes=[pltpu.VMEM((GROUP_W // LANES, tile, LANES), F32)],
        compiler_params=_compiler_params(("parallel", "parallel")),
        name="in_proj",
    )(x, g, w)


def _dilated_body(q_ref, k_ref, v_ref, kp_ref, vp_ref, bias_ref, o_ref, lse_ref, *,
                  blocks, residues):
    heads = HEADS_PER_GROUP
    missing_cols = jnp.where(pl.program_id(2) == 0, BLOCK, 0)
    lane_head = lax.broadcasted_iota(jnp.int32, (BLOCK, GROUP_W), 1) // HEAD_DIM
    key_col = lax.broadcasted_iota(jnp.int32, (heads * BLOCK, 2 * BLOCK), 1)
    bias = bias_ref[...]

    def keys_values(res):
        lanes = slice(res * GROUP_W, (res + 1) * GROUP_W)
        k_all = jnp.concatenate([kp_ref[:, lanes], k_ref[:, lanes]], axis=0)
        v_all = jnp.concatenate([vp_ref[:, lanes], v_ref[:, lanes]], axis=0)
        v_lane_head = lax.broadcasted_iota(jnp.int32, v_all.shape, 1) // HEAD_DIM
        v_heads = [jnp.where(v_lane_head == hd, v_all, jnp.zeros_like(v_all))
                   for hd in range(heads)]
        return k_all, v_heads

    def scores(res, nb, k_all):
        q = q_ref[nb * BLOCK:(nb + 1) * BLOCK, res * GROUP_W:(res + 1) * GROUP_W]
        q_stack = jnp.concatenate(
            [jnp.where(lane_head == hd, q, jnp.zeros_like(q)) for hd in range(heads)], axis=0)
        s = lax.dot_general(q_stack, k_all[nb * BLOCK:(nb + 2) * BLOCK], (((1,), (1,)), ((), ())),
                            preferred_element_type=F32) + bias
        if nb == 0:
            s = jnp.where(key_col < missing_cols, MASKED, s)
        return s

    def finish(res, nb, s, v_heads):
        mx = jnp.max(s, axis=-1, keepdims=True)
        e = jnp.exp2(s - mx)
        den = jnp.sum(e, axis=-1, keepdims=True)
        p = (e * (1.0 / den)).astype(BF16)
        p_cat = jnp.concatenate([p[hd * BLOCK:(hd + 1) * BLOCK] for hd in range(heads)], axis=1)
        v_stack = jnp.concatenate([vh[nb * BLOCK:(nb + 2) * BLOCK] for vh in v_heads], axis=0)
        o = jnp.dot(p_cat, v_stack, preferred_element_type=F32)
        lse = mx + jnp.log2(den)
        lse_full = jnp.broadcast_to(lse[:BLOCK], (BLOCK, GROUP_W))
        for hd in range(1, heads):
            lse_full = jnp.where(lane_head == hd, lse[hd * BLOCK:(hd + 1) * BLOCK], lse_full)
        rows = slice(nb * BLOCK, (nb + 1) * BLOCK)
        lanes = slice(res * GROUP_W, (res + 1) * GROUP_W)
        o_ref[rows, lanes] = o.astype(o_ref.dtype)
        lse_ref[rows, lanes] = lse_full

    units = [(res, nb) for res in range(residues) for nb in range(blocks)]
    kv = {res: keys_values(res) for res in range(residues)}
    s_next = scores(*units[0], kv[units[0][0]][0])
    for i, (res, nb) in enumerate(units):
        s = s_next
        if i + 1 < len(units):
            s_next = scores(*units[i + 1], kv[units[i + 1][0]][0])
        finish(res, nb, s, kv[res][1])


def _dilated_group(q, k, v, bias_band, dil):
    batch, sub_seq, _ = q.shape
    blocks = min(DILATED_BLOCKS_PER_STEP, sub_seq // BLOCK)
    residues = min(DILATED_BLOCKS_PER_STEP // blocks, dil)
    tile = blocks * BLOCK
    width = residues * GROUP_W
    cur = pl.BlockSpec((None, tile, width), lambda b, r, n: (b, n, r))
    prev = pl.BlockSpec((None, BLOCK, width),
                        lambda b, r, n: (b, jnp.maximum(n * blocks - 1, 0), r))
    return pl.pallas_call(
        functools.partial(_dilated_body, blocks=blocks, residues=residues),
        grid=(batch, dil // residues, sub_seq // tile),
        in_specs=[cur, cur, cur, prev, prev, _resident(bias_band.shape)],
        out_specs=[cur, cur],
        out_shape=[jax.ShapeDtypeStruct(q.shape, BF16), jax.ShapeDtypeStruct(q.shape, F32)],
        compiler_params=_compiler_params(("parallel", "parallel", "parallel")),
        name=f"dilated_d{dil}",
    )(q, k, v, k, v, bias_band)


def _bias_band(rel_bias, group, dil, n_back):
    heads = slice(group * HEADS_PER_GROUP, (group + 1) * HEADS_PER_GROUP)
    buckets = _t5_bucket(np.arange(n_back + 1) * dil)
    bias = rel_bias[buckets][:, heads].T.astype(F32) * LOG2E
    period = 2 * BLOCK + 1
    j = BLOCK - np.arange(period)
    local = (j >= 0) & (j <= n_back)
    diag = jnp.where(local[None], bias[:, np.clip(j, 0, n_back)], MASKED)
    band = jnp.tile(diag, (1, BLOCK))[:, :BLOCK * 2 * BLOCK]
    return band.reshape(HEADS_PER_GROUP * BLOCK, 2 * BLOCK)


class _StickStep(NamedTuple):
    q: jax.Array
    k: jax.Array
    v: jax.Array
    mask: Optional[jax.Array]
    carry: Any
    after: Optional[int]


def _neg_abs(x):
    bits = lax.bitcast_convert_type(x, jnp.uint32) | jnp.uint32(0x80000000)
    return lax.bitcast_convert_type(bits, F32)


def _stick_steps(steps, upper):
    z = [lax.dot_general(st.q, st.k, (((1,), (1,)), ((), ())), preferred_element_type=F32)
         for st in steps]
    z = [s if st.mask is None else jnp.where(st.mask, s, MASKED) for s, st in zip(z, steps)]
    soft = [jnp.maximum(s, 0.0) + jnp.log2(1.0 + jnp.exp2(_neg_abs(s))) for s in z]
    log_beta = [s - p for s, p in zip(z, soft)]
    tail = [jnp.dot(p.astype(BF16), upper, preferred_element_type=F32) for p in soft]
    sums = [jnp.sum(p, axis=-1, keepdims=True) for p in soft]
    outs = []
    for i, st in enumerate(steps):
        carry = st.carry if st.after is None else st.carry + sums[st.after]
        w = jnp.exp2(log_beta[i] - (tail[i] + carry))
        outs.append(w.astype(BF16))
    outs = [jnp.dot(w, st.v, preferred_element_type=F32) for w, st in zip(outs, steps)]
    return outs, sums


def _stick_body(q_ref, k_ref, v_ref, upper_ref, o_ref, acc_ref, carry_ref):
    tile = upper_ref.shape[0]
    q_tiles = q_ref.shape[0] // tile
    heads = LANES // HEAD_DIM
    first_tile = pl.program_id(2) * q_tiles
    lane_head = lax.broadcasted_iota(jnp.int32, (tile, LANES), 1) // HEAD_DIM
    row = lax.broadcasted_iota(jnp.int32, (tile, tile), 0)
    col = lax.broadcasted_iota(jnp.int32, (tile, tile), 1)
    strictly_before = col < row
    upper = upper_ref[...]
    chains = []
    for t in range(q_tiles):
        q = q_ref[t * tile:(t + 1) * tile, :]
        for hd in range(heads):
            chains.append((t, hd, jnp.where(lane_head == hd, q, jnp.zeros_like(q))))

    def key_value(kt):
        start = pl.multiple_of(jnp.maximum(kt, 0) * tile, tile)
        v = v_ref[pl.ds(start, tile), :]
        return k_ref[pl.ds(start, tile), :], jnp.where(kt >= 0, v, jnp.zeros_like(v))

    diag = [key_value(first_tile + t) for t in range(q_tiles)]
    prev = [key_value(first_tile + t - 1) for t in range(q_tiles)]
    steps = []
    for t, hd, qh in chains:
        steps.append(_StickStep(qh, *diag[t], strictly_before, 0.0, None))
        steps.append(_StickStep(qh, *prev[t], None, 0.0, len(steps) - 1))
    outs, sums = _stick_steps(steps, upper)
    for c in range(len(chains)):
        acc_ref[c] = outs[2 * c] + outs[2 * c + 1]
        carry_ref[c] = sums[2 * c] + sums[2 * c + 1]

    def any_weight_left():
        return (jnp.min(carry_ref[...]) < F32_EXP2_UNDERFLOW).astype(jnp.int32)

    def more_tiles(state):
        j, live = state
        return jnp.logical_and(j < first_tile + q_tiles - 1, live > 0)

    def earlier(state):
        j, _ = state
        tiles = [key_value(first_tile + t - 1 - j) for t in range(q_tiles)]
        outs, sums = _stick_steps(
            [_StickStep(qh, *tiles[t], None, carry_ref[c], None)
             for c, (t, _, qh) in enumerate(chains)], upper)
        for c in range(len(chains)):
            acc_ref[c] += outs[c]
            carry_ref[c] += sums[c]
        return j + 1, any_weight_left()

    lax.while_loop(more_tiles, earlier, (jnp.int32(1), any_weight_left()))
    for t in range(q_tiles):
        out = acc_ref[t * heads]
        for hd in range(1, heads):
            out = jnp.where(lane_head == hd, acc_ref[t * heads + hd], out)
        o_ref[t * tile:(t + 1) * tile, :] = out.astype(o_ref.dtype)


def _stick_breaking(q, k, v):
    batch, seq, width = q.shape
    tile = STICK_TILE
    rows = STICK_Q_TILES * tile
    chains = STICK_Q_TILES * (LANES // HEAD_DIM)
    j = np.arange(tile)
    upper = jnp.asarray((j[:, None] > j[None, :]), BF16)
    return pl.pallas_call(
        _stick_body,
        grid=(batch, width // LANES, seq // rows),
        in_specs=[pl.BlockSpec((None, rows, LANES), lambda b, p, i: (b, i, p)),
                  pl.BlockSpec((None, seq, LANES), lambda b, p, i: (b, 0, p)),
                  pl.BlockSpec((None, seq, LANES), lambda b, p, i: (b, 0, p)),
                  _resident((tile, tile))],
        out_specs=pl.BlockSpec((None, rows, LANES), lambda b, p, i: (b, i, p)),
        out_shape=jax.ShapeDtypeStruct(q.shape, BF16),
        scratch_shapes=[pltpu.VMEM((chains, tile, LANES), F32),
                        pltpu.VMEM((chains, tile, 1), F32)],
        compiler_params=_compiler_params(("parallel", "parallel", "parallel")),
        name="stick_breaking",
    )(q, k, v, upper)


def _mix_tile(x_ref, gates_ref, yb_ref, group_refs, wa_ref, wb_ref, wo_ref, g_ref, slab_ref,
              mid_ref):
    tile, d_model = x_ref.shape

    def natural_order(ref, dil):
        if dil == 1:
            return ref[...].astype(F32)
        rows = tile // dil
        slabs = GROUP_W // LANES
        if dil <= SUBLANES:
            for r in range(dil):
                for s in range(slabs):
                    lane0 = r * GROUP_W + s * LANES
                    slab_ref[s, pl.ds(r, rows, stride=dil), :] = (
                        ref[:, lane0:lane0 + LANES].astype(F32))
        else:
            outer = dil // STRIDE_STEP
            for hi in range(outer):
                for lo in range(STRIDE_STEP):
                    for s in range(slabs):
                        lane0 = (hi * STRIDE_STEP + lo) * GROUP_W + s * LANES
                        mid_ref[lo * slabs + s, pl.ds(hi, rows, stride=outer), :] = (
                            ref[:, lane0:lane0 + LANES].astype(F32))
            for lo in range(STRIDE_STEP):
                for s in range(slabs):
                    slab_ref[s, pl.ds(lo, rows * outer, stride=STRIDE_STEP), :] = (
                        mid_ref[lo * slabs + s])
        return jnp.concatenate([slab_ref[s] for s in range(slabs)], axis=-1)

    def sigmoid(v):
        return 0.5 * jnp.tanh(0.5 * v) + 0.5

    branch_b = jnp.dot(yb_ref[...], wb_ref[...], preferred_element_type=F32)
    outs, lses = [], []
    for g, (_, dil) in enumerate(DILATED_GROUPS):
        outs.append(natural_order(group_refs[2 * g], dil))
        lses.append(natural_order(group_refs[2 * g + 1], dil))
    top = functools.reduce(jnp.maximum, lses)
    wts = [jnp.exp2(l - top) for l in lses]
    ya = sum(w * o for w, o in zip(wts, outs)) / sum(wts)

    branch_a = jnp.dot(ya.astype(BF16), wa_ref[...], preferred_element_type=F32)
    gate_a = gates_ref[:, :d_model].astype(F32)
    gate_b = gates_ref[:, d_model:].astype(F32)
    merged = sigmoid(gate_a) * branch_a + sigmoid(gate_b) * branch_b
    mixed = jnp.dot(merged.astype(BF16), wo_ref[...], preferred_element_type=F32)
    return x_ref[...] + _rms_norm(mixed, g_ref[...])


def _ffn_tile(x_mid, g_pre_ref, w_up_ref, cw_ref, cb_ref, w_down_ref, g_post_ref, out_ref,
              halo_ref, u_ref, act_ref, slab_ref):
    tile, d_model = x_mid.shape
    d_ff = w_down_ref.shape[0]
    seg = tile // SUBLANES
    n_slabs = d_model // LANES
    halo_rows = (CONV_WIDTH - 1) * SUBLANES

    @pl.when(pl.program_id(1) == 0)
    def _():
        halo_ref[...] = jnp.zeros_like(halo_ref)

    for c in range(n_slabs):
        for s in range(SUBLANES):
            slab_ref[c, pl.ds(s, seg, stride=SUBLANES), :] = (
                x_mid[s * seg:(s + 1) * seg, c * LANES:(c + 1) * LANES])
    x = jnp.concatenate([slab_ref[c] for c in range(n_slabs)], axis=-1)
    h = _rms_norm(x, g_pre_ref[...]).astype(BF16)
    sublane = lax.broadcasted_iota(jnp.int32, (SUBLANES, FF_CHUNK), 0)

    def up(c):
        return [jnp.dot(h, w_up_ref[:, col:col + FF_CHUNK], preferred_element_type=F32)
                for col in (c * FF_CHUNK, d_ff + c * FF_CHUNK)]

    def conv(u, col, slot):
        cols = slice(col, col + FF_CHUNK)
        last = u[tile - halo_rows:, :]
        for k in range(CONV_WIDTH - 1):
            group = slice(k * SUBLANES, (k + 1) * SUBLANES)
            mixed = jnp.where(sublane == SUBLANES - 1, halo_ref[group, cols], last[group, :])
            u_ref[slot, group, :] = pltpu.roll(mixed, 1, axis=0)
        u_ref[slot, halo_rows:, :] = u
        halo_ref[:, cols] = last
        y = cb_ref[:, cols]
        for t in range(CONV_WIDTH):
            start = t * SUBLANES
            y = y + cw_ref[t:t + 1, cols] * u_ref[slot, start:start + tile, :]
        return y

    n_chunks = d_ff // FF_CHUNK
    u_next = up(0)
    for c in range(n_chunks):
        u_gate, u_val = u_next
        if c + 1 < n_chunks:
            u_next = up(c + 1)
        gate = conv(u_gate, c * FF_CHUNK, 0)
        val = conv(u_val, d_ff + c * FF_CHUNK, 1)
        act_ref[:, c * FF_CHUNK:(c + 1) * FF_CHUNK] = (
            jax.nn.gelu(gate, approximate=True) * val).astype(BF16)
    part_rows = tile // DOWN_PARTS
    groups = part_rows // SUBLANES
    down = [jnp.dot(act_ref[p * part_rows:(p + 1) * part_rows, :], w_down_ref[...],
                    preferred_element_type=F32) for p in range(DOWN_PARTS)]
    for p, acc in enumerate(down):
        rows = slice(p * part_rows, (p + 1) * part_rows)
        y = x[rows, :] + _rms_norm(acc, g_post_ref[...])
        for c in range(n_slabs):
            slab_ref[c, rows, :] = y[:, c * LANES:(c + 1) * LANES]
        for c in range(n_slabs):
            for s in range(SUBLANES):
                out_ref[s * seg + p * groups:s * seg + (p + 1) * groups,
                        c * LANES:(c + 1) * LANES] = (
                    slab_ref[c, pl.ds(p * part_rows + s, groups, stride=SUBLANES), :])


def _post_attention_body(x_ref, gates_ref, yb_ref, *refs):
    group_refs = refs[:2 * N_GROUPS]
    (wa_ref, wb_ref, wo_ref, g_post_mix_ref, g_pre_ref, w_up_ref, cw_ref, cb_ref, w_down_ref,
     g_post_ref, out_ref, group_slab_ref, group_mid_ref, halo_ref, u_ref, act_ref,
     slab_ref) = refs[2 * N_GROUPS:]
    x_mid = _mix_tile(x_ref, gates_ref, yb_ref, group_refs, wa_ref, wb_ref, wo_ref,
                      g_post_mix_ref, group_slab_ref, group_mid_ref)
    _ffn_tile(x_mid, g_pre_ref, w_up_ref, cw_ref, cb_ref, w_down_ref, g_post_ref, out_ref,
              halo_ref, u_ref, act_ref, slab_ref)


def _post_attention(x, gates, yb, groups, mix_consts, ffn_consts):
    batch, seq, d_model = x.shape
    tile = TOKEN_TILE
    d_ff = ffn_consts[4].shape[0]
    row_spec = lambda width: pl.BlockSpec((None, tile, width), lambda b, i: (b, i, 0))
    in_specs = [row_spec(d_model), row_spec(gates.shape[-1]), row_spec(yb.shape[-1])]
    operands = [x, gates, yb]
    for (o, lse), (_, dil) in zip(groups, DILATED_GROUPS):
        for arr in (o, lse):
            in_specs.append(pl.BlockSpec((None, tile // dil, dil * GROUP_W), lambda b, i: (b, i, 0)))
            operands.append(arr)
    for arr in (*mix_consts, *ffn_consts):
        in_specs.append(_resident(arr.shape))
        operands.append(arr)
    halo_rows = (CONV_WIDTH - 1) * SUBLANES
    return pl.pallas_call(
        _post_attention_body,
        grid=(batch, seq // tile),
        in_specs=in_specs,
        out_specs=row_spec(d_model),
        out_shape=jax.ShapeDtypeStruct(x.shape, F32),
        scratch_shapes=[pltpu.VMEM((GROUP_W // LANES, tile, LANES), F32),
                        pltpu.VMEM((STRIDE_STEP * GROUP_W // LANES, tile // STRIDE_STEP, LANES), F32),
                        pltpu.VMEM((halo_rows, 2 * d_ff), F32),
                        pltpu.VMEM((2, halo_rows + tile, FF_CHUNK), F32),
                        pltpu.VMEM((tile, d_ff), BF16),
                        pltpu.VMEM((d_model // LANES, tile, LANES), F32)],
        compiler_params=_compiler_params(("arbitrary", "arbitrary")),
        name="post_attention",
    )(*operands)


def kernel(x, g_pre_mix, w_in, rel_bias, w_branch_a, w_branch_b, w_out, g_post_mix, g_pre_ffn,
           w_up, conv_w, conv_b, w_down, g_post_ffn):
    depth = w_in.shape[0]
    for l in range(depth):
        proj = _in_proj(x, g_pre_mix[l][None], w_in[l].astype(BF16))
        qb, kb, vb, gates = proj[3 * N_GROUPS:]
        groups = []
        for g, (window, dil) in enumerate(DILATED_GROUPS):
            q, k, v = proj[3 * g:3 * g + 3]
            band = _bias_band(rel_bias, g, dil, window // dil)
            groups.append(_dilated_group(q, k, v, band, dil))
        yb = _stick_breaking(qb, kb, vb)
        mix_consts = (w_branch_a[l].astype(BF16), w_branch_b[l].astype(BF16),
                      w_out[l].astype(BF16), g_post_mix[l][None])
        ffn_consts = (g_pre_ffn[l][None], w_up[l].astype(BF16), conv_w[l], conv_b[l][None],
                      w_down[l].astype(BF16), g_post_ffn[l][None])
        x = _post_attention(x, gates, yb, groups, mix_consts, ffn_consts)
    return x
```

```python
import functools
import math
from typing import Any, NamedTuple, Optional

import numpy as np
import jax
import jax.numpy as jnp
from jax import lax
from jax.experimental import pallas as pl
from jax.experimental.pallas import tpu as pltpu

F32 = jnp.float32
BF16 = jnp.bfloat16

HEAD_DIM = 64
DILATED_GROUPS = ((128, 1), (512, 4), (2048, 16))
HEADS_PER_GROUP = 4
N_GROUPS = len(DILATED_GROUPS)
N_HEADS_A = HEADS_PER_GROUP * N_GROUPS
N_HEADS_B = 8
GROUP_W = HEADS_PER_GROUP * HEAD_DIM
A_QKV = N_HEADS_A * HEAD_DIM
B_QKV = N_HEADS_B * HEAD_DIM
BLOCK = 128
NUM_BUCKETS = 32
MAX_DISTANCE = 2048
CONV_WIDTH = 3
EPS = 1e-6
QK_SCALE = 1.0 / math.sqrt(HEAD_DIM)
LOG2E = math.log2(math.e)
MASKED = -1e30
F32_EXP2_UNDERFLOW = 150.1

LANES = 128
SUBLANES = 8
STRIDE_STEP = 4
VMEM_LIMIT_BYTES = 56 * 1024 * 1024

TOKEN_TILE = 512
IN_PROJ_TILE = 1024
GATE_CHUNK = 512
DILATED_BLOCKS_PER_STEP = 16
STICK_Q_TILES = 8
STICK_TILE = 256
FF_CHUNK = 256
DOWN_PARTS = 2


def _t5_bucket(dist):
    max_exact = NUM_BUCKETS // 2
    n = np.asarray(dist, dtype=np.float32)
    large = max_exact + (np.log(np.maximum(n, 1.0) / max_exact)
                         / np.log(MAX_DISTANCE / max_exact)
                         * (NUM_BUCKETS - max_exact)).astype(np.int32)
    large = np.minimum(large, NUM_BUCKETS - 1)
    return np.where(n < max_exact, n.astype(np.int32), large).astype(np.int32)


def _rms_norm(x, g):
    return x * lax.rsqrt(jnp.mean(x * x, axis=-1, keepdims=True) + EPS) * g


def _compiler_params(semantics):
    return pltpu.CompilerParams(dimension_semantics=semantics, vmem_limit_bytes=VMEM_LIMIT_BYTES)


def _resident(shape):
    return pl.BlockSpec(shape, lambda *_: (0,) * len(shape), pipeline_mode=pl.Buffered(1))


def _in_proj_body(x_ref, g_ref, w_ref, *refs):
    a_refs = refs[:3 * N_GROUPS]
    qb_ref, kb_ref, vb_ref, gates_ref, slab_ref = refs[3 * N_GROUPS:]
    tile = x_ref.shape[0]
    h = _rms_norm(x_ref[...], g_ref[...]).astype(BF16)

    def proj(col, width):
        return jnp.dot(h, w_ref[:, col:col + width], preferred_element_type=F32)

    for t in range(3):
        scale = QK_SCALE * LOG2E if t == 0 else 1.0
        for g, (_, dil) in enumerate(DILATED_GROUPS):
            acc = proj(t * A_QKV + g * GROUP_W, GROUP_W) * scale
            out_ref = a_refs[3 * g + t]
            if dil == 1:
                out_ref[...] = acc.astype(BF16)
                continue
            rows = tile // dil
            for s in range(GROUP_W // LANES):
                slab_ref[s] = acc[:, s * LANES:(s + 1) * LANES]
            for r in range(dil):
                for s in range(GROUP_W // LANES):
                    lane0 = r * GROUP_W + s * LANES
                    out_ref[:, lane0:lane0 + LANES] = (
                        slab_ref[s, pl.ds(r, rows, stride=dil), :].astype(BF16))

    base = 3 * A_QKV
    qb_ref[...] = (proj(base, B_QKV) * (QK_SCALE * LOG2E)).astype(BF16)
    kb_ref[...] = proj(base + B_QKV, B_QKV).astype(BF16)
    vb_ref[...] = proj(base + 2 * B_QKV, B_QKV).astype(BF16)
    base += 3 * B_QKV
    gate_w = gates_ref.shape[1]
    for c in range(0, gate_w, GATE_CHUNK):
        gates_ref[:, c:c + GATE_CHUNK] = proj(base + c, GATE_CHUNK).astype(BF16)


def _in_proj(x, g, w):
    batch, seq, d_model = x.shape
    tile = IN_PROJ_TILE
    n_in = w.shape[1]
    gate_w = n_in - 3 * A_QKV - 3 * B_QKV
    out_shape, out_specs = [], []
    for _, dil in DILATED_GROUPS:
        for _ in range(3):
            out_shape.append(jax.ShapeDtypeStruct((batch, seq // dil, dil * GROUP_W), BF16))
            out_specs.append(pl.BlockSpec((None, tile // dil, dil * GROUP_W), lambda b, i: (b, i, 0)))
    for _ in range(3):
        out_shape.append(jax.ShapeDtypeStruct((batch, seq, B_QKV), BF16))
        out_specs.append(pl.BlockSpec((None, tile, B_QKV), lambda b, i: (b, i, 0)))
    out_shape.append(jax.ShapeDtypeStruct((batch, seq, gate_w), BF16))
    out_specs.append(pl.BlockSpec((None, tile, gate_w), lambda b, i: (b, i, 0)))
    return pl.pallas_call(
        _in_proj_body,
        grid=(batch, seq // tile),
        in_specs=[pl.BlockSpec((None, tile, d_model), lambda b, i: (b, i, 0)),
                  _resident((1, d_model)),
                  _resident((d_model, n_in))],
        out_specs=out_specs,
        out_shape=out_shape,
        scratch_shapes=[pltpu.VMEM((GROUP_W // LANES, tile, LANES), F32)],
        compiler_params=_compiler_params(("parallel", "parallel")),
        name="in_proj",
    )(x, g, w)


def _dilated_body(q_ref, k_ref, v_ref, kp_ref, vp_ref, bias_ref, o_ref, lse_ref, *,
                  blocks, residues):
    heads = HEADS_PER_GROUP
    missing_cols = jnp.where(pl.program_id(2) == 0, BLOCK, 0)
    lane_head = lax.broadcasted_iota(jnp.int32, (BLOCK, GROUP_W), 1) // HEAD_DIM
    key_col = lax.broadcasted_iota(jnp.int32, (heads * BLOCK, 2 * BLOCK), 1)
    bias = bias_ref[...]

    def keys_values(res):
        lanes = slice(res * GROUP_W, (res + 1) * GROUP_W)
        k_all = jnp.concatenate([kp_ref[:, lanes], k_ref[:, lanes]], axis=0)
        v_all = jnp.concatenate([vp_ref[:, lanes], v_ref[:, lanes]], axis=0)
        v_lane_head = lax.broadcasted_iota(jnp.int32, v_all.shape, 1) // HEAD_DIM
        v_heads = [jnp.where(v_lane_head == hd, v_all, jnp.zeros_like(v_all))
                   for hd in range(heads)]
        return k_all, v_heads

    def scores(res, nb, k_all):
        q = q_ref[nb * BLOCK:(nb + 1) * BLOCK, res * GROUP_W:(res + 1) * GROUP_W]
        q_stack = jnp.concatenate(
            [jnp.where(lane_head == hd, q, jnp.zeros_like(q)) for hd in range(heads)], axis=0)
        s = lax.dot_general(q_stack, k_all[nb * BLOCK:(nb + 2) * BLOCK], (((1,), (1,)), ((), ())),
                            preferred_element_type=F32) + bias
        if nb == 0:
            s = jnp.where(key_col < missing_cols, MASKED, s)
        return s

    def finish(res, nb, s, v_heads):
        mx = jnp.max(s, axis=-1, keepdims=True)
        e = jnp.exp2(s - mx)
        den = jnp.sum(e, axis=-1, keepdims=True)
        p = (e * (1.0 / den)).astype(BF16)
        p_cat = jnp.concatenate([p[hd * BLOCK:(hd + 1) * BLOCK] for hd in range(heads)], axis=1)
        v_stack = jnp.concatenate([vh[nb * BLOCK:(nb + 2) * BLOCK] for vh in v_heads], axis=0)
        o = jnp.dot(p_cat, v_stack, preferred_element_type=F32)
        lse = mx + jnp.log2(den)
        lse_full = jnp.broadcast_to(lse[:BLOCK], (BLOCK, GROUP_W))
        for hd in range(1, heads):
            lse_full = jnp.where(lane_head == hd, lse[hd * BLOCK:(hd + 1) * BLOCK], lse_full)
        rows = slice(nb * BLOCK, (nb + 1) * BLOCK)
        lanes = slice(res * GROUP_W, (res + 1) * GROUP_W)
        o_ref[rows, lanes] = o.astype(o_ref.dtype)
        lse_ref[rows, lanes] = lse_full

    units = [(res, nb) for res in range(residues) for nb in range(blocks)]
    kv = {res: keys_values(res) for res in range(residues)}
    s_next = scores(*units[0], kv[units[0][0]][0])
    for i, (res, nb) in enumerate(units):
        s = s_next
        if i + 1 < len(units):
            s_next = scores(*units[i + 1], kv[units[i + 1][0]][0])
        finish(res, nb, s, kv[res][1])


def _dilated_group(q, k, v, bias_band, dil):
    batch, sub_seq, _ = q.shape
    blocks = min(DILATED_BLOCKS_PER_STEP, sub_seq // BLOCK)
    residues = min(DILATED_BLOCKS_PER_STEP // blocks, dil)
    tile = blocks * BLOCK
    width = residues * GROUP_W
    cur = pl.BlockSpec((None, tile, width), lambda b, r, n: (b, n, r))
    prev = pl.BlockSpec((None, BLOCK, width),
                        lambda b, r, n: (b, jnp.maximum(n * blocks - 1, 0), r))
    return pl.pallas_call(
        functools.partial(_dilated_body, blocks=blocks, residues=residues),
        grid=(batch, dil // residues, sub_seq // tile),
        in_specs=[cur, cur, cur, prev, prev, _resident(bias_band.shape)],
        out_specs=[cur, cur],
        out_shape=[jax.ShapeDtypeStruct(q.shape, BF16), jax.ShapeDtypeStruct(q.shape, F32)],
        compiler_params=_compiler_params(("parallel", "parallel", "parallel")),
        name=f"dilated_d{dil}",
    )(q, k, v, k, v, bias_band)


def _bias_band(rel_bias, group, dil, n_back):
    heads = slice(group * HEADS_PER_GROUP, (group + 1) * HEADS_PER_GROUP)
    buckets = _t5_bucket(np.arange(n_back + 1) * dil)
    bias = rel_bias[buckets][:, heads].T.astype(F32) * LOG2E
    period = 2 * BLOCK + 1
    j = BLOCK - np.arange(period)
    local = (j >= 0) & (j <= n_back)
    diag = jnp.where(local[None], bias[:, np.clip(j, 0, n_back)], MASKED)
    band = jnp.tile(diag, (1, BLOCK))[:, :BLOCK * 2 * BLOCK]
    return band.reshape(HEADS_PER_GROUP * BLOCK, 2 * BLOCK)


class _StickStep(NamedTuple):
    q: jax.Array
    k: jax.Array
    v: jax.Array
    mask: Optional[jax.Array]
    carry: Any
    after: Optional[int]


def _neg_abs(x):
    bits = lax.bitcast_convert_type(x, jnp.uint32) | jnp.uint32(0x80000000)
    return lax.bitcast_convert_type(bits, F32)


def _stick_steps(steps, upper):
    z = [lax.dot_general(st.q, st.k, (((1,), (1,)), ((), ())), preferred_element_type=F32)
         for st in steps]
    z = [s if st.mask is None else jnp.where(st.mask, s, MASKED) for s, st in zip(z, steps)]
    soft = [jnp.maximum(s, 0.0) + jnp.log2(1.0 + jnp.exp2(_neg_abs(s))) for s in z]
    log_beta = [s - p for s, p in zip(z, soft)]
    tail = [jnp.dot(p.astype(BF16), upper, preferred_element_type=F32) for p in soft]
    sums = [jnp.sum(p, axis=-1, keepdims=True) for p in soft]
    outs = []
    for i, st in enumerate(steps):
        carry = st.carry if st.after is None else st.carry + sums[st.after]
        w = jnp.exp2(log_beta[i] - (tail[i] + carry))
        outs.append(w.astype(BF16))
    outs = [jnp.dot(w, st.v, preferred_element_type=F32) for w, st in zip(outs, steps)]
    return outs, sums


def _stick_body(q_ref, k_ref, v_ref, upper_ref, o_ref, acc_ref, carry_ref):
    tile = upper_ref.shape[0]
    q_tiles = q_ref.shape[0] // tile
    heads = LANES // HEAD_DIM
    first_tile = pl.program_id(2) * q_tiles
    lane_head = lax.broadcasted_iota(jnp.int32, (tile, LANES), 1) // HEAD_DIM
    row = lax.broadcasted_iota(jnp.int32, (tile, tile), 0)
    col = lax.broadcasted_iota(jnp.int32, (tile, tile), 1)
    strictly_before = col < row
    upper = upper_ref[...]
    chains = []
    for t in range(q_tiles):
        q = q_ref[t * tile:(t + 1) * tile, :]
        for hd in range(heads):
            chains.append((t, hd, jnp.where(lane_head == hd, q, jnp.zeros_like(q))))

    def key_value(kt):
        start = pl.multiple_of(jnp.maximum(kt, 0) * tile, tile)
        v = v_ref[pl.ds(start, tile), :]
        return k_ref[pl.ds(start, tile), :], jnp.where(kt >= 0, v, jnp.zeros_like(v))

    diag = [key_value(first_tile + t) for t in range(q_tiles)]
    prev = [key_value(first_tile + t - 1) for t in range(q_tiles)]
    steps = []
    for t, hd, qh in chains:
        steps.append(_StickStep(qh, *diag[t], strictly_before, 0.0, None))
        steps.append(_StickStep(qh, *prev[t], None, 0.0, len(steps) - 1))
    outs, sums = _stick_steps(steps, upper)
    for c in range(len(chains)):
        acc_ref[c] = outs[2 * c] + outs[2 * c + 1]
        carry_ref[c] = sums[2 * c] + sums[2 * c + 1]

    def any_weight_left():
        return (jnp.min(carry_ref[...]) < F32_EXP2_UNDERFLOW).astype(jnp.int32)

    def more_tiles(state):
        j, live = state
        return jnp.logical_and(j < first_tile + q_tiles - 1, live > 0)

    def earlier(state):
        j, _ = state
        tiles = [key_value(first_tile + t - 1 - j) for t in range(q_tiles)]
        outs, sums = _stick_steps(
            [_StickStep(qh, *tiles[t], None, carry_ref[c], None)
             for c, (t, _, qh) in enumerate(chains)], upper)
        for c in range(len(chains)):
            acc_ref[c] += outs[c]
            carry_ref[c] += sums[c]
        return j + 1, any_weight_left()

    lax.while_loop(more_tiles, earlier, (jnp.int32(1), any_weight_left()))
    for t in range(q_tiles):
        out = acc_ref[t * heads]
        for hd in range(1, heads):
            out = jnp.where(lane_head == hd, acc_ref[t * heads + hd], out)
        o_ref[t * tile:(t + 1) * tile, :] = out.astype(o_ref.dtype)


def _stick_breaking(q, k, v):
    batch, seq, width = q.shape
    tile = STICK_TILE
    rows = STICK_Q_TILES * tile
    chains = STICK_Q_TILES * (LANES // HEAD_DIM)
    j = np.arange(tile)
    upper = jnp.asarray((j[:, None] > j[None, :]), BF16)
    return pl.pallas_call(
        _stick_body,
        grid=(batch, width // LANES, seq // rows),
        in_specs=[pl.BlockSpec((None, rows, LANES), lambda b, p, i: (b, i, p)),
                  pl.BlockSpec((None, seq, LANES), lambda b, p, i: (b, 0, p)),
                  pl.BlockSpec((None, seq, LANES), lambda b, p, i: (b, 0, p)),
                  _resident((tile, tile))],
        out_specs=pl.BlockSpec((None, rows, LANES), lambda b, p, i: (b, i, p)),
        out_shape=jax.ShapeDtypeStruct(q.shape, BF16),
        scratch_shapes=[pltpu.VMEM((chains, tile, LANES), F32),
                        pltpu.VMEM((chains, tile, 1), F32)],
        compiler_params=_compiler_params(("parallel", "parallel", "parallel")),
        name="stick_breaking",
    )(q, k, v, upper)


def _mix_tile(x_ref, gates_ref, yb_ref, group_refs, wa_ref, wb_ref, wo_ref, g_ref, slab_ref,
              mid_ref):
    tile, d_model = x_ref.shape

    def natural_order(ref, dil):
        if dil == 1:
            return ref[...].astype(F32)
        rows = tile // dil
        slabs = GROUP_W // LANES
        if dil <= SUBLANES:
            for r in range(dil):
                for s in range(slabs):
                    lane0 = r * GROUP_W + s * LANES
                    slab_ref[s, pl.ds(r, rows, stride=dil), :] = (
                        ref[:, lane0:lane0 + LANES].astype(F32))
        else:
            outer = dil // STRIDE_STEP
            for hi in range(outer):
                for lo in range(STRIDE_STEP):
                    for s in range(slabs):
                        lane0 = (hi * STRIDE_STEP + lo) * GROUP_W + s * LANES
                        mid_ref[lo * slabs + s, pl.ds(hi, rows, stride=outer), :] = (
                            ref[:, lane0:lane0 + LANES].astype(F32))
            for lo in range(STRIDE_STEP):
                for s in range(slabs):
                    slab_ref[s, pl.ds(lo, rows * outer, stride=STRIDE_STEP), :] = (
                        mid_ref[lo * slabs + s])
        return jnp.concatenate([slab_ref[s] for s in range(slabs)], axis=-1)

    def sigmoid(v):
        return 0.5 * jnp.tanh(0.5 * v) + 0.5

    lses = [natural_order(group_refs[2 * g + 1], dil) for g, (_, dil) in enumerate(DILATED_GROUPS)]
    top = functools.reduce(jnp.maximum, lses)
    num, den = 0.0, 0.0
    for g, (_, dil) in enumerate(DILATED_GROUPS):
        w = jnp.exp2(lses[g] - top)
        num = num + w * natural_order(group_refs[2 * g], dil)
        den = den + w
    ya = num / den

    branch_a = jnp.dot(ya.astype(BF16), wa_ref[...], preferred_element_type=F32)
    branch_b = jnp.dot(yb_ref[...], wb_ref[...], preferred_element_type=F32)
    gate_a = gates_ref[:, :d_model].astype(F32)
    gate_b = gates_ref[:, d_model:].astype(F32)
    merged = sigmoid(gate_a) * branch_a + sigmoid(gate_b) * branch_b
    mixed = jnp.dot(merged.astype(BF16), wo_ref[...], preferred_element_type=F32)
    return x_ref[...] + _rms_norm(mixed, g_ref[...])


def _ffn_tile(x_mid, g_pre_ref, w_up_ref, cw_ref, cb_ref, w_down_ref, g_post_ref, out_ref,
              halo_ref, u_ref, act_ref, slab_ref):
    tile, d_model = x_mid.shape
    d_ff = w_down_ref.shape[0]
    seg = tile // SUBLANES
    n_slabs = d_model // LANES
    halo_rows = (CONV_WIDTH - 1) * SUBLANES

    @pl.when(pl.program_id(1) == 0)
    def _():
        halo_ref[...] = jnp.zeros_like(halo_ref)

    for c in range(n_slabs):
        for s in range(SUBLANES):
            slab_ref[c, pl.ds(s, seg, stride=SUBLANES), :] = (
                x_mid[s * seg:(s + 1) * seg, c * LANES:(c + 1) * LANES])
    x = jnp.concatenate([slab_ref[c] for c in range(n_slabs)], axis=-1)
    h = _rms_norm(x, g_pre_ref[...]).astype(BF16)
    sublane = lax.broadcasted_iota(jnp.int32, (SUBLANES, FF_CHUNK), 0)

    def up(c):
        return [jnp.dot(h, w_up_ref[:, col:col + FF_CHUNK], preferred_element_type=F32)
                for col in (c * FF_CHUNK, d_ff + c * FF_CHUNK)]

    def conv(u, col, slot):
        cols = slice(col, col + FF_CHUNK)
        last = u[tile - halo_rows:, :]
        for k in range(CONV_WIDTH - 1):
            group = slice(k * SUBLANES, (k + 1) * SUBLANES)
            mixed = jnp.where(sublane == SUBLANES - 1, halo_ref[group, cols], last[group, :])
            u_ref[slot, group, :] = pltpu.roll(mixed, 1, axis=0)
        u_ref[slot, halo_rows:, :] = u
        halo_ref[:, cols] = last
        y = cb_ref[:, cols]
        for t in range(CONV_WIDTH):
            start = t * SUBLANES
            y = y + cw_ref[t:t + 1, cols] * u_ref[slot, start:start + tile, :]
        return y

    n_chunks = d_ff // FF_CHUNK
    u_next = up(0)
    for c in range(n_chunks):
        u_gate, u_val = u_next
        if c + 1 < n_chunks:
            u_next = up(c + 1)
        gate = conv(u_gate, c * FF_CHUNK, 0)
        val = conv(u_val, d_ff + c * FF_CHUNK, 1)
        act_ref[:, c * FF_CHUNK:(c + 1) * FF_CHUNK] = (
            jax.nn.gelu(gate, approximate=True) * val).astype(BF16)
    part_rows = tile // DOWN_PARTS
    groups = part_rows // SUBLANES
    down = [jnp.dot(act_ref[p * part_rows:(p + 1) * part_rows, :], w_down_ref[...],
                    preferred_element_type=F32) for p in range(DOWN_PARTS)]
    for p, acc in enumerate(down):
        rows = slice(p * part_rows, (p + 1) * part_rows)
        y = x[rows, :] + _rms_norm(acc, g_post_ref[...])
        for c in range(n_slabs):
            slab_ref[c, rows, :] = y[:, c * LANES:(c + 1) * LANES]
        for c in range(n_slabs):
            for s in range(SUBLANES):
                out_ref[s * seg + p * groups:s * seg + (p + 1) * groups,
                        c * LANES:(c + 1) * LANES] = (
                    slab_ref[c, pl.ds(p * part_rows + s, groups, stride=SUBLANES), :])


def _post_attention_body(x_ref, gates_ref, yb_ref, *refs):
    group_refs = refs[:2 * N_GROUPS]
    (wa_ref, wb_ref, wo_ref, g_post_mix_ref, g_pre_ref, w_up_ref, cw_ref, cb_ref, w_down_ref,
     g_post_ref, out_ref, group_slab_ref, group_mid_ref, halo_ref, u_ref, act_ref,
     slab_ref) = refs[2 * N_GROUPS:]
    x_mid = _mix_tile(x_ref, gates_ref, yb_ref, group_refs, wa_ref, wb_ref, wo_ref,
                      g_post_mix_ref, group_slab_ref, group_mid_ref)
    _ffn_tile(x_mid, g_pre_ref, w_up_ref, cw_ref, cb_ref, w_down_ref, g_post_ref, out_ref,
              halo_ref, u_ref, act_ref, slab_ref)


def _post_attention(x, gates, yb, groups, mix_consts, ffn_consts):
    batch, seq, d_model = x.shape
    tile = TOKEN_TILE
    d_ff = ffn_consts[4].shape[0]
    row_spec = lambda width: pl.BlockSpec((None, tile, width), lambda b, i: (b, i, 0))
    in_specs = [row_spec(d_model), row_spec(gates.shape[-1]), row_spec(yb.shape[-1])]
    operands = [x, gates, yb]
    for (o, lse), (_, dil) in zip(groups, DILATED_GROUPS):
        for arr in (o, lse):
            in_specs.append(pl.BlockSpec((None, tile // dil, dil * GROUP_W), lambda b, i: (b, i, 0)))
            operands.append(arr)
    for arr in (*mix_consts, *ffn_consts):
        in_specs.append(_resident(arr.shape))
        operands.append(arr)
    halo_rows = (CONV_WIDTH - 1) * SUBLANES
    return pl.pallas_call(
        _post_attention_body,
        grid=(batch, seq // tile),
        in_specs=in_specs,
        out_specs=row_spec(d_model),
        out_shape=jax.ShapeDtypeStruct(x.shape, F32),
        scratch_shapes=[pltpu.VMEM((GROUP_W // LANES, tile, LANES), F32),
                        pltpu.VMEM((STRIDE_STEP * GROUP_W // LANES, tile // STRIDE_STEP, LANES), F32),
                        pltpu.VMEM((halo_rows, 2 * d_ff), F32),
                        pltpu.VMEM((2, halo_rows + tile, FF_CHUNK), F32),
                        pltpu.VMEM((tile, d_ff), BF16),
                        pltpu.VMEM((d_model // LANES, tile, LANES), F32)],
        compiler_params=_compiler_params(("arbitrary", "arbitrary")),
        name="post_attention",
    )(*operands)


def kernel(x, g_pre_mix, w_in, rel_bias, w_branch_a, w_branch_b, w_out, g_post_mix, g_pre_ffn,
           w_up, conv_w, conv_b, w_down, g_post_ffn):
    depth = w_in.shape[0]
    for l in range(depth):
        proj = _in_proj(x, g_pre_mix[l][None], w_in[l].astype(BF16))
        qb, kb, vb, gates = proj[3 * N_GROUPS:]
        groups = []
        for g, (window, dil) in enumerate(DILATED_GROUPS):
            q, k, v = proj[3 * g:3 * g + 3]
            band = _bias_band(rel_bias, g, dil, window // dil)
            groups.append(_dilated_group(q, k, v, band, dil))
        yb = _stick_breaking(qb, kb, vb)
        mix_consts = (w_branch_a[l].astype(BF16), w_branch_b[l].astype(BF16),
                      w_out[l].astype(BF16), g_post_mix[l][None])
        ffn_consts = (g_pre_ffn[l][None], w_up[l].astype(BF16), conv_w[l], conv_b[l][None],
                      w_down[l].astype(BF16), g_post_ffn[l][None])
        x = _post_attention(x, gates, yb, groups, mix_consts, ffn_consts)
    return x
```

```python
import functools
import math
from typing import Any, NamedTuple, Optional

import numpy as np
import jax
import jax.numpy as jnp
from jax import lax
from jax.experimental import pallas as pl
from jax.experimental.pallas import tpu as pltpu

F32 = jnp.float32
BF16 = jnp.bfloat16

HEAD_DIM = 64
DILATED_GROUPS = ((128, 1), (512, 4), (2048, 16))
HEADS_PER_GROUP = 4
N_GROUPS = len(DILATED_GROUPS)
N_HEADS_A = HEADS_PER_GROUP * N_GROUPS
N_HEADS_B = 8
GROUP_W = HEADS_PER_GROUP * HEAD_DIM
A_QKV = N_HEADS_A * HEAD_DIM
B_QKV = N_HEADS_B * HEAD_DIM
BLOCK = 128
NUM_BUCKETS = 32
MAX_DISTANCE = 2048
CONV_WIDTH = 3
EPS = 1e-6
QK_SCALE = 1.0 / math.sqrt(HEAD_DIM)
LOG2E = math.log2(math.e)
MASKED = -1e30
F32_EXP2_UNDERFLOW = 150.1

LANES = 128
SUBLANES = 8
STRIDE_STEP = 4
VMEM_LIMIT_BYTES = 56 * 1024 * 1024

TOKEN_TILE = 512
IN_PROJ_TILE = 1024
GATE_CHUNK = 512
DILATED_BLOCKS_PER_STEP = 16
STICK_Q_TILES = 8
STICK_TILE = 256
FF_CHUNK = 256
DOWN_PARTS = 2


def _t5_bucket(dist):
    max_exact = NUM_BUCKETS // 2
    n = np.asarray(dist, dtype=np.float32)
    large = max_exact + (np.log(np.maximum(n, 1.0) / max_exact)
                         / np.log(MAX_DISTANCE / max_exact)
                         * (NUM_BUCKETS - max_exact)).astype(np.int32)
    large = np.minimum(large, NUM_BUCKETS - 1)
    return np.where(n < max_exact, n.astype(np.int32), large).astype(np.int32)


def _rms_norm(x, g):
    return x * lax.rsqrt(jnp.mean(x * x, axis=-1, keepdims=True) + EPS) * g


def _compiler_params(semantics):
    return pltpu.CompilerParams(dimension_semantics=semantics, vmem_limit_bytes=VMEM_LIMIT_BYTES)


def _resident(shape):
    return pl.BlockSpec(shape, lambda *_: (0,) * len(shape), pipeline_mode=pl.Buffered(1))


def _in_proj_body(x_ref, g_ref, w_ref, *refs):
    a_refs = refs[:3 * N_GROUPS]
    qb_ref, kb_ref, vb_ref, gates_ref, slab_ref = refs[3 * N_GROUPS:]
    tile = x_ref.shape[0]
    h = _rms_norm(x_ref[...], g_ref[...]).astype(BF16)

    def proj(col, width):
        return jnp.dot(h, w_ref[:, col:col + width], preferred_element_type=F32)

    for t in range(3):
        scale = QK_SCALE * LOG2E if t == 0 else 1.0
        for g, (_, dil) in enumerate(DILATED_GROUPS):
            acc = proj(t * A_QKV + g * GROUP_W, GROUP_W) * scale
            out_ref = a_refs[3 * g + t]
            if dil == 1:
                out_ref[...] = acc.astype(BF16)
                continue
            rows = tile // dil
            for s in range(GROUP_W // LANES):
                slab_ref[s] = acc[:, s * LANES:(s + 1) * LANES]
            for r in range(dil):
                for s in range(GROUP_W // LANES):
                    lane0 = r * GROUP_W + s * LANES
                    out_ref[:, lane0:lane0 + LANES] = (
                        slab_ref[s, pl.ds(r, rows, stride=dil), :].astype(BF16))

    base = 3 * A_QKV
    qb_ref[...] = (proj(base, B_QKV) * (QK_SCALE * LOG2E)).astype(BF16)
    kb_ref[...] = proj(base + B_QKV, B_QKV).astype(BF16)
    vb_ref[...] = proj(base + 2 * B_QKV, B_QKV).astype(BF16)
    base += 3 * B_QKV
    gate_w = gates_ref.shape[1]
    for c in range(0, gate_w, GATE_CHUNK):
        gates_ref[:, c:c + GATE_CHUNK] = proj(base + c, GATE_CHUNK).astype(BF16)


def _in_proj(x, g, w):
    batch, seq, d_model = x.shape
    tile = IN_PROJ_TILE
    n_in = w.shape[1]
    gate_w = n_in - 3 * A_QKV - 3 * B_QKV
    out_shape, out_specs = [], []
    for _, dil in DILATED_GROUPS:
        for _ in range(3):
            out_shape.append(jax.ShapeDtypeStruct((batch, seq // dil, dil * GROUP_W), BF16))
            out_specs.append(pl.BlockSpec((None, tile // dil, dil * GROUP_W), lambda b, i: (b, i, 0)))
    for _ in range(3):
        out_shape.append(jax.ShapeDtypeStruct((batch, seq, B_QKV), BF16))
        out_specs.append(pl.BlockSpec((None, tile, B_QKV), lambda b, i: (b, i, 0)))
    out_shape.append(jax.ShapeDtypeStruct((batch, seq, gate_w), BF16))
    out_specs.append(pl.BlockSpec((None, tile, gate_w), lambda b, i: (b, i, 0)))
    return pl.pallas_call(
        _in_proj_body,
        grid=(batch, seq // tile),
        in_specs=[pl.BlockSpec((None, tile, d_model), lambda b, i: (b, i, 0)),
                  _resident((1, d_model)),
                  _resident((d_model, n_in))],
        out_specs=out_specs,
        out_shape=out_shape,
        scratch_shapes=[pltpu.VMEM((GROUP_W // LANES, tile, LANES), F32)],
        compiler_params=_compiler_params(("parallel", "parallel")),
        name="in_proj",
    )(x, g, w)


def _dilated_body(q_ref, k_ref, v_ref, kp_ref, vp_ref, bias_ref, o_ref, lse_ref, *,
                  blocks, residues, first_tile):
    heads = HEADS_PER_GROUP
    missing_cols = jnp.where(first_tile, BLOCK, 0)
    lane_head = lax.broadcasted_iota(jnp.int32, (BLOCK, GROUP_W), 1) // HEAD_DIM
    key_col = lax.broadcasted_iota(jnp.int32, (heads * BLOCK, 2 * BLOCK), 1)
    bias = bias_ref[...]

    def keys_values(res):
        lanes = slice(res * GROUP_W, (res + 1) * GROUP_W)
        k_all = jnp.concatenate([kp_ref[:, lanes], k_ref[:, lanes]], axis=0)
        v_all = jnp.concatenate([vp_ref[:, lanes], v_ref[:, lanes]], axis=0)
        v_lane_head = lax.broadcasted_iota(jnp.int32, v_all.shape, 1) // HEAD_DIM
        v_heads = [jnp.where(v_lane_head == hd, v_all, jnp.zeros_like(v_all))
                   for hd in range(heads)]
        return k_all, v_heads

    def scores(res, nb, k_all):
        q = q_ref[nb * BLOCK:(nb + 1) * BLOCK, res * GROUP_W:(res + 1) * GROUP_W]
        q_stack = jnp.concatenate(
            [jnp.where(lane_head == hd, q, jnp.zeros_like(q)) for hd in range(heads)], axis=0)
        s = lax.dot_general(q_stack, k_all[nb * BLOCK:(nb + 2) * BLOCK], (((1,), (1,)), ((), ())),
                            preferred_element_type=F32) + bias
        if nb == 0:
            s = jnp.where(key_col < missing_cols, MASKED, s)
        return s

    def finish(res, nb, s, v_heads):
        mx = jnp.max(s, axis=-1, keepdims=True)
        e = jnp.exp2(s - mx)
        den = jnp.sum(e, axis=-1, keepdims=True)
        p = (e * (1.0 / den)).astype(BF16)
        p_cat = jnp.concatenate([p[hd * BLOCK:(hd + 1) * BLOCK] for hd in range(heads)], axis=1)
        v_stack = jnp.concatenate([vh[nb * BLOCK:(nb + 2) * BLOCK] for vh in v_heads], axis=0)
        o = jnp.dot(p_cat, v_stack, preferred_element_type=F32)
        lse = mx + jnp.log2(den)
        lse_full = jnp.broadcast_to(lse[:BLOCK], (BLOCK, GROUP_W))
        for hd in range(1, heads):
            lse_full = jnp.where(lane_head == hd, lse[hd * BLOCK:(hd + 1) * BLOCK], lse_full)
        rows = slice(nb * BLOCK, (nb + 1) * BLOCK)
        lanes = slice(res * GROUP_W, (res + 1) * GROUP_W)
        o_ref[rows, lanes] = o.astype(o_ref.dtype)
        lse_ref[rows, lanes] = lse_full

    units = [(res, nb) for res in range(residues) for nb in range(blocks)]
    kv = {res: keys_values(res) for res in range(residues)}
    s_next = scores(*units[0], kv[units[0][0]][0])
    for i, (res, nb) in enumerate(units):
        s = s_next
        if i + 1 < len(units):
            s_next = scores(*units[i + 1], kv[units[i + 1][0]][0])
        finish(res, nb, s, kv[res][1])


def _dilated_groups_body(*refs, plans):
    n_groups = len(plans)
    ins, outs = refs[:6 * n_groups], refs[6 * n_groups:]
    group, step = pl.program_id(0), pl.program_id(1)
    for gi, (blocks, residues, _, tiles) in enumerate(plans):
        @pl.when(group == gi)
        def _(gi=gi, blocks=blocks, residues=residues, tiles=tiles):
            _dilated_body(*ins[6 * gi:6 * gi + 6], *outs[2 * gi:2 * gi + 2], blocks=blocks,
                          residues=residues, first_tile=lax.rem(step, tiles) == 0)


def _dilated_groups(qkv, bands):
    plans, steps = [], None
    for (q, _, _), (_, dil) in zip(qkv, DILATED_GROUPS):
        batch, sub_seq, _ = q.shape
        blocks = min(DILATED_BLOCKS_PER_STEP, sub_seq // BLOCK)
        residues = min(DILATED_BLOCKS_PER_STEP // blocks, dil)
        plan = (blocks, residues, dil // residues, sub_seq // (blocks * BLOCK))
        assert steps in (None, batch * plan[2] * plan[3]), "groups must take equally many steps"
        steps = batch * plan[2] * plan[3]
        plans.append(plan)

    def block_index(gi, plan, prev):
        blocks, _, classes, tiles = plan

        def index(group, step):
            s = jnp.where(group == gi, step, jnp.where(group < gi, 0, steps - 1))
            b, r, n = s // (classes * tiles), (s // tiles) % classes, s % tiles
            return (b, jnp.maximum(n * blocks - 1, 0), r) if prev else (b, n, r)
        return index

    in_specs, operands, out_specs, out_shape = [], [], [], []
    for gi, ((q, k, v), band, plan) in enumerate(zip(qkv, bands, plans)):
        blocks, residues, _, _ = plan
        cur = pl.BlockSpec((None, blocks * BLOCK, residues * GROUP_W), block_index(gi, plan, False))
        prev = pl.BlockSpec((None, BLOCK, residues * GROUP_W), block_index(gi, plan, True))
        in_specs += [cur, cur, cur, prev, prev, _resident(band.shape)]
        operands += [q, k, v, k, v, band]
        out_specs += [cur, cur]
        out_shape += [jax.ShapeDtypeStruct(q.shape, BF16), jax.ShapeDtypeStruct(q.shape, F32)]
    outs = pl.pallas_call(
        functools.partial(_dilated_groups_body, plans=tuple(plans)),
        grid=(len(plans), steps),
        in_specs=in_specs,
        out_specs=out_specs,
        out_shape=out_shape,
        compiler_params=_compiler_params(("arbitrary", "arbitrary")),
        name="dilated",
    )(*operands)
    return [(outs[2 * gi], outs[2 * gi + 1]) for gi in range(len(plans))]


def _bias_band(rel_bias, group, dil, n_back):
    heads = slice(group * HEADS_PER_GROUP, (group + 1) * HEADS_PER_GROUP)
    buckets = _t5_bucket(np.arange(n_back + 1) * dil)
    bias = rel_bias[buckets][:, heads].T.astype(F32) * LOG2E
    period = 2 * BLOCK + 1
    j = BLOCK - np.arange(period)
    local = (j >= 0) & (j <= n_back)
    diag = jnp.where(local[None], bias[:, np.clip(j, 0, n_back)], MASKED)
    band = jnp.tile(diag, (1, BLOCK))[:, :BLOCK * 2 * BLOCK]
    return band.reshape(HEADS_PER_GROUP * BLOCK, 2 * BLOCK)


class _StickStep(NamedTuple):
    q: jax.Array
    k: jax.Array
    v: jax.Array
    mask: Optional[jax.Array]
    carry: Any
    after: Optional[int]


def _neg_abs(x):
    bits = lax.bitcast_convert_type(x, jnp.uint32) | jnp.uint32(0x80000000)
    return lax.bitcast_convert_type(bits, F32)


def _stick_steps(steps, upper):
    z = [lax.dot_general(st.q, st.k, (((1,), (1,)), ((), ())), preferred_element_type=F32)
         for st in steps]
    z = [s if st.mask is None else jnp.where(st.mask, s, MASKED) for s, st in zip(z, steps)]
    soft = [jnp.maximum(s, 0.0) + jnp.log2(1.0 + jnp.exp2(_neg_abs(s))) for s in z]
    log_beta = [s - p for s, p in zip(z, soft)]
    tail = [jnp.dot(p.astype(BF16), upper, preferred_element_type=F32) for p in soft]
    sums = [jnp.sum(p, axis=-1, keepdims=True) for p in soft]
    outs = []
    for i, st in enumerate(steps):
        carry = st.carry if st.after is None else st.carry + sums[st.after]
        w = jnp.exp2(log_beta[i] - (tail[i] + carry))
        outs.append(w.astype(BF16))
    outs = [jnp.dot(w, st.v, preferred_element_type=F32) for w, st in zip(outs, steps)]
    return outs, sums


def _stick_body(q_ref, k_ref, v_ref, upper_ref, o_ref, acc_ref, carry_ref):
    tile = upper_ref.shape[0]
    q_tiles = q_ref.shape[0] // tile
    heads = LANES // HEAD_DIM
    first_tile = pl.program_id(2) * q_tiles
    lane_head = lax.broadcasted_iota(jnp.int32, (tile, LANES), 1) // HEAD_DIM
    row = lax.broadcasted_iota(jnp.int32, (tile, tile), 0)
    col = lax.broadcasted_iota(jnp.int32, (tile, tile), 1)
    strictly_before = col < row
    upper = upper_ref[...]
    chains = []
    for t in range(q_tiles):
        q = q_ref[t * tile:(t + 1) * tile, :]
        for hd in range(heads):
            chains.append((t, hd, jnp.where(lane_head == hd, q, jnp.zeros_like(q))))

    def key_value(kt):
        start = pl.multiple_of(jnp.maximum(kt, 0) * tile, tile)
        v = v_ref[pl.ds(start, tile), :]
        return k_ref[pl.ds(start, tile), :], jnp.where(kt >= 0, v, jnp.zeros_like(v))

    diag = [key_value(first_tile + t) for t in range(q_tiles)]
    prev = [key_value(first_tile + t - 1) for t in range(q_tiles)]
    steps = []
    for t, hd, qh in chains:
        steps.append(_StickStep(qh, *diag[t], strictly_before, 0.0, None))
        steps.append(_StickStep(qh, *prev[t], None, 0.0, len(steps) - 1))
    outs, sums = _stick_steps(steps, upper)
    for c in range(len(chains)):
        acc_ref[c] = outs[2 * c] + outs[2 * c + 1]
        carry_ref[c] = sums[2 * c] + sums[2 * c + 1]

    def any_weight_left():
        return (jnp.min(carry_ref[...]) < F32_EXP2_UNDERFLOW).astype(jnp.int32)

    def more_tiles(state):
        j, live = state
        return jnp.logical_and(j < first_tile + q_tiles - 1, live > 0)

    def earlier(state):
        j, _ = state
        tiles = [key_value(first_tile + t - 1 - j) for t in range(q_tiles)]
        outs, sums = _stick_steps(
            [_StickStep(qh, *tiles[t], None, carry_ref[c], None)
             for c, (t, _, qh) in enumerate(chains)], upper)
        for c in range(len(chains)):
            acc_ref[c] += outs[c]
            carry_ref[c] += sums[c]
        return j + 1, any_weight_left()

    lax.while_loop(more_tiles, earlier, (jnp.int32(1), any_weight_left()))
    for t in range(q_tiles):
        out = acc_ref[t * heads]
        for hd in range(1, heads):
            out = jnp.where(lane_head == hd, acc_ref[t * heads + hd], out)
        o_ref[t * tile:(t + 1) * tile, :] = out.astype(o_ref.dtype)


def _stick_breaking(q, k, v):
    batch, seq, width = q.shape
    tile = STICK_TILE
    rows = STICK_Q_TILES * tile
    chains = STICK_Q_TILES * (LANES // HEAD_DIM)
    j = np.arange(tile)
    upper = jnp.asarray((j[:, None] > j[None, :]), BF16)
    return pl.pallas_call(
        _stick_body,
        grid=(batch, width // LANES, seq // rows),
        in_specs=[pl.BlockSpec((None, rows, LANES), lambda b, p, i: (b, i, p)),
                  pl.BlockSpec((None, seq, LANES), lambda b, p, i: (b, 0, p)),
                  pl.BlockSpec((None, seq, LANES), lambda b, p, i: (b, 0, p)),
                  _resident((tile, tile))],
        out_specs=pl.BlockSpec((None, rows, LANES), lambda b, p, i: (b, i, p)),
        out_shape=jax.ShapeDtypeStruct(q.shape, BF16),
        scratch_shapes=[pltpu.VMEM((chains, tile, LANES), F32),
                        pltpu.VMEM((chains, tile, 1), F32)],
        compiler_params=_compiler_params(("parallel", "parallel", "parallel")),
        name="stick_breaking",
    )(q, k, v, upper)


def _mix_tile(x_ref, gates_ref, yb_ref, group_refs, wa_ref, wb_ref, wo_ref, g_ref, slab_ref,
              mid_ref):
    tile, d_model = x_ref.shape

    def natural_order(ref, dil):
        if dil == 1:
            return ref[...].astype(F32)
        rows = tile // dil
        slabs = GROUP_W // LANES
        if dil <= SUBLANES:
            for r in range(dil):
                for s in range(slabs):
                    lane0 = r * GROUP_W + s * LANES
                    slab_ref[s, pl.ds(r, rows, stride=dil), :] = (
                        ref[:, lane0:lane0 + LANES].astype(F32))
        else:
            outer = dil // STRIDE_STEP
            for hi in range(outer):
                for lo in range(STRIDE_STEP):
                    for s in range(slabs):
                        lane0 = (hi * STRIDE_STEP + lo) * GROUP_W + s * LANES
                        mid_ref[lo * slabs + s, pl.ds(hi, rows, stride=outer), :] = (
                            ref[:, lane0:lane0 + LANES].astype(F32))
            for lo in range(STRIDE_STEP):
                for s in range(slabs):
                    slab_ref[s, pl.ds(lo, rows * outer, stride=STRIDE_STEP), :] = (
                        mid_ref[lo * slabs + s])
        return jnp.concatenate([slab_ref[s] for s in range(slabs)], axis=-1)

    def sigmoid(v):
        return 0.5 * jnp.tanh(0.5 * v) + 0.5

    lses = [natural_order(group_refs[2 * g + 1], dil) for g, (_, dil) in enumerate(DILATED_GROUPS)]
    top = functools.reduce(jnp.maximum, lses)
    num, den = 0.0, 0.0
    for g, (_, dil) in enumerate(DILATED_GROUPS):
        w = jnp.exp2(lses[g] - top)
        num = num + w * natural_order(group_refs[2 * g], dil)
        den = den + w
    ya = num / den

    branch_a = jnp.dot(ya.astype(BF16), wa_ref[...], preferred_element_type=F32)
    branch_b = jnp.dot(yb_ref[...], wb_ref[...], preferred_element_type=F32)
    gate_a = gates_ref[:, :d_model].astype(F32)
    gate_b = gates_ref[:, d_model:].astype(F32)
    merged = sigmoid(gate_a) * branch_a + sigmoid(gate_b) * branch_b
    mixed = jnp.dot(merged.astype(BF16), wo_ref[...], preferred_element_type=F32)
    return x_ref[...] + _rms_norm(mixed, g_ref[...])


def _ffn_tile(x_mid, g_pre_ref, w_up_ref, cw_ref, cb_ref, w_down_ref, g_post_ref, out_ref,
              halo_ref, u_ref, act_ref, slab_ref):
    tile, d_model = x_mid.shape
    d_ff = w_down_ref.shape[0]
    seg = tile // SUBLANES
    n_slabs = d_model // LANES
    halo_rows = (CONV_WIDTH - 1) * SUBLANES

    @pl.when(pl.program_id(1) == 0)
    def _():
        halo_ref[...] = jnp.zeros_like(halo_ref)

    for c in range(n_slabs):
        for s in range(SUBLANES):
            slab_ref[c, pl.ds(s, seg, stride=SUBLANES), :] = (
                x_mid[s * seg:(s + 1) * seg, c * LANES:(c + 1) * LANES])
    x = jnp.concatenate([slab_ref[c] for c in range(n_slabs)], axis=-1)
    h = _rms_norm(x, g_pre_ref[...]).astype(BF16)
    sublane = lax.broadcasted_iota(jnp.int32, (SUBLANES, FF_CHUNK), 0)

    def up(c):
        return [jnp.dot(h, w_up_ref[:, col:col + FF_CHUNK], preferred_element_type=F32)
                for col in (c * FF_CHUNK, d_ff + c * FF_CHUNK)]

    def conv(u, col, slot):
        cols = slice(col, col + FF_CHUNK)
        last = u[tile - halo_rows:, :]
        for k in range(CONV_WIDTH - 1):
            group = slice(k * SUBLANES, (k + 1) * SUBLANES)
            mixed = jnp.where(sublane == SUBLANES - 1, halo_ref[group, cols], last[group, :])
            u_ref[slot, group, :] = pltpu.roll(mixed, 1, axis=0)
        u_ref[slot, halo_rows:, :] = u
        halo_ref[:, cols] = last
        y = cb_ref[:, cols]
        for t in range(CONV_WIDTH):
            start = t * SUBLANES
            y = y + cw_ref[t:t + 1, cols] * u_ref[slot, start:start + tile, :]
        return y

    n_chunks = d_ff // FF_CHUNK
    u_next = up(0)
    for c in range(n_chunks):
        u_gate, u_val = u_next
        if c + 1 < n_chunks:
            u_next = up(c + 1)
        gate = conv(u_gate, c * FF_CHUNK, 0)
        val = conv(u_val, d_ff + c * FF_CHUNK, 1)
        act_ref[:, c * FF_CHUNK:(c + 1) * FF_CHUNK] = (
            jax.nn.gelu(gate, approximate=True) * val).astype(BF16)
    part_rows = tile // DOWN_PARTS
    groups = part_rows // SUBLANES
    down = [jnp.dot(act_ref[p * part_rows:(p + 1) * part_rows, :], w_down_ref[...],
                    preferred_element_type=F32) for p in range(DOWN_PARTS)]
    for p, acc in enumerate(down):
        rows = slice(p * part_rows, (p + 1) * part_rows)
        y = x[rows, :] + _rms_norm(acc, g_post_ref[...])
        for c in range(n_slabs):
            slab_ref[c, rows, :] = y[:, c * LANES:(c + 1) * LANES]
        for c in range(n_slabs):
            for s in range(SUBLANES):
                out_ref[s * seg + p * groups:s * seg + (p + 1) * groups,
                        c * LANES:(c + 1) * LANES] = (
                    slab_ref[c, pl.ds(p * part_rows + s, groups, stride=SUBLANES), :])


def _post_attention_body(x_ref, gates_ref, yb_ref, *refs):
    group_refs = refs[:2 * N_GROUPS]
    (wa_ref, wb_ref, wo_ref, g_post_mix_ref, g_pre_ref, w_up_ref, cw_ref, cb_ref, w_down_ref,
     g_post_ref, out_ref, group_slab_ref, group_mid_ref, halo_ref, u_ref, act_ref,
     slab_ref) = refs[2 * N_GROUPS:]
    x_mid = _mix_tile(x_ref, gates_ref, yb_ref, group_refs, wa_ref, wb_ref, wo_ref,
                      g_post_mix_ref, group_slab_ref, group_mid_ref)
    _ffn_tile(x_mid, g_pre_ref, w_up_ref, cw_ref, cb_ref, w_down_ref, g_post_ref, out_ref,
              halo_ref, u_ref, act_ref, slab_ref)


def _post_attention(x, gates, yb, groups, mix_consts, ffn_consts):
    batch, seq, d_model = x.shape
    tile = TOKEN_TILE
    d_ff = ffn_consts[4].shape[0]
    row_spec = lambda width: pl.BlockSpec((None, tile, width), lambda b, i: (b, i, 0))
    in_specs = [row_spec(d_model), row_spec(gates.shape[-1]), row_spec(yb.shape[-1])]
    operands = [x, gates, yb]
    for (o, lse), (_, dil) in zip(groups, DILATED_GROUPS):
        for arr in (o, lse):
            in_specs.append(pl.BlockSpec((None, tile // dil, dil * GROUP_W), lambda b, i: (b, i, 0)))
            operands.append(arr)
    for arr in (*mix_consts, *ffn_consts):
        in_specs.append(_resident(arr.shape))
        operands.append(arr)
    halo_rows = (CONV_WIDTH - 1) * SUBLANES
    return pl.pallas_call(
        _post_attention_body,
        grid=(batch, seq // tile),
        in_specs=in_specs,
        out_specs=row_spec(d_model),
        out_shape=jax.ShapeDtypeStruct(x.shape, F32),
        scratch_shapes=[pltpu.VMEM((GROUP_W // LANES, tile, LANES), F32),
                        pltpu.VMEM((STRIDE_STEP * GROUP_W // LANES, tile // STRIDE_STEP, LANES), F32),
                        pltpu.VMEM((halo_rows, 2 * d_ff), F32),
                        pltpu.VMEM((2, halo_rows + tile, FF_CHUNK), F32),
                        pltpu.VMEM((tile, d_ff), BF16),
                        pltpu.VMEM((d_model // LANES, tile, LANES), F32)],
        compiler_params=_compiler_params(("arbitrary", "arbitrary")),
        name="post_attention",
    )(*operands)


def kernel(x, g_pre_mix, w_in, rel_bias, w_branch_a, w_branch_b, w_out, g_post_mix, g_pre_ffn,
           w_up, conv_w, conv_b, w_down, g_post_ffn):
    depth = w_in.shape[0]
    for l in range(depth):
        proj = _in_proj(x, g_pre_mix[l][None], w_in[l].astype(BF16))
        qb, kb, vb, gates = proj[3 * N_GROUPS:]
        bands = [_bias_band(rel_bias, g, dil, window // dil)
                 for g, (window, dil) in enumerate(DILATED_GROUPS)]
        groups = _dilated_groups([proj[3 * g:3 * g + 3] for g in range(N_GROUPS)], bands)
        yb = _stick_breaking(qb, kb, vb)
        mix_consts = (w_branch_a[l].astype(BF16), w_branch_b[l].astype(BF16),
                      w_out[l].astype(BF16), g_post_mix[l][None])
        ffn_consts = (g_pre_ffn[l][None], w_up[l].astype(BF16), conv_w[l], conv_b[l][None],
                      w_down[l].astype(BF16), g_post_ffn[l][None])
        x = _post_attention(x, gates, yb, groups, mix_consts, ffn_consts)
    return x
```

```python
import functools
import math
from typing import Any, NamedTuple, Optional

import numpy as np
import jax
import jax.numpy as jnp
from jax import lax
from jax.experimental import pallas as pl
from jax.experimental.pallas import tpu as pltpu

F32 = jnp.float32
BF16 = jnp.bfloat16

HEAD_DIM = 64
DILATED_GROUPS = ((128, 1), (512, 4), (2048, 16))
HEADS_PER_GROUP = 4
N_GROUPS = len(DILATED_GROUPS)
N_HEADS_A = HEADS_PER_GROUP * N_GROUPS
N_HEADS_B = 8
GROUP_W = HEADS_PER_GROUP * HEAD_DIM
A_QKV = N_HEADS_A * HEAD_DIM
B_QKV = N_HEADS_B * HEAD_DIM
BLOCK = 128
NUM_BUCKETS = 32
MAX_DISTANCE = 2048
CONV_WIDTH = 3
EPS = 1e-6
QK_SCALE = 1.0 / math.sqrt(HEAD_DIM)
LOG2E = math.log2(math.e)
MASKED = -1e30
F32_EXP2_UNDERFLOW = 150.1

LANES = 128
SUBLANES = 8
STRIDE_STEP = 4
VMEM_LIMIT_BYTES = 56 * 1024 * 1024

TOKEN_TILE = 512
IN_PROJ_TILE = 1024
GATE_CHUNK = 512
DILATED_BLOCKS_PER_STEP = 16
STICK_Q_TILES = 8
STICK_TILE = 256
FF_CHUNK = 256
DOWN_PARTS = 2


def _t5_bucket(dist):
    max_exact = NUM_BUCKETS // 2
    n = np.asarray(dist, dtype=np.float32)
    large = max_exact + (np.log(np.maximum(n, 1.0) / max_exact)
                         / np.log(MAX_DISTANCE / max_exact)
                         * (NUM_BUCKETS - max_exact)).astype(np.int32)
    large = np.minimum(large, NUM_BUCKETS - 1)
    return np.where(n < max_exact, n.astype(np.int32), large).astype(np.int32)


def _rms_norm(x, g):
    return x * lax.rsqrt(jnp.mean(x * x, axis=-1, keepdims=True) + EPS) * g


def _compiler_params(semantics):
    return pltpu.CompilerParams(dimension_semantics=semantics, vmem_limit_bytes=VMEM_LIMIT_BYTES)


def _resident(shape):
    return pl.BlockSpec(shape, lambda *_: (0,) * len(shape), pipeline_mode=pl.Buffered(1))


def _in_proj_body(x_ref, g_ref, w_ref, *refs):
    a_refs = refs[:3 * N_GROUPS]
    qb_ref, kb_ref, vb_ref, gates_ref, slab_ref = refs[3 * N_GROUPS:]
    tile = x_ref.shape[0]
    h = _rms_norm(x_ref[...], g_ref[...]).astype(BF16)

    def proj(col, width):
        return jnp.dot(h, w_ref[:, col:col + width], preferred_element_type=F32)

    for t in range(3):
        scale = QK_SCALE * LOG2E if t == 0 else 1.0
        for g, (_, dil) in enumerate(DILATED_GROUPS):
            acc = proj(t * A_QKV + g * GROUP_W, GROUP_W) * scale
            out_ref = a_refs[3 * g + t]
            if dil == 1:
                out_ref[...] = acc.astype(BF16)
                continue
            rows = tile // dil
            for s in range(GROUP_W // LANES):
                slab_ref[s] = acc[:, s * LANES:(s + 1) * LANES]
            for r in range(dil):
                for s in range(GROUP_W // LANES):
                    lane0 = r * GROUP_W + s * LANES
                    out_ref[:, lane0:lane0 + LANES] = (
                        slab_ref[s, pl.ds(r, rows, stride=dil), :].astype(BF16))

    base = 3 * A_QKV
    qb_ref[...] = (proj(base, B_QKV) * (QK_SCALE * LOG2E)).astype(BF16)
    kb_ref[...] = proj(base + B_QKV, B_QKV).astype(BF16)
    vb_ref[...] = proj(base + 2 * B_QKV, B_QKV).astype(BF16)
    base += 3 * B_QKV
    gate_w = gates_ref.shape[1]
    for c in range(0, gate_w, GATE_CHUNK):
        gates_ref[:, c:c + GATE_CHUNK] = (proj(base + c, GATE_CHUNK) * 0.5).astype(BF16)


def _in_proj(x, g, w):
    batch, seq, d_model = x.shape
    tile = IN_PROJ_TILE
    n_in = w.shape[1]
    gate_w = n_in - 3 * A_QKV - 3 * B_QKV
    out_shape, out_specs = [], []
    for _, dil in DILATED_GROUPS:
        for _ in range(3):
            out_shape.append(jax.ShapeDtypeStruct((batch, seq // dil, dil * GROUP_W), BF16))
            out_specs.append(pl.BlockSpec((None, tile // dil, dil * GROUP_W), lambda b, i: (b, i, 0)))
    for _ in range(3):
        out_shape.append(jax.ShapeDtypeStruct((batch, seq, B_QKV), BF16))
        out_specs.append(pl.BlockSpec((None, tile, B_QKV), lambda b, i: (b, i, 0)))
    out_shape.append(jax.ShapeDtypeStruct((batch, seq, gate_w), BF16))
    out_specs.append(pl.BlockSpec((None, tile, gate_w), lambda b, i: (b, i, 0)))
    return pl.pallas_call(
        _in_proj_body,
        grid=(batch, seq // tile),
        in_specs=[pl.BlockSpec((None, tile, d_model), lambda b, i: (b, i, 0)),
                  _resident((1, d_model)),
                  _resident((d_model, n_in))],
        out_specs=out_specs,
        out_shape=out_shape,
        scratch_shapes=[pltpu.VMEM((GROUP_W // LANES, tile, LANES), F32)],
        compiler_params=_compiler_params(("parallel", "parallel")),
        name="in_proj",
    )(x, g, w)


def _dilated_body(q_ref, k_ref, v_ref, kp_ref, vp_ref, bias_ref, o_ref, lse_ref, *,
                  blocks, residues):
    heads = HEADS_PER_GROUP
    missing_cols = jnp.where(pl.program_id(2) == 0, BLOCK, 0)
    lane_head = lax.broadcasted_iota(jnp.int32, (BLOCK, GROUP_W), 1) // HEAD_DIM
    key_col = lax.broadcasted_iota(jnp.int32, (heads * BLOCK, 2 * BLOCK), 1)
    bias = bias_ref[...]

    def keys_values(res):
        lanes = slice(res * GROUP_W, (res + 1) * GROUP_W)
        k_all = jnp.concatenate([kp_ref[:, lanes], k_ref[:, lanes]], axis=0)
        v_all = jnp.concatenate([vp_ref[:, lanes], v_ref[:, lanes]], axis=0)
        v_lane_head = lax.broadcasted_iota(jnp.int32, v_all.shape, 1) // HEAD_DIM
        v_heads = [jnp.where(v_lane_head == hd, v_all, jnp.zeros_like(v_all))
                   for hd in range(heads)]
        return k_all, v_heads

    def scores(res, nb, k_all):
        q = q_ref[nb * BLOCK:(nb + 1) * BLOCK, res * GROUP_W:(res + 1) * GROUP_W]
        q_stack = jnp.concatenate(
            [jnp.where(lane_head == hd, q, jnp.zeros_like(q)) for hd in range(heads)], axis=0)
        s = lax.dot_general(q_stack, k_all[nb * BLOCK:(nb + 2) * BLOCK], (((1,), (1,)), ((), ())),
                            preferred_element_type=F32) + bias
        if nb == 0:
            s = jnp.where(key_col < missing_cols, MASKED, s)
        return s

    def finish(res, nb, s, v_heads):
        mx = jnp.max(s, axis=-1, keepdims=True)
        e = jnp.exp2(s - mx)
        den = jnp.sum(e, axis=-1, keepdims=True)
        p = (e * (1.0 / den)).astype(BF16)
        p_cat = jnp.concatenate([p[hd * BLOCK:(hd + 1) * BLOCK] for hd in range(heads)], axis=1)
        v_stack = jnp.concatenate([vh[nb * BLOCK:(nb + 2) * BLOCK] for vh in v_heads], axis=0)
        o = jnp.dot(p_cat, v_stack, preferred_element_type=F32)
        lse = mx + jnp.log2(den)
        lse_full = jnp.broadcast_to(lse[:BLOCK], (BLOCK, GROUP_W))
        for hd in range(1, heads):
            lse_full = jnp.where(lane_head == hd, lse[hd * BLOCK:(hd + 1) * BLOCK], lse_full)
        rows = slice(nb * BLOCK, (nb + 1) * BLOCK)
        lanes = slice(res * GROUP_W, (res + 1) * GROUP_W)
        o_ref[rows, lanes] = o.astype(o_ref.dtype)
        lse_ref[rows, lanes] = lse_full

    units = [(res, nb) for res in range(residues) for nb in range(blocks)]
    kv = {res: keys_values(res) for res in range(residues)}
    s_next = scores(*units[0], kv[units[0][0]][0])
    for i, (res, nb) in enumerate(units):
        s = s_next
        if i + 1 < len(units):
            s_next = scores(*units[i + 1], kv[units[i + 1][0]][0])
        finish(res, nb, s, kv[res][1])


def _dilated_group(q, k, v, bias_band, dil):
    batch, sub_seq, _ = q.shape
    blocks = min(DILATED_BLOCKS_PER_STEP, sub_seq // BLOCK)
    residues = min(DILATED_BLOCKS_PER_STEP // blocks, dil)
    tile = blocks * BLOCK
    width = residues * GROUP_W
    cur = pl.BlockSpec((None, tile, width), lambda b, r, n: (b, n, r))
    prev = pl.BlockSpec((None, BLOCK, width),
                        lambda b, r, n: (b, jnp.maximum(n * blocks - 1, 0), r))
    return pl.pallas_call(
        functools.partial(_dilated_body, blocks=blocks, residues=residues),
        grid=(batch, dil // residues, sub_seq // tile),
        in_specs=[cur, cur, cur, prev, prev, _resident(bias_band.shape)],
        out_specs=[cur, cur],
        out_shape=[jax.ShapeDtypeStruct(q.shape, BF16), jax.ShapeDtypeStruct(q.shape, F32)],
        compiler_params=_compiler_params(("parallel", "parallel", "parallel")),
        name=f"dilated_d{dil}",
    )(q, k, v, k, v, bias_band)


def _bias_band(rel_bias, group, dil, n_back):
    heads = slice(group * HEADS_PER_GROUP, (group + 1) * HEADS_PER_GROUP)
    buckets = _t5_bucket(np.arange(n_back + 1) * dil)
    bias = rel_bias[buckets][:, heads].T.astype(F32) * LOG2E
    period = 2 * BLOCK + 1
    j = BLOCK - np.arange(period)
    local = (j >= 0) & (j <= n_back)
    diag = jnp.where(local[None], bias[:, np.clip(j, 0, n_back)], MASKED)
    band = jnp.tile(diag, (1, BLOCK))[:, :BLOCK * 2 * BLOCK]
    return band.reshape(HEADS_PER_GROUP * BLOCK, 2 * BLOCK)


class _StickStep(NamedTuple):
    q: jax.Array
    k: jax.Array
    v: jax.Array
    mask: Optional[jax.Array]
    carry: Any
    after: Optional[int]


def _neg_abs(x):
    bits = lax.bitcast_convert_type(x, jnp.uint32) | jnp.uint32(0x80000000)
    return lax.bitcast_convert_type(bits, F32)


def _stick_steps(steps, upper):
    z = [lax.dot_general(st.q, st.k, (((1,), (1,)), ((), ())), preferred_element_type=F32)
         for st in steps]
    z = [s if st.mask is None else jnp.where(st.mask, s, MASKED) for s, st in zip(z, steps)]
    soft = [jnp.maximum(s, 0.0) + jnp.log2(1.0 + jnp.exp2(_neg_abs(s))) for s in z]
    log_beta = [s - p for s, p in zip(z, soft)]
    tail = [jnp.dot(p.astype(BF16), upper, preferred_element_type=F32) for p in soft]
    sums = [jnp.sum(p, axis=-1, keepdims=True) for p in soft]
    outs = []
    for i, st in enumerate(steps):
        carry = st.carry if st.after is None else st.carry + sums[st.after]
        w = jnp.exp2(log_beta[i] - (tail[i] + carry))
        outs.append(w.astype(BF16))
    outs = [jnp.dot(w, st.v, preferred_element_type=F32) for w, st in zip(outs, steps)]
    return outs, sums


def _stick_body(q_ref, k_ref, v_ref, upper_ref, o_ref, acc_ref, carry_ref):
    tile = upper_ref.shape[0]
    q_tiles = q_ref.shape[0] // tile
    heads = LANES // HEAD_DIM
    first_tile = pl.program_id(2) * q_tiles
    lane_head = lax.broadcasted_iota(jnp.int32, (tile, LANES), 1) // HEAD_DIM
    row = lax.broadcasted_iota(jnp.int32, (tile, tile), 0)
    col = lax.broadcasted_iota(jnp.int32, (tile, tile), 1)
    strictly_before = col < row
    upper = upper_ref[...]
    chains = []
    for t in range(q_tiles):
        q = q_ref[t * tile:(t + 1) * tile, :]
        for hd in range(heads):
            chains.append((t, hd, jnp.where(lane_head == hd, q, jnp.zeros_like(q))))

    def key_value(kt):
        start = pl.multiple_of(jnp.maximum(kt, 0) * tile, tile)
        v = v_ref[pl.ds(start, tile), :]
        return k_ref[pl.ds(start, tile), :], jnp.where(kt >= 0, v, jnp.zeros_like(v))

    diag = [key_value(first_tile + t) for t in range(q_tiles)]
    prev = [key_value(first_tile + t - 1) for t in range(q_tiles)]
    steps = []
    for t, hd, qh in chains:
        steps.append(_StickStep(qh, *diag[t], strictly_before, 0.0, None))
        steps.append(_StickStep(qh, *prev[t], None, 0.0, len(steps) - 1))
    outs, sums = _stick_steps(steps, upper)
    for c in range(len(chains)):
        acc_ref[c] = outs[2 * c] + outs[2 * c + 1]
        carry_ref[c] = sums[2 * c] + sums[2 * c + 1]

    def any_weight_left():
        return (jnp.min(carry_ref[...]) < F32_EXP2_UNDERFLOW).astype(jnp.int32)

    def more_tiles(state):
        j, live = state
        return jnp.logical_and(j < first_tile + q_tiles - 1, live > 0)

    def earlier(state):
        j, _ = state
        tiles = [key_value(first_tile + t - 1 - j) for t in range(q_tiles)]
        outs, sums = _stick_steps(
            [_StickStep(qh, *tiles[t], None, carry_ref[c], None)
             for c, (t, _, qh) in enumerate(chains)], upper)
        for c in range(len(chains)):
            acc_ref[c] += outs[c]
            carry_ref[c] += sums[c]
        return j + 1, any_weight_left()

    lax.while_loop(more_tiles, earlier, (jnp.int32(1), any_weight_left()))
    for t in range(q_tiles):
        out = acc_ref[t * heads]
        for hd in range(1, heads):
            out = jnp.where(lane_head == hd, acc_ref[t * heads + hd], out)
        o_ref[t * tile:(t + 1) * tile, :] = out.astype(o_ref.dtype)


def _stick_breaking(q, k, v):
    batch, seq, width = q.shape
    tile = STICK_TILE
    rows = STICK_Q_TILES * tile
    chains = STICK_Q_TILES * (LANES // HEAD_DIM)
    j = np.arange(tile)
    upper = jnp.asarray((j[:, None] > j[None, :]), BF16)
    return pl.pallas_call(
        _stick_body,
        grid=(batch, width // LANES, seq // rows),
        in_specs=[pl.BlockSpec((None, rows, LANES), lambda b, p, i: (b, i, p)),
                  pl.BlockSpec((None, seq, LANES), lambda b, p, i: (b, 0, p)),
                  pl.BlockSpec((None, seq, LANES), lambda b, p, i: (b, 0, p)),
                  _resident((tile, tile))],
        out_specs=pl.BlockSpec((None, rows, LANES), lambda b, p, i: (b, i, p)),
        out_shape=jax.ShapeDtypeStruct(q.shape, BF16),
        scratch_shapes=[pltpu.VMEM((chains, tile, LANES), F32),
                        pltpu.VMEM((chains, tile, 1), F32)],
        compiler_params=_compiler_params(("parallel", "parallel", "parallel")),
        name="stick_breaking",
    )(q, k, v, upper)


def _mix_tile(x_ref, gates_ref, yb_ref, group_refs, wa_ref, wb_ref, wo_ref, g_ref, slab_ref,
              mid_ref):
    tile, d_model = x_ref.shape

    def natural_order(ref, dil):
        if dil == 1:
            return ref[...].astype(F32)
        rows = tile // dil
        slabs = GROUP_W // LANES
        if dil <= SUBLANES:
            for r in range(dil):
                for s in range(slabs):
                    lane0 = r * GROUP_W + s * LANES
                    slab_ref[s, pl.ds(r, rows, stride=dil), :] = (
                        ref[:, lane0:lane0 + LANES].astype(F32))
        else:
            outer = dil // STRIDE_STEP
            for hi in range(outer):
                for lo in range(STRIDE_STEP):
                    for s in range(slabs):
                        lane0 = (hi * STRIDE_STEP + lo) * GROUP_W + s * LANES
                        mid_ref[lo * slabs + s, pl.ds(hi, rows, stride=outer), :] = (
                            ref[:, lane0:lane0 + LANES].astype(F32))
            for lo in range(STRIDE_STEP):
                for s in range(slabs):
                    slab_ref[s, pl.ds(lo, rows * outer, stride=STRIDE_STEP), :] = (
                        mid_ref[lo * slabs + s])
        return jnp.concatenate([slab_ref[s] for s in range(slabs)], axis=-1)

    lses = [natural_order(group_refs[2 * g + 1], dil) for g, (_, dil) in enumerate(DILATED_GROUPS)]
    top = functools.reduce(jnp.maximum, lses)
    num, den = 0.0, 0.0
    for g, (_, dil) in enumerate(DILATED_GROUPS):
        w = jnp.exp2(lses[g] - top)
        num = num + w * natural_order(group_refs[2 * g], dil)
        den = den + w
    ya = num / den

    branch_a = jnp.dot(ya.astype(BF16), wa_ref[...], preferred_element_type=F32)
    branch_b = jnp.dot(yb_ref[...], wb_ref[...], preferred_element_type=F32)
    gate_a = gates_ref[:, :d_model].astype(F32)
    gate_b = gates_ref[:, d_model:].astype(F32)
    merged = 0.5 * ((1.0 + jnp.tanh(gate_a)) * branch_a + (1.0 + jnp.tanh(gate_b)) * branch_b)
    mixed = jnp.dot(merged.astype(BF16), wo_ref[...], preferred_element_type=F32)
    return x_ref[...] + _rms_norm(mixed, g_ref[...])


def _ffn_tile(x_mid, g_pre_ref, w_up_ref, cw_ref, cb_ref, w_down_ref, g_post_ref, out_ref,
              halo_ref, u_ref, act_ref, slab_ref):
    tile, d_model = x_mid.shape
    d_ff = w_down_ref.shape[0]
    seg = tile // SUBLANES
    n_slabs = d_model // LANES
    halo_rows = (CONV_WIDTH - 1) * SUBLANES

    @pl.when(pl.program_id(1) == 0)
    def _():
        halo_ref[...] = jnp.zeros_like(halo_ref)

    for c in range(n_slabs):
        for s in range(SUBLANES):
            slab_ref[c, pl.ds(s, seg, stride=SUBLANES), :] = (
                x_mid[s * seg:(s + 1) * seg, c * LANES:(c + 1) * LANES])
    x = jnp.concatenate([slab_ref[c] for c in range(n_slabs)], axis=-1)
    h = _rms_norm(x, g_pre_ref[...]).astype(BF16)
    sublane = lax.broadcasted_iota(jnp.int32, (SUBLANES, FF_CHUNK), 0)

    def up(c):
        return [jnp.dot(h, w_up_ref[:, col:col + FF_CHUNK], preferred_element_type=F32)
                for col in (c * FF_CHUNK, d_ff + c * FF_CHUNK)]

    def conv(u, col, slot):
        cols = slice(col, col + FF_CHUNK)
        last = u[tile - halo_rows:, :]
        for k in range(CONV_WIDTH - 1):
            group = slice(k * SUBLANES, (k + 1) * SUBLANES)
            mixed = jnp.where(sublane == SUBLANES - 1, halo_ref[group, cols], last[group, :])
            u_ref[slot, group, :] = pltpu.roll(mixed, 1, axis=0)
        u_ref[slot, halo_rows:, :] = u
        halo_ref[:, cols] = last
        y = cb_ref[:, cols]
        for t in range(CONV_WIDTH):
            start = t * SUBLANES
            y = y + cw_ref[t:t + 1, cols] * u_ref[slot, start:start + tile, :]
        return y

    n_chunks = d_ff // FF_CHUNK
    u_next = up(0)
    for c in range(n_chunks):
        u_gate, u_val = u_next
        if c + 1 < n_chunks:
            u_next = up(c + 1)
        gate = conv(u_gate, c * FF_CHUNK, 0)
        val = conv(u_val, d_ff + c * FF_CHUNK, 1)
        act_ref[:, c * FF_CHUNK:(c + 1) * FF_CHUNK] = (
            jax.nn.gelu(gate, approximate=True) * val).astype(BF16)
    part_rows = tile // DOWN_PARTS
    groups = part_rows // SUBLANES
    down = [jnp.dot(act_ref[p * part_rows:(p + 1) * part_rows, :], w_down_ref[...],
                    preferred_element_type=F32) for p in range(DOWN_PARTS)]
    for p, acc in enumerate(down):
        rows = slice(p * part_rows, (p + 1) * part_rows)
        y = x[rows, :] + _rms_norm(acc, g_post_ref[...])
        for c in range(n_slabs):
            slab_ref[c, rows, :] = y[:, c * LANES:(c + 1) * LANES]
        for c in range(n_slabs):
            for s in range(SUBLANES):
                out_ref[s * seg + p * groups:s * seg + (p + 1) * groups,
                        c * LANES:(c + 1) * LANES] = (
                    slab_ref[c, pl.ds(p * part_rows + s, groups, stride=SUBLANES), :])


def _post_attention_body(x_ref, gates_ref, yb_ref, *refs):
    group_refs = refs[:2 * N_GROUPS]
    (wa_ref, wb_ref, wo_ref, g_post_mix_ref, g_pre_ref, w_up_ref, cw_ref, cb_ref, w_down_ref,
     g_post_ref, out_ref, group_slab_ref, group_mid_ref, halo_ref, u_ref, act_ref,
     slab_ref) = refs[2 * N_GROUPS:]
    x_mid = _mix_tile(x_ref, gates_ref, yb_ref, group_refs, wa_ref, wb_ref, wo_ref,
                      g_post_mix_ref, group_slab_ref, group_mid_ref)
    _ffn_tile(x_mid, g_pre_ref, w_up_ref, cw_ref, cb_ref, w_down_ref, g_post_ref, out_ref,
              halo_ref, u_ref, act_ref, slab_ref)


def _post_attention(x, gates, yb, groups, mix_consts, ffn_consts):
    batch, seq, d_model = x.shape
    tile = TOKEN_TILE
    d_ff = ffn_consts[4].shape[0]
    row_spec = lambda width: pl.BlockSpec((None, tile, width), lambda b, i: (b, i, 0))
    in_specs = [row_spec(d_model), row_spec(gates.shape[-1]), row_spec(yb.shape[-1])]
    operands = [x, gates, yb]
    for (o, lse), (_, dil) in zip(groups, DILATED_GROUPS):
        for arr in (o, lse):
            in_specs.append(pl.BlockSpec((None, tile // dil, dil * GROUP_W), lambda b, i: (b, i, 0)))
            operands.append(arr)
    for arr in (*mix_consts, *ffn_consts):
        in_specs.append(_resident(arr.shape))
        operands.append(arr)
    halo_rows = (CONV_WIDTH - 1) * SUBLANES
    return pl.pallas_call(
        _post_attention_body,
        grid=(batch, seq // tile),
        in_specs=in_specs,
        out_specs=row_spec(d_model),
        out_shape=jax.ShapeDtypeStruct(x.shape, F32),
        scratch_shapes=[pltpu.VMEM((GROUP_W // LANES, tile, LANES), F32),
                        pltpu.VMEM((STRIDE_STEP * GROUP_W // LANES, tile // STRIDE_STEP, LANES), F32),
                        pltpu.VMEM((halo_rows, 2 * d_ff), F32),
                        pltpu.VMEM((2, halo_rows + tile, FF_CHUNK), F32),
                        pltpu.VMEM((tile, d_ff), BF16),
                        pltpu.VMEM((d_model // LANES, tile, LANES), F32)],
        compiler_params=_compiler_params(("arbitrary", "arbitrary")),
        name="post_attention",
    )(*operands)


def kernel(x, g_pre_mix, w_in, rel_bias, w_branch_a, w_branch_b, w_out, g_post_mix, g_pre_ffn,
           w_up, conv_w, conv_b, w_down, g_post_ffn):
    depth = w_in.shape[0]
    for l in range(depth):
        proj = _in_proj(x, g_pre_mix[l][None], w_in[l].astype(BF16))
        qb, kb, vb, gates = proj[3 * N_GROUPS:]
        groups = []
        for g, (window, dil) in enumerate(DILATED_GROUPS):
            q, k, v = proj[3 * g:3 * g + 3]
            band = _bias_band(rel_bias, g, dil, window // dil)
            groups.append(_dilated_group(q, k, v, band, dil))
        yb = _stick_breaking(qb, kb, vb)
        mix_consts = (w_branch_a[l].astype(BF16), w_branch_b[l].astype(BF16),
                      w_out[l].astype(BF16), g_post_mix[l][None])
        ffn_consts = (g_pre_ffn[l][None], w_up[l].astype(BF16), conv_w[l], conv_b[l][None],
                      w_down[l].astype(BF16), g_post_ffn[l][None])
        x = _post_attention(x, gates, yb, groups, mix_consts, ffn_consts)
    return x
```

```python
import functools
import math
from typing import Any, NamedTuple, Optional

import numpy as np
import jax
import jax.numpy as jnp
from jax import lax
from jax.experimental import pallas as pl
from jax.experimental.pallas import tpu as pltpu

F32 = jnp.float32
BF16 = jnp.bfloat16

HEAD_DIM = 64
DILATED_GROUPS = ((128, 1), (512, 4), (2048, 16))
HEADS_PER_GROUP = 4
N_GROUPS = len(DILATED_GROUPS)
N_HEADS_A = HEADS_PER_GROUP * N_GROUPS
N_HEADS_B = 8
GROUP_W = HEADS_PER_GROUP * HEAD_DIM
A_QKV = N_HEADS_A * HEAD_DIM
B_QKV = N_HEADS_B * HEAD_DIM
BLOCK = 128
NUM_BUCKETS = 32
MAX_DISTANCE = 2048
CONV_WIDTH = 3
EPS = 1e-6
QK_SCALE = 1.0 / math.sqrt(HEAD_DIM)
LOG2E = math.log2(math.e)
MASKED = -1e30
F32_EXP2_UNDERFLOW = 150.1

LANES = 128
SUBLANES = 8
STRIDE_STEP = 4
VMEM_LIMIT_BYTES = 56 * 1024 * 1024

TOKEN_TILE = 512
IN_PROJ_TILE = 1024
GATE_CHUNK = 512
DILATED_BLOCKS_PER_STEP = 16
STICK_Q_TILES = 8
STICK_TILE = 256
FF_CHUNK = 256
DOWN_PARTS = 2


def _t5_bucket(dist):
    max_exact = NUM_BUCKETS // 2
    n = np.asarray(dist, dtype=np.float32)
    large = max_exact + (np.log(np.maximum(n, 1.0) / max_exact)
                         / np.log(MAX_DISTANCE / max_exact)
                         * (NUM_BUCKETS - max_exact)).astype(np.int32)
    large = np.minimum(large, NUM_BUCKETS - 1)
    return np.where(n < max_exact, n.astype(np.int32), large).astype(np.int32)


def _rms_norm(x, g):
    return x * lax.rsqrt(jnp.mean(x * x, axis=-1, keepdims=True) + EPS) * g


def _compiler_params(semantics):
    return pltpu.CompilerParams(dimension_semantics=semantics, vmem_limit_bytes=VMEM_LIMIT_BYTES)


def _resident(shape):
    return pl.BlockSpec(shape, lambda *_: (0,) * len(shape), pipeline_mode=pl.Buffered(1))


def _in_proj_body(x_ref, g_ref, w_ref, *refs):
    a_refs = refs[:3 * N_GROUPS]
    qb_ref, kb_ref, vb_ref, gates_ref, slab_ref = refs[3 * N_GROUPS:]
    tile = x_ref.shape[0]
    h = _rms_norm(x_ref[...], g_ref[...]).astype(BF16)

    def proj(col, width):
        return jnp.dot(h, w_ref[:, col:col + width], preferred_element_type=F32)

    for t in range(3):
        scale = QK_SCALE * LOG2E if t == 0 else 1.0
        for g, (_, dil) in enumerate(DILATED_GROUPS):
            acc = proj(t * A_QKV + g * GROUP_W, GROUP_W) * scale
            out_ref = a_refs[3 * g + t]
            if dil == 1:
                out_ref[...] = acc.astype(BF16)
                continue
            rows = tile // dil
            for s in range(GROUP_W // LANES):
                slab_ref[s] = acc[:, s * LANES:(s + 1) * LANES]
            for r in range(dil):
                for s in range(GROUP_W // LANES):
                    lane0 = r * GROUP_W + s * LANES
                    out_ref[:, lane0:lane0 + LANES] = (
                        slab_ref[s, pl.ds(r, rows, stride=dil), :].astype(BF16))

    base = 3 * A_QKV
    qb_ref[...] = (proj(base, B_QKV) * (QK_SCALE * LOG2E)).astype(BF16)
    kb_ref[...] = proj(base + B_QKV, B_QKV).astype(BF16)
    vb_ref[...] = proj(base + 2 * B_QKV, B_QKV).astype(BF16)
    base += 3 * B_QKV
    gate_w = gates_ref.shape[1]
    for c in range(0, gate_w, GATE_CHUNK):
        gates_ref[:, c:c + GATE_CHUNK] = (proj(base + c, GATE_CHUNK) * 0.5).astype(BF16)


def _in_proj(x, g, w):
    batch, seq, d_model = x.shape
    tile = IN_PROJ_TILE
    n_in = w.shape[1]
    gate_w = n_in - 3 * A_QKV - 3 * B_QKV
    out_shape, out_specs = [], []
    for _, dil in DILATED_GROUPS:
        for _ in range(3):
            out_shape.append(jax.ShapeDtypeStruct((batch, seq // dil, dil * GROUP_W), BF16))
            out_specs.append(pl.BlockSpec((None, tile // dil, dil * GROUP_W), lambda b, i: (b, i, 0)))
    for _ in range(3):
        out_shape.append(jax.ShapeDtypeStruct((batch, seq, B_QKV), BF16))
        out_specs.append(pl.BlockSpec((None, tile, B_QKV), lambda b, i: (b, i, 0)))
    out_shape.append(jax.ShapeDtypeStruct((batch, seq, gate_w), BF16))
    out_specs.append(pl.BlockSpec((None, tile, gate_w), lambda b, i: (b, i, 0)))
    return pl.pallas_call(
        _in_proj_body,
        grid=(batch, seq // tile),
        in_specs=[pl.BlockSpec((None, tile, d_model), lambda b, i: (b, i, 0)),
                  _resident((1, d_model)),
                  _resident((d_model, n_in))],
        out_specs=out_specs,
        out_shape=out_shape,
        scratch_shapes=[pltpu.VMEM((GROUP_W // LANES, tile, LANES), F32)],
        compiler_params=_compiler_params(("parallel", "parallel")),
        name="in_proj",
    )(x, g, w)


def _dilated_body(q_ref, k_ref, v_ref, kp_ref, vp_ref, bias_ref, o_ref, lse_ref, *,
                  blocks, residues):
    heads = HEADS_PER_GROUP
    missing_cols = jnp.where(pl.program_id(2) == 0, BLOCK, 0)
    lane_head = lax.broadcasted_iota(jnp.int32, (BLOCK, GROUP_W), 1) // HEAD_DIM
    key_col = lax.broadcasted_iota(jnp.int32, (heads * BLOCK, 2 * BLOCK), 1)

    def keys_values(res):
        lanes = slice(res * GROUP_W, (res + 1) * GROUP_W)
        k_all = jnp.concatenate([kp_ref[:, lanes], k_ref[:, lanes]], axis=0)
        v_all = jnp.concatenate([vp_ref[:, lanes], v_ref[:, lanes]], axis=0)
        v_lane_head = lax.broadcasted_iota(jnp.int32, v_all.shape, 1) // HEAD_DIM
        v_heads = [jnp.where(v_lane_head == hd, v_all, jnp.zeros_like(v_all))
                   for hd in range(heads)]
        return k_all, v_heads

    def scores(res, nb, k_all):
        q = q_ref[nb * BLOCK:(nb + 1) * BLOCK, res * GROUP_W:(res + 1) * GROUP_W]
        q_stack = jnp.concatenate(
            [jnp.where(lane_head == hd, q, jnp.zeros_like(q)) for hd in range(heads)], axis=0)
        s = lax.dot_general(q_stack, k_all[nb * BLOCK:(nb + 2) * BLOCK], (((1,), (1,)), ((), ())),
                            preferred_element_type=F32) + bias_ref[...]
        if nb == 0:
            s = jnp.where(key_col < missing_cols, MASKED, s)
        return s

    def finish(res, nb, s, v_heads):
        mx = jnp.max(s, axis=-1, keepdims=True)
        e = jnp.exp2(s - mx)
        den = jnp.sum(e, axis=-1, keepdims=True)
        p = (e * (1.0 / den)).astype(BF16)
        p_cat = jnp.concatenate([p[hd * BLOCK:(hd + 1) * BLOCK] for hd in range(heads)], axis=1)
        v_stack = jnp.concatenate([vh[nb * BLOCK:(nb + 2) * BLOCK] for vh in v_heads], axis=0)
        o = jnp.dot(p_cat, v_stack, preferred_element_type=F32)
        lse = mx + jnp.log2(den)
        lse_full = jnp.broadcast_to(lse[:BLOCK], (BLOCK, GROUP_W))
        for hd in range(1, heads):
            lse_full = jnp.where(lane_head == hd, lse[hd * BLOCK:(hd + 1) * BLOCK], lse_full)
        rows = slice(nb * BLOCK, (nb + 1) * BLOCK)
        lanes = slice(res * GROUP_W, (res + 1) * GROUP_W)
        o_ref[rows, lanes] = o.astype(o_ref.dtype)
        lse_ref[rows, lanes] = lse_full

    units = [(res, nb) for res in range(residues) for nb in range(blocks)]
    kv = {res: keys_values(res) for res in range(residues)}
    s_next = scores(*units[0], kv[units[0][0]][0])
    for i, (res, nb) in enumerate(units):
        s = s_next
        if i + 1 < len(units):
            s_next = scores(*units[i + 1], kv[units[i + 1][0]][0])
        finish(res, nb, s, kv[res][1])


def _dilated_group(q, k, v, bias_band, dil):
    batch, sub_seq, _ = q.shape
    blocks = min(DILATED_BLOCKS_PER_STEP, sub_seq // BLOCK)
    residues = min(DILATED_BLOCKS_PER_STEP // blocks, dil)
    tile = blocks * BLOCK
    width = residues * GROUP_W
    cur = pl.BlockSpec((None, tile, width), lambda b, r, n: (b, n, r))
    prev = pl.BlockSpec((None, BLOCK, width),
                        lambda b, r, n: (b, jnp.maximum(n * blocks - 1, 0), r))
    return pl.pallas_call(
        functools.partial(_dilated_body, blocks=blocks, residues=residues),
        grid=(batch, dil // residues, sub_seq // tile),
        in_specs=[cur, cur, cur, prev, prev, _resident(bias_band.shape)],
        out_specs=[cur, cur],
        out_shape=[jax.ShapeDtypeStruct(q.shape, BF16), jax.ShapeDtypeStruct(q.shape, F32)],
        compiler_params=_compiler_params(("parallel", "parallel", "parallel")),
        name=f"dilated_d{dil}",
    )(q, k, v, k, v, bias_band)


def _bias_band(rel_bias, group, dil, n_back):
    heads = slice(group * HEADS_PER_GROUP, (group + 1) * HEADS_PER_GROUP)
    buckets = _t5_bucket(np.arange(n_back + 1) * dil)
    bias = rel_bias[buckets][:, heads].T.astype(F32) * LOG2E
    period = 2 * BLOCK + 1
    j = BLOCK - np.arange(period)
    local = (j >= 0) & (j <= n_back)
    diag = jnp.where(local[None], bias[:, np.clip(j, 0, n_back)], MASKED)
    band = jnp.tile(diag, (1, BLOCK))[:, :BLOCK * 2 * BLOCK]
    return band.reshape(HEADS_PER_GROUP * BLOCK, 2 * BLOCK)


class _StickStep(NamedTuple):
    q: jax.Array
    k: jax.Array
    v: jax.Array
    mask: Optional[jax.Array]
    carry: Any
    after: Optional[int]


def _neg_abs(x):
    bits = lax.bitcast_convert_type(x, jnp.uint32) | jnp.uint32(0x80000000)
    return lax.bitcast_convert_type(bits, F32)


def _stick_steps(steps, upper):
    z = [lax.dot_general(st.q, st.k, (((1,), (1,)), ((), ())), preferred_element_type=F32)
         for st in steps]
    z = [s if st.mask is None else jnp.where(st.mask, s, MASKED) for s, st in zip(z, steps)]
    soft = [jnp.maximum(s, 0.0) + jnp.log2(1.0 + jnp.exp2(_neg_abs(s))) for s in z]
    log_beta = [s - p for s, p in zip(z, soft)]
    tail = [jnp.dot(p.astype(BF16), upper, preferred_element_type=F32) for p in soft]
    sums = [jnp.sum(p, axis=-1, keepdims=True) for p in soft]
    outs = []
    for i, st in enumerate(steps):
        carry = st.carry if st.after is None else st.carry + sums[st.after]
        w = jnp.exp2(log_beta[i] - (tail[i] + carry))
        outs.append(w.astype(BF16))
    outs = [jnp.dot(w, st.v, preferred_element_type=F32) for w, st in zip(outs, steps)]
    return outs, sums


def _stick_body(q_ref, k_ref, v_ref, upper_ref, o_ref, acc_ref, carry_ref):
    tile = upper_ref.shape[0]
    q_tiles = q_ref.shape[0] // tile
    heads = LANES // HEAD_DIM
    first_tile = pl.program_id(2) * q_tiles
    lane_head = lax.broadcasted_iota(jnp.int32, (tile, LANES), 1) // HEAD_DIM
    row = lax.broadcasted_iota(jnp.int32, (tile, tile), 0)
    col = lax.broadcasted_iota(jnp.int32, (tile, tile), 1)
    strictly_before = col < row
    upper = upper_ref[...]
    chains = []
    for t in range(q_tiles):
        q = q_ref[t * tile:(t + 1) * tile, :]
        for hd in range(heads):
            chains.append((t, hd, jnp.where(lane_head == hd, q, jnp.zeros_like(q))))

    def key_value(kt):
        start = pl.multiple_of(jnp.maximum(kt, 0) * tile, tile)
        v = v_ref[pl.ds(start, tile), :]
        return k_ref[pl.ds(start, tile), :], jnp.where(kt >= 0, v, jnp.zeros_like(v))

    diag = [key_value(first_tile + t) for t in range(q_tiles)]
    prev = [key_value(first_tile + t - 1) for t in range(q_tiles)]
    steps = []
    for t, hd, qh in chains:
        steps.append(_StickStep(qh, *diag[t], strictly_before, 0.0, None))
        steps.append(_StickStep(qh, *prev[t], None, 0.0, len(steps) - 1))
    outs, sums = _stick_steps(steps, upper)
    for c in range(len(chains)):
        acc_ref[c] = outs[2 * c] + outs[2 * c + 1]
        carry_ref[c] = sums[2 * c] + sums[2 * c + 1]

    def any_weight_left():
        return (jnp.min(carry_ref[...]) < F32_EXP2_UNDERFLOW).astype(jnp.int32)

    def more_tiles(state):
        j, live = state
        return jnp.logical_and(j < first_tile + q_tiles - 1, live > 0)

    def earlier(state):
        j, _ = state
        tiles = [key_value(first_tile + t - 1 - j) for t in range(q_tiles)]
        outs, sums = _stick_steps(
            [_StickStep(qh, *tiles[t], None, carry_ref[c], None)
             for c, (t, _, qh) in enumerate(chains)], upper)
        for c in range(len(chains)):
            acc_ref[c] += outs[c]
            carry_ref[c] += sums[c]
        return j + 1, any_weight_left()

    lax.while_loop(more_tiles, earlier, (jnp.int32(1), any_weight_left()))
    for t in range(q_tiles):
        out = acc_ref[t * heads]
        for hd in range(1, heads):
            out = jnp.where(lane_head == hd, acc_ref[t * heads + hd], out)
        o_ref[t * tile:(t + 1) * tile, :] = out.astype(o_ref.dtype)


def _stick_breaking(q, k, v):
    batch, seq, width = q.shape
    tile = STICK_TILE
    rows = STICK_Q_TILES * tile
    chains = STICK_Q_TILES * (LANES // HEAD_DIM)
    j = np.arange(tile)
    upper = jnp.asarray((j[:, None] > j[None, :]), BF16)
    return pl.pallas_call(
        _stick_body,
        grid=(batch, width // LANES, seq // rows),
        in_specs=[pl.BlockSpec((None, rows, LANES), lambda b, p, i: (b, i, p)),
                  pl.BlockSpec((None, seq, LANES), lambda b, p, i: (b, 0, p)),
                  pl.BlockSpec((None, seq, LANES), lambda b, p, i: (b, 0, p)),
                  _resident((tile, tile))],
        out_specs=pl.BlockSpec((None, rows, LANES), lambda b, p, i: (b, i, p)),
        out_shape=jax.ShapeDtypeStruct(q.shape, BF16),
        scratch_shapes=[pltpu.VMEM((chains, tile, LANES), F32),
                        pltpu.VMEM((chains, tile, 1), F32)],
        compiler_params=_compiler_params(("parallel", "parallel", "parallel")),
        name="stick_breaking",
    )(q, k, v, upper)


def _mix_tile(x_ref, gates_ref, yb_ref, group_refs, wa_ref, wb_ref, wo_ref, g_ref, slab_ref,
              mid_ref):
    tile, d_model = x_ref.shape

    def natural_order(ref, dil):
        if dil == 1:
            return ref[...].astype(F32)
        rows = tile // dil
        slabs = GROUP_W // LANES
        if dil <= SUBLANES:
            for r in range(dil):
                for s in range(slabs):
                    lane0 = r * GROUP_W + s * LANES
                    slab_ref[s, pl.ds(r, rows, stride=dil), :] = (
                        ref[:, lane0:lane0 + LANES].astype(F32))
        else:
            outer = dil // STRIDE_STEP
            for hi in range(outer):
                for lo in range(STRIDE_STEP):
                    for s in range(slabs):
                        lane0 = (hi * STRIDE_STEP + lo) * GROUP_W + s * LANES
                        mid_ref[lo * slabs + s, pl.ds(hi, rows, stride=outer), :] = (
                            ref[:, lane0:lane0 + LANES].astype(F32))
            for lo in range(STRIDE_STEP):
                for s in range(slabs):
                    slab_ref[s, pl.ds(lo, rows * outer, stride=STRIDE_STEP), :] = (
                        mid_ref[lo * slabs + s])
        return jnp.concatenate([slab_ref[s] for s in range(slabs)], axis=-1)

    lses = [natural_order(group_refs[2 * g + 1], dil) for g, (_, dil) in enumerate(DILATED_GROUPS)]
    top = functools.reduce(jnp.maximum, lses)
    num, den = 0.0, 0.0
    for g, (_, dil) in enumerate(DILATED_GROUPS):
        w = jnp.exp2(lses[g] - top)
        num = num + w * natural_order(group_refs[2 * g], dil)
        den = den + w
    ya = num / den

    branch_a = jnp.dot(ya.astype(BF16), wa_ref[...], preferred_element_type=F32)
    branch_b = jnp.dot(yb_ref[...], wb_ref[...], preferred_element_type=F32)
    gate_a = gates_ref[:, :d_model].astype(F32)
    gate_b = gates_ref[:, d_model:].astype(F32)
    merged = 0.5 * ((1.0 + jnp.tanh(gate_a)) * branch_a + (1.0 + jnp.tanh(gate_b)) * branch_b)
    mixed = jnp.dot(merged.astype(BF16), wo_ref[...], preferred_element_type=F32)
    return x_ref[...] + _rms_norm(mixed, g_ref[...])


def _ffn_tile(x_mid, g_pre_ref, w_up_ref, cw_ref, cb_ref, w_down_ref, g_post_ref, out_ref,
              halo_ref, u_ref, act_ref, slab_ref):
    tile, d_model = x_mid.shape
    d_ff = w_down_ref.shape[0]
    seg = tile // SUBLANES
    n_slabs = d_model // LANES
    halo_rows = (CONV_WIDTH - 1) * SUBLANES

    @pl.when(pl.program_id(1) == 0)
    def _():
        halo_ref[...] = jnp.zeros_like(halo_ref)

    for c in range(n_slabs):
        for s in range(SUBLANES):
            slab_ref[c, pl.ds(s, seg, stride=SUBLANES), :] = (
                x_mid[s * seg:(s + 1) * seg, c * LANES:(c + 1) * LANES])
    x = jnp.concatenate([slab_ref[c] for c in range(n_slabs)], axis=-1)
    h = _rms_norm(x, g_pre_ref[...]).astype(BF16)
    sublane = lax.broadcasted_iota(jnp.int32, (SUBLANES, FF_CHUNK), 0)

    def up(c):
        return [jnp.dot(h, w_up_ref[:, col:col + FF_CHUNK], preferred_element_type=F32)
                for col in (c * FF_CHUNK, d_ff + c * FF_CHUNK)]

    def conv(u, col, slot):
        cols = slice(col, col + FF_CHUNK)
        last = u[tile - halo_rows:, :]
        for k in range(CONV_WIDTH - 1):
            group = slice(k * SUBLANES, (k + 1) * SUBLANES)
            mixed = jnp.where(sublane == SUBLANES - 1, halo_ref[group, cols], last[group, :])
            u_ref[slot, group, :] = pltpu.roll(mixed, 1, axis=0)
        u_ref[slot, halo_rows:, :] = u
        halo_ref[:, cols] = last
        y = cb_ref[:, cols]
        for t in range(CONV_WIDTH):
            start = t * SUBLANES
            y = y + cw_ref[t:t + 1, cols] * u_ref[slot, start:start + tile, :]
        return y

    n_chunks = d_ff // FF_CHUNK
    u_next = up(0)
    for c in range(n_chunks):
        u_gate, u_val = u_next
        if c + 1 < n_chunks:
            u_next = up(c + 1)
        gate = conv(u_gate, c * FF_CHUNK, 0)
        val = conv(u_val, d_ff + c * FF_CHUNK, 1)
        act_ref[:, c * FF_CHUNK:(c + 1) * FF_CHUNK] = (
            jax.nn.gelu(gate, approximate=True) * val).astype(BF16)
    part_rows = tile // DOWN_PARTS
    groups = part_rows // SUBLANES
    down = [jnp.dot(act_ref[p * part_rows:(p + 1) * part_rows, :], w_down_ref[...],
                    preferred_element_type=F32) for p in range(DOWN_PARTS)]
    for p, acc in enumerate(down):
        rows = slice(p * part_rows, (p + 1) * part_rows)
        y = x[rows, :] + _rms_norm(acc, g_post_ref[...])
        for c in range(n_slabs):
            slab_ref[c, rows, :] = y[:, c * LANES:(c + 1) * LANES]
        for c in range(n_slabs):
            for s in range(SUBLANES):
                out_ref[s * seg + p * groups:s * seg + (p + 1) * groups,
                        c * LANES:(c + 1) * LANES] = (
                    slab_ref[c, pl.ds(p * part_rows + s, groups, stride=SUBLANES), :])


def _post_attention_body(x_ref, gates_ref, yb_ref, *refs):
    group_refs = refs[:2 * N_GROUPS]
    (wa_ref, wb_ref, wo_ref, g_post_mix_ref, g_pre_ref, w_up_ref, cw_ref, cb_ref, w_down_ref,
     g_post_ref, out_ref, group_slab_ref, group_mid_ref, halo_ref, u_ref, act_ref,
     slab_ref) = refs[2 * N_GROUPS:]
    x_mid = _mix_tile(x_ref, gates_ref, yb_ref, group_refs, wa_ref, wb_ref, wo_ref,
                      g_post_mix_ref, group_slab_ref, group_mid_ref)
    _ffn_tile(x_mid, g_pre_ref, w_up_ref, cw_ref, cb_ref, w_down_ref, g_post_ref, out_ref,
              halo_ref, u_ref, act_ref, slab_ref)


def _post_attention(x, gates, yb, groups, mix_consts, ffn_consts):
    batch, seq, d_model = x.shape
    tile = TOKEN_TILE
    d_ff = ffn_consts[4].shape[0]
    row_spec = lambda width: pl.BlockSpec((None, tile, width), lambda b, i: (b, i, 0))
    in_specs = [row_spec(d_model), row_spec(gates.shape[-1]), row_spec(yb.shape[-1])]
    operands = [x, gates, yb]
    for (o, lse), (_, dil) in zip(groups, DILATED_GROUPS):
        for arr in (o, lse):
            in_specs.append(pl.BlockSpec((None, tile // dil, dil * GROUP_W), lambda b, i: (b, i, 0)))
            operands.append(arr)
    for arr in (*mix_consts, *ffn_consts):
        in_specs.append(_resident(arr.shape))
        operands.append(arr)
    halo_rows = (CONV_WIDTH - 1) * SUBLANES
    return pl.pallas_call(
        _post_attention_body,
        grid=(batch, seq // tile),
        in_specs=in_specs,
        out_specs=row_spec(d_model),
        out_shape=jax.ShapeDtypeStruct(x.shape, F32),
        scratch_shapes=[pltpu.VMEM((GROUP_W // LANES, tile, LANES), F32),
                        pltpu.VMEM((STRIDE_STEP * GROUP_W // LANES, tile // STRIDE_STEP, LANES), F32),
                        pltpu.VMEM((halo_rows, 2 * d_ff), F32),
                        pltpu.VMEM((2, halo_rows + tile, FF_CHUNK), F32),
                        pltpu.VMEM((tile, d_ff), BF16),
                        pltpu.VMEM((d_model // LANES, tile, LANES), F32)],
        compiler_params=_compiler_params(("arbitrary", "arbitrary")),
        name="post_attention",
    )(*operands)


def kernel(x, g_pre_mix, w_in, rel_bias, w_branch_a, w_branch_b, w_out, g_post_mix, g_pre_ffn,
           w_up, conv_w, conv_b, w_down, g_post_ffn):
    depth = w_in.shape[0]
    for l in range(depth):
        proj = _in_proj(x, g_pre_mix[l][None], w_in[l].astype(BF16))
        qb, kb, vb, gates = proj[3 * N_GROUPS:]
        groups = []
        for g, (window, dil) in enumerate(DILATED_GROUPS):
            q, k, v = proj[3 * g:3 * g + 3]
            band = _bias_band(rel_bias, g, dil, window // dil)
            groups.append(_dilated_group(q, k, v, band, dil))
        yb = _stick_breaking(qb, kb, vb)
        mix_consts = (w_branch_a[l].astype(BF16), w_branch_b[l].astype(BF16),
                      w_out[l].astype(BF16), g_post_mix[l][None])
        ffn_consts = (g_pre_ffn[l][None], w_up[l].astype(BF16), conv_w[l], conv_b[l][None],
                      w_down[l].astype(BF16), g_post_ffn[l][None])
        x = _post_attention(x, gates, yb, groups, mix_consts, ffn_consts)
    return x
```
